```python
import jax, jax.numpy as jnp
from jax import lax
import numpy as np

D_MODEL = 1024
BATCH = 8
SEQ = 2048
DEPTH = 2
DEC_BATCH = 32
DEC_SEQ = 4
PAST_LEN = 8192
PAGE_SIZE = 128

HEAD_DIM = 64
MIX_WIDTH = D_MODEL
SGU_WIDTH = MIX_WIDTH // 4
SGU_GROUPS = 4
SGU_GROUP_DIM = SGU_WIDTH // SGU_GROUPS
ATT_WIDTH = MIX_WIDTH - SGU_WIDTH
N_ATT_HEADS = ATT_WIDTH // HEAD_DIM
DIL_PATTERNS = ((128, 1), (512, 4), (2048, 16))
HEADS_PER_PATTERN = N_ATT_HEADS // len(DIL_PATTERNS)
ROT_DIM = HEAD_DIM // 4
ROPE_THETA = 500000.0
CHUNK = 128
QBLOCK = 128
N_MEM = 256
MEM_HEADS = 4
MEM_HEAD_DIM = D_MODEL // MEM_HEADS
D_FF = 4 * D_MODEL
IN_COLS = 3 * ATT_WIDTH + 2 * SGU_WIDTH
DEEPNORM_ALPHA = (2 * DEPTH) ** 0.25
DEEPNORM_BETA = (8 * DEPTH) ** -0.25
LN_EPS = 1e-5
NEG = -1e30

kernel_name = 'hymba_style_dilated_attn_chunk_sgu_deepnorm_step'


def layer_norm(x, g, b):
    xf = x.astype(jnp.float32)
    mu = jnp.mean(xf, -1, keepdims=True)
    var = jnp.mean(jnp.square(xf - mu), -1, keepdims=True)
    y = (xf - mu) * lax.rsqrt(var + LN_EPS) * g.astype(jnp.float32) + b.astype(jnp.float32)
    return y.astype(x.dtype)


def rope_partial(x, pos):
    half = ROT_DIM // 2
    inv = ROPE_THETA ** (-jnp.arange(half, dtype=jnp.float32) / half)
    ang = pos.astype(jnp.float32)[:, None] * inv[None, :]
    cos = jnp.cos(ang)[None, :, None, :]
    sin = jnp.sin(ang)[None, :, None, :]
    xr = x[..., :ROT_DIM].astype(jnp.float32)
    x1, x2 = xr[..., :half], xr[..., half:]
    rot = jnp.concatenate([x1 * cos - x2 * sin, x2 * cos + x1 * sin], -1)
    return jnp.concatenate([rot.astype(x.dtype), x[..., ROT_DIM:]], -1)


def dilated_window_attention(q, k_all, v_all, n_prefix, window, dilation):
    B, T, H, Dh = q.shape
    qb = min(T, QBLOCK)
    nb = T // qb
    dist = dilation * jnp.arange(window // dilation + 1)
    scale = HEAD_DIM ** -0.5
    q_blocks = jnp.moveaxis(q.reshape(B, nb, qb, H, Dh), 1, 0)

    def block(args):
        qc, c = args
        rows = n_prefix + c * qb + jnp.arange(qb)
        idx = rows[:, None] - dist[None, :]
        valid = idx >= 0
        idx = jnp.maximum(idx, 0)
        kg = jnp.take(k_all, idx, axis=1)
        vg = jnp.take(v_all, idx, axis=1)
        s = jnp.einsum('bqhd,bqjhd->bqhj', qc, kg).astype(jnp.float32) * scale
        s = jnp.where(valid[None, :, None, :], s, NEG)
        m = jnp.max(s, -1, keepdims=True)
        pr = jnp.exp(s - m)
        den = jnp.sum(pr, -1)
        o = jnp.einsum('bqhj,bqjhd->bqhd', pr, vg.astype(jnp.float32)) / den[..., None]
        return o, m[..., 0] + jnp.log(den)

    o, lse = lax.map(block, (q_blocks, jnp.arange(nb)))
    o = jnp.moveaxis(o, 0, 1).reshape(B, T, H, Dh)
    lse = jnp.moveaxis(lse, 0, 1).reshape(B, T, H)
    return o, lse


def chunk_spatial_gate(u, v, w_s, b_s):
    B, T, _ = u.shape
    c = min(T, CHUNK)
    n = T // c
    mask = jnp.tril(jnp.ones((c, c), dtype=bool))
    w = jnp.where(mask[None], w_s[:, :c, :c], 0.0).astype(v.dtype)
    vr = v.reshape(B, n, c, SGU_GROUPS, SGU_GROUP_DIM)
    mixed = jnp.einsum('gts,bnsgd->bntgd', w, vr) + b_s[:, :c].T[None, None, :, :, None]
    return u * mixed.reshape(B, T, SGU_WIDTH).astype(u.dtype)


def decoder_layer(x, pos, past_kv, mem_k, mem_v, p):
    B, T, _ = x.shape
    proj = x @ p['w_in']
    a = ATT_WIDTH
    q = rope_partial(proj[..., :a].reshape(B, T, N_ATT_HEADS, HEAD_DIM), pos)
    k = rope_partial(proj[..., a:2 * a].reshape(B, T, N_ATT_HEADS, HEAD_DIM), pos)
    v = proj[..., 2 * a:3 * a].reshape(B, T, N_ATT_HEADS, HEAD_DIM)
    u = jax.nn.gelu(proj[..., 3 * a:3 * a + SGU_WIDTH])
    gate_in = jax.nn.gelu(proj[..., 3 * a + SGU_WIDTH:])

    outs, lses, win_rows = [], [], []
    for gi, (win, dil) in enumerate(DIL_PATTERNS):
        hs = slice(gi * HEADS_PER_PATTERN, (gi + 1) * HEADS_PER_PATTERN)
        new_kv = jnp.stack([k[:, :, hs], v[:, :, hs]], axis=2)
        if past_kv is None:
            kv_all = new_kv
            n_prefix = 0
            win_rows.append(new_kv[:, T - min(win, T):])
        else:
            past = past_kv[gi].astype(new_kv.dtype)
            kv_all = jnp.concatenate([past, new_kv], axis=1)
            n_prefix = past.shape[1]
            win_rows.append(new_kv)
        o, lse = dilated_window_attention(q[:, :, hs], kv_all[:, :, 0], kv_all[:, :, 1], n_prefix, win, dil)
        outs.append(o)
        lses.append(lse)
    mixw = jax.nn.softmax(jnp.stack(lses, 0), axis=0)
    att = jnp.concatenate([mixw[i][..., None] * outs[i] for i in range(len(DIL_PATTERNS))], axis=2)
    att = att.reshape(B, T, ATT_WIDTH).astype(x.dtype)

    gv = layer_norm(gate_in, p['sgu_ln_g'], p['sgu_ln_b'])
    sgu = chunk_spatial_gate(u, gv, p['w_spatial'], p['b_spatial'])

    mixed = jnp.concatenate([att, sgu], -1) @ p['w_mix_out']
    x = layer_norm(DEEPNORM_ALPHA * x + mixed, p['ln1_g'], p['ln1_b'])

    qx = (x @ p['w_xq']).reshape(B, T, MEM_HEADS, MEM_HEAD_DIM)
    s = jnp.einsum('bthd,bmhd->bhtm', qx, mem_k).astype(jnp.float32) * MEM_HEAD_DIM ** -0.5
    pr = jax.nn.softmax(s, axis=-1)
    ox = jnp.einsum('bhtm,bmhd->bthd', pr, mem_v.astype(jnp.float32)).reshape(B, T, D_MODEL).astype(x.dtype)
    x = layer_norm(DEEPNORM_ALPHA * x + ox @ p['w_xo'], p['ln2_g'], p['ln2_b'])

    h = jnp.square(jax.nn.relu(x @ p['w_up']))
    x = layer_norm(DEEPNORM_ALPHA * x + h @ p['w_down'], p['ln3_g'], p['ln3_b'])
    return x, win_rows, gv


def setup_inputs(seed: int = 0) -> dict:
    key = jax.random.key(seed)
    ks = jax.random.split(key, 32)
    f32 = jnp.float32

    def nrm(k, shape, scale=1.0):
        return jax.random.normal(k, shape, f32) * scale

    hg = HEADS_PER_PATTERN
    inp = {}
    inp['x_prompt'] = nrm(ks[0], (BATCH, SEQ, D_MODEL))
    inp['x_sample'] = nrm(ks[1], (DEC_BATCH, DEC_SEQ, D_MODEL))
    inp['cache_kv_w128'] = nrm(ks[2], (DEPTH, DEC_BATCH, min(128, PAST_LEN), 2, hg, HEAD_DIM))
    inp['cache_kv_w512'] = nrm(ks[3], (DEPTH, DEC_BATCH, min(512, PAST_LEN), 2, hg, HEAD_DIM))
    inp['cache_kv_w2048'] = nrm(ks[4], (DEPTH, DEC_BATCH, min(2048, PAST_LEN), 2, hg, HEAD_DIM))
    inp['cache_mem_kv'] = nrm(ks[5], (DEPTH, DEC_BATCH, N_MEM, 2, MEM_HEADS, MEM_HEAD_DIM))
    inp['mem_prompt'] = nrm(ks[6], (BATCH, N_MEM, D_MODEL))
    inp['w_in'] = nrm(ks[7], (DEPTH, D_MODEL, IN_COLS), D_MODEL ** -0.5)
    inp['sgu_ln_g'] = 1.0 + nrm(ks[8], (DEPTH, SGU_WIDTH), 0.01)
    inp['sgu_ln_b'] = nrm(ks[9], (DEPTH, SGU_WIDTH), 0.01)
    inp['w_spatial'] = nrm(ks[10], (DEPTH, SGU_GROUPS, CHUNK, CHUNK), CHUNK ** -0.5)
    inp['b_spatial'] = 1.0 + nrm(ks[11], (DEPTH, SGU_GROUPS, CHUNK), 0.01)
    inp['w_mix_out'] = nrm(ks[12], (DEPTH, MIX_WIDTH, D_MODEL), DEEPNORM_BETA * MIX_WIDTH ** -0.5)
    inp['ln1_g'] = 1.0 + nrm(ks[13], (DEPTH, D_MODEL), 0.01)
    inp['ln1_b'] = nrm(ks[14], (DEPTH, D_MODEL), 0.01)
    inp['w_xq'] = nrm(ks[15], (DEPTH, D_MODEL, D_MODEL), D_MODEL ** -0.5)
    inp['w_xkv'] = nrm(ks[16], (DEPTH, D_MODEL, 2 * D_MODEL), D_MODEL ** -0.5)
    inp['w_xo'] = nrm(ks[17], (DEPTH, D_MODEL, D_MODEL), DEEPNORM_BETA * D_MODEL ** -0.5)
    inp['ln2_g'] = 1.0 + nrm(ks[18], (DEPTH, D_MODEL), 0.01)
    inp['ln2_b'] = nrm(ks[19], (DEPTH, D_MODEL), 0.01)
    inp['w_up'] = nrm(ks[20], (DEPTH, D_MODEL, D_FF), D_MODEL ** -0.5)
    inp['w_down'] = nrm(ks[21], (DEPTH, D_FF, D_MODEL), DEEPNORM_BETA * D_FF ** -0.5)
    inp['ln3_g'] = 1.0 + nrm(ks[22], (DEPTH, D_MODEL), 0.01)
    inp['ln3_b'] = nrm(ks[23], (DEPTH, D_MODEL), 0.01)
    return inp


def reference(x_prompt, x_sample, cache_kv_w128, cache_kv_w512, cache_kv_w2048, cache_mem_kv, mem_prompt,
              w_in, sgu_ln_g, sgu_ln_b, w_spatial, b_spatial, w_mix_out, ln1_g, ln1_b,
              w_xq, w_xkv, w_xo, ln2_g, ln2_b, w_up, w_down, ln3_g, ln3_b):
    pos_p = jnp.arange(x_prompt.shape[1], dtype=jnp.int32)
    pos_s = PAST_LEN + jnp.arange(x_sample.shape[1], dtype=jnp.int32)
    hp, hs = x_prompt, x_sample
    rows_p = [[], [], []]
    rows_s = [[], [], []]
    mem_p = []
    chunk_v = []
    for l in range(DEPTH):
        p = {'w_in': w_in[l], 'sgu_ln_g': sgu_ln_g[l], 'sgu_ln_b': sgu_ln_b[l],
             'w_spatial': w_spatial[l], 'b_spatial': b_spatial[l], 'w_mix_out': w_mix_out[l],
             'ln1_g': ln1_g[l], 'ln1_b': ln1_b[l], 'w_xq': w_xq[l], 'w_xo': w_xo[l],
             'ln2_g': ln2_g[l], 'ln2_b': ln2_b[l], 'w_up': w_up[l], 'w_down': w_down[l],
             'ln3_g': ln3_g[l], 'ln3_b': ln3_b[l]}
        mkv = (mem_prompt @ w_xkv[l]).reshape(mem_prompt.shape[0], mem_prompt.shape[1], 2, MEM_HEADS, MEM_HEAD_DIM)
        mem_p.append(mkv)
        hp, wp, _ = decoder_layer(hp, pos_p, None, mkv[:, :, 0], mkv[:, :, 1], p)
        hs, ws, gvs = decoder_layer(hs, pos_s, (cache_kv_w128[l], cache_kv_w512[l], cache_kv_w2048[l]),
                                    cache_mem_kv[l, :, :, 0], cache_mem_kv[l, :, :, 1], p)
        for gi in range(len(DIL_PATTERNS)):
            rows_p[gi].append(wp[gi])
            rows_s[gi].append(ws[gi])
        chunk_v.append(gvs)
    return (hp, hs,
            jnp.stack(rows_p[0]), jnp.stack(rows_p[1]), jnp.stack(rows_p[2]), jnp.stack(mem_p),
            jnp.stack(rows_s[0]), jnp.stack(rows_s[1]), jnp.stack(rows_s[2]), jnp.stack(chunk_v))
```

```python
import functools
import math

import jax
import jax.numpy as jnp
from jax import lax
from jax.experimental import pallas as pl
from jax.experimental.pallas import tpu as pltpu

F32 = jnp.float32
BF16 = jnp.bfloat16

HEAD_DIM = 64
HEADS_PER_GROUP = 4
GROUP_WIDTH = HEAD_DIM * HEADS_PER_GROUP
DILATIONS = (1, 4, 16)
WINDOW_STEPS = 128
ROT_DIM = 16
ROPE_THETA = 500000.0
SGU_CHUNK = 128
MEM_HEADS = 4
LN_EPS = 1e-5
NEG = -1e30
LANES = 128
SUBLANES = 8
VMEM_LIMIT = 56 * 1024 * 1024


def _params(n_grid_dims):
    return pltpu.CompilerParams(
        dimension_semantics=("arbitrary",) * n_grid_dims,
        vmem_limit_bytes=VMEM_LIMIT)


def _const_spec(shape):
    nd = len(shape)
    return pl.BlockSpec(shape, lambda *_: (0,) * nd, pipeline_mode=pl.Buffered(1))


def _layer_norm(y, g, b):
    mu = jnp.mean(y, axis=-1, keepdims=True)
    yc = y - mu
    var = jnp.mean(yc * yc, axis=-1, keepdims=True)
    return yc * lax.rsqrt(var + LN_EPS) * g + b


def _gelu_tanh(x):
    return 0.5 * x * (1.0 + jnp.tanh(0.7978845608028654 * (x + 0.044715 * (x * x * x))))


def _head_half_mask(width, half):
    lane = lax.broadcasted_iota(jnp.int32, (1, width), 1)
    return (lane // HEAD_DIM) % 2 == half


def _inproj_kernel(x_ref, w_ref, cos_ref, sin_lo_ref, sin_hi_ref, g_ref, b_ref,
                   q_ref, kv0_ref, kv1_ref, kv2_ref, u_ref, gv_ref, *, att_width, sgu_width):
    xb = x_ref[0].astype(BF16)
    cos = cos_ref[...]
    sin_lo = sin_lo_ref[...]
    sin_hi = sin_hi_ref[...]

    def proj(c0, width):
        return jnp.dot(xb, w_ref[:, c0:c0 + width], preferred_element_type=F32)

    def rope(t):
        return t * cos + pltpu.roll(t, LANES - ROT_DIM // 2, 1) * sin_lo + pltpu.roll(t, ROT_DIM // 2, 1) * sin_hi

    def rope_group(t):
        return jnp.concatenate([rope(t[:, s:s + LANES]) for s in range(0, GROUP_WIDTH, LANES)], axis=1)

    for grp, kv_ref in enumerate((kv0_ref, kv1_ref, kv2_ref)):
        c = grp * GROUP_WIDTH
        q_ref[0, :, c:c + GROUP_WIDTH] = rope_group(proj(c, GROUP_WIDTH))
        kv_ref[0, :, 0:GROUP_WIDTH] = rope_group(proj(att_width + c, GROUP_WIDTH))
        kv_ref[0, :, GROUP_WIDTH:2 * GROUP_WIDTH] = proj(2 * att_width + c, GROUP_WIDTH)
    u_ref[0] = _gelu_tanh(proj(3 * att_width, sgu_width))
    gate = _gelu_tanh(proj(3 * att_width + sgu_width, sgu_width))
    gv_ref[0] = _layer_norm(gate, g_ref[...], b_ref[...])


def _inproj(x, w_in_b, tables, g, b, tm):
    bk, tk, d = x.shape
    att_width = len(DILATIONS) * GROUP_WIDTH
    sgu_width = (w_in_b.shape[1] - 3 * att_width) // 2
    cos, sin_lo, sin_hi = tables
    row = lambda w: pl.BlockSpec((1, tm, w), lambda i, j: (i, j, 0))
    tab = pl.BlockSpec((tm, LANES), lambda i, j: (j, 0))
    out_shape = [jax.ShapeDtypeStruct((bk, tk, w), F32)
                 for w in (att_width, 2 * GROUP_WIDTH, 2 * GROUP_WIDTH, 2 * GROUP_WIDTH, sgu_width, sgu_width)]
    return pl.pallas_call(
        functools.partial(_inproj_kernel, att_width=att_width, sgu_width=sgu_width),
        grid=(bk, tk // tm),
        in_specs=[row(d), _const_spec(w_in_b.shape), tab, tab, tab,
                  _const_spec((1, sgu_width)), _const_spec((1, sgu_width))],
        out_specs=[row(s.shape[2]) for s in out_shape],
        out_shape=out_shape,
        compiler_params=_params(2),
        name="inproj",
    )(x, w_in_b, cos, sin_lo, sin_hi, g, b)


def _rope_tables(pos):
    half = ROT_DIM // 2
    inv = ROPE_THETA ** (-jnp.arange(half, dtype=F32) / half)
    ang = pos.astype(F32)[:, None] * inv[None, :]
    cos, sin = jnp.cos(ang), jnp.sin(ang)
    zeros = jnp.zeros((pos.shape[0], HEAD_DIM - ROT_DIM), F32)
    zero_half = jnp.zeros_like(sin)
    cos_head = jnp.concatenate([cos, cos, zeros + 1.0], axis=1)
    lo_head = jnp.concatenate([-sin, zero_half, zeros], axis=1)
    hi_head = jnp.concatenate([zero_half, sin, zeros], axis=1)
    two = lambda t: jnp.concatenate([t, t], axis=1)
    return two(cos_head), two(lo_head), two(hi_head)


def _attn_kernel(q_ref, kv_ref, o_ref, lse_ref, q_s, kv_s, o_s, lse_s, *, dilation, seq):
    blk = WINDOW_STEPS
    n_blocks = seq // dilation // blk
    n_q, n_kv = GROUP_WIDTH // LANES, 2 * GROUP_WIDTH // LANES
    for s in range(n_q):
        q_s[s] = q_ref[0, :, s * LANES:(s + 1) * LANES]
    for s in range(n_kv):
        kv_s[s] = kv_ref[0, :, s * LANES:(s + 1) * LANES]
    x_idx = lax.broadcasted_iota(jnp.int32, (blk, 2 * blk), 0)
    k_idx = lax.broadcasted_iota(jnp.int32, (blk, 2 * blk), 1)
    band_mask = (k_idx >= x_idx) & (k_idx <= x_idx + blk)
    causal_mask = (lax.broadcasted_iota(jnp.int32, (blk, blk), 1)
                   <= lax.broadcasted_iota(jnp.int32, (blk, blk), 0))
    half_masks = [_head_half_mask(LANES, hh) for hh in range(2)]

    def rows(r, first_block, n):
        start = r + dilation * blk * first_block
        return pl.ds(start, n) if dilation == 1 else pl.ds(start, n, stride=dilation)

    for r in range(dilation):
        for c in range(n_blocks):
            q_rows = rows(r, c, blk)
            k_rows = rows(r, 0, blk) if c == 0 else rows(r, c - 1, 2 * blk)
            mask = causal_mask if c == 0 else band_mask
            for pair in range(n_q):
                qp = q_s[pair, q_rows, :].astype(BF16)
                kp = kv_s[pair, k_rows, :]
                vp = kv_s[n_q + pair, k_rows, :]
                o_pair = jnp.zeros((blk, LANES), F32)
                lse_pair = jnp.zeros((blk, LANES), F32)
                for hh in range(2):
                    kh = jnp.where(half_masks[hh], kp, 0.0).astype(BF16)
                    vh = jnp.where(half_masks[hh], vp, 0.0).astype(BF16)
                    s = lax.dot_general(qp, kh, (((1,), (1,)), ((), ())), preferred_element_type=F32)
                    s = jnp.where(mask, s, NEG)
                    m = jnp.max(s, axis=-1, keepdims=True)
                    p = jnp.exp(s - m)
                    den = jnp.sum(p, axis=-1, keepdims=True)
                    pn = (p * (1.0 / den)).astype(BF16)
                    o_pair = o_pair + jnp.dot(pn, vh, preferred_element_type=F32)
                    lse_pair = jnp.where(half_masks[hh], m + jnp.log(den), lse_pair)
                o_s[pair, q_rows, :] = o_pair
                lse_s[pair, q_rows, :] = lse_pair
    for s in range(n_q):
        o_ref[0, :, s * LANES:(s + 1) * LANES] = o_s[s]
        lse_ref[0, :, s * LANES:(s + 1) * LANES] = lse_s[s]


def _prompt_attention(q, kv, group):
    bk, tk, _ = q.shape
    out = jax.ShapeDtypeStruct((bk, tk, GROUP_WIDTH), F32)
    slabs = lambda n: pltpu.VMEM((n, tk, LANES), F32)
    return pl.pallas_call(
        functools.partial(_attn_kernel, dilation=DILATIONS[group], seq=tk),
        grid=(bk,),
        in_specs=[pl.BlockSpec((1, tk, GROUP_WIDTH), lambda i: (i, 0, group)),
                  pl.BlockSpec((1, tk, 2 * GROUP_WIDTH), lambda i: (i, 0, 0))],
        out_specs=[pl.BlockSpec((1, tk, GROUP_WIDTH), lambda i: (i, 0, 0))] * 2,
        out_shape=[out, out],
        scratch_shapes=[slabs(GROUP_WIDTH // LANES), slabs(2 * GROUP_WIDTH // LANES),
                        slabs(GROUP_WIDTH // LANES), slabs(GROUP_WIDTH // LANES)],
        compiler_params=_params(1),
        name=f"attn_d{DILATIONS[group]}",
    )(q, kv)


def _sample_attn_kernel(q_ref, kvn_ref, c0_ref, c1_ref, c2_ref, o_ref, lse_ref, *, dec_seq, pairs_per_step):
    n_cache = WINDOW_STEPS
    seg = n_cache + SUBLANES
    col = lax.broadcasted_iota(jnp.int32, (GROUP_WIDTH, LANES), 0)
    head = lax.broadcasted_iota(jnp.int32, (GROUP_WIDTH, LANES), 1)
    seg_sum = jnp.where(col // HEAD_DIM == head, 1.0, 0.0).astype(BF16)
    head_t = lax.broadcasted_iota(jnp.int32, (LANES, GROUP_WIDTH), 0)
    col_t = lax.broadcasted_iota(jnp.int32, (LANES, GROUP_WIDTH), 1)
    seg_bcast = jnp.where(col_t // HEAD_DIM == head_t, 1.0, 0.0).astype(BF16)
    cache_row = lax.broadcasted_iota(jnp.int32, (n_cache, 1), 0)
    slab_row = lax.broadcasted_iota(jnp.int32, (SUBLANES, 1), 0)
    out_row = lax.broadcasted_iota(jnp.int32, (SUBLANES, GROUP_WIDTH), 0)
    lane_head = lax.broadcasted_iota(jnp.int32, (1, GROUP_WIDTH), 1) // HEAD_DIM
    cache_refs = (c0_ref, c1_ref, c2_ref)
    seqs_per_slab = SUBLANES // dec_seq
    step = pl.program_id(0)

    def pair_body(j, carry):
        slab = pl.ds(pl.multiple_of((step * pairs_per_step + j) * SUBLANES, SUBLANES), SUBLANES)
        for g in range(len(DILATIONS)):
            gcols = slice(g * GROUP_WIDTH, (g + 1) * GROUP_WIDTH)
            q8 = q_ref[slab, gcols]
            kn = kvn_ref[slab, 2 * g * GROUP_WIDTH:(2 * g + 1) * GROUP_WIDTH]
            vn = kvn_ref[slab, (2 * g + 1) * GROUP_WIDTH:(2 * g + 2) * GROUP_WIDTH]
            o8 = jnp.zeros((SUBLANES, GROUP_WIDTH), F32)
            lse8 = jnp.zeros((SUBLANES, GROUP_WIDTH), F32)
            for e in range(seqs_per_slab):
                b_local = j * seqs_per_slab + e
                keys, vals, prods, masks = [], [], [], []
                for i in range(dec_seq):
                    qrow = q8[e * dec_seq + i:e * dec_seq + i + 1, :]
                    if g == 0:
                        kc = cache_refs[0][b_local, :, 0:GROUP_WIDTH]
                        vc = cache_refs[0][b_local, :, GROUP_WIDTH:2 * GROUP_WIDTH]
                        cache_ok = cache_row >= i
                        new_ok = (slab_row >= e * dec_seq) & (slab_row <= e * dec_seq + i)
                    else:
                        base = i * 2 * GROUP_WIDTH
                        kc = cache_refs[g][b_local, :, base:base + GROUP_WIDTH]
                        vc = cache_refs[g][b_local, :, base + GROUP_WIDTH:base + 2 * GROUP_WIDTH]
                        cache_ok = cache_row >= 0
                        new_ok = slab_row == e * dec_seq + i
                    prods += [kc * qrow, kn * qrow]
                    vals += [vc, vn]
                    masks += [cache_ok, new_ok]
                scores = jnp.dot(jnp.concatenate(prods, axis=0).astype(BF16), seg_sum,
                                 preferred_element_type=F32)
                probs, maxes = [], []
                for i in range(dec_seq):
                    sc = jnp.where(masks[2 * i], scores[i * seg:i * seg + n_cache], NEG)
                    sn = jnp.where(masks[2 * i + 1], scores[i * seg + n_cache:(i + 1) * seg], NEG)
                    m = jnp.maximum(jnp.max(sc, axis=0, keepdims=True), jnp.max(sn, axis=0, keepdims=True))
                    probs += [jnp.exp(sc - m), jnp.exp(sn - m)]
                    maxes.append(m)
                spread = jnp.dot(jnp.concatenate(probs, axis=0).astype(BF16), seg_bcast,
                                 preferred_element_type=F32)
                for i in range(dec_seq):
                    pc = spread[i * seg:i * seg + n_cache]
                    pn = spread[i * seg + n_cache:(i + 1) * seg]
                    num = (jnp.sum(pc * vals[2 * i], axis=0, keepdims=True)
                           + jnp.sum(pn * vals[2 * i + 1], axis=0, keepdims=True))
                    den = jnp.sum(pc, axis=0, keepdims=True) + jnp.sum(pn, axis=0, keepdims=True)
                    m_wide = jnp.zeros((1, GROUP_WIDTH), F32)
                    for h in range(HEADS_PER_GROUP):
                        m_wide = jnp.where(lane_head == h, maxes[i][:, h:h + 1], m_wide)
                    here = out_row == e * dec_seq + i
                    o8 = jnp.where(here, num / den, o8)
                    lse8 = jnp.where(here, m_wide + jnp.log(den), lse8)
            o_ref[slab, gcols] = o8
            lse_ref[slab, gcols] = lse8
        return carry

    lax.fori_loop(0, pairs_per_step, pair_body, 0)


def _sample_attention(q, kvn, c0, c1, c2, dec_seq, seqs_per_step):
    rows, width = q.shape
    n_seq = rows // dec_seq
    pairs_per_step = seqs_per_step * dec_seq // SUBLANES
    cache_spec = lambda c: pl.BlockSpec((seqs_per_step, c.shape[1], min(c.shape[2], dec_seq * 2 * GROUP_WIDTH)),
                                        lambda i: (i, 0, 0))
    out = jax.ShapeDtypeStruct((rows, width), F32)
    return pl.pallas_call(
        functools.partial(_sample_attn_kernel, dec_seq=dec_seq, pairs_per_step=pairs_per_step),
        grid=(n_seq // seqs_per_step,),
        in_specs=[_const_spec(q.shape), _const_spec(kvn.shape), cache_spec(c0), cache_spec(c1), cache_spec(c2)],
        out_specs=[pl.BlockSpec((rows, width), lambda i: (0, 0))] * 2,
        out_shape=[out, out],
        compiler_params=_params(1),
        name="sample_attn",
    )(q, kvn, c0, c1, c2)


def _mix_kernel(o0_ref, o1_ref, o2_ref, l0_ref, l1_ref, l2_ref, u_ref, gv_ref, x_ref,
                wsp_ref, bsp_ref, wm_ref, g_ref, b_ref, out_ref, *, tm, alpha):
    lses = [l0_ref[0], l1_ref[0], l2_ref[0]]
    top = jnp.maximum(jnp.maximum(lses[0], lses[1]), lses[2])
    es = [jnp.exp(l - top) for l in lses]
    inv = 1.0 / (es[0] + es[1] + es[2])
    y = alpha * x_ref[0]
    for g, o_ref in enumerate((o0_ref, o1_ref, o2_ref)):
        att = (o_ref[0] * (es[g] * inv)).astype(BF16)
        y = y + jnp.dot(att, wm_ref[g * GROUP_WIDTH:(g + 1) * GROUP_WIDTH, :], preferred_element_type=F32)

    t_idx = lax.broadcasted_iota(jnp.int32, (SGU_CHUNK, SGU_CHUNK), 0)
    s_idx = lax.broadcasted_iota(jnp.int32, (SGU_CHUNK, SGU_CHUNK), 1)
    w_sp = [jnp.where(s_idx <= t_idx, wsp_ref[k], 0.0).astype(BF16) for k in range(wsp_ref.shape[0])]
    sgu_width = gv_ref.shape[2]
    chunks = []
    for c in range(tm // SGU_CHUNK):
        crow = slice(c * SGU_CHUNK, (c + 1) * SGU_CHUNK)
        slabs = []
        for pair in range(sgu_width // LANES):
            gp = gv_ref[0, crow, pair * LANES:(pair + 1) * LANES]
            mixed = jnp.zeros((SGU_CHUNK, LANES), F32)
            for hh in range(2):
                gm = jnp.where(_head_half_mask(LANES, hh), gp, 0.0).astype(BF16)
                mixed = mixed + jnp.dot(w_sp[2 * pair + hh], gm, preferred_element_type=F32)
            slabs.append(mixed)
        mixed = jnp.concatenate(slabs, axis=1) + bsp_ref[...]
        chunks.append(u_ref[0, crow, :] * mixed)
    sgu = jnp.concatenate(chunks, axis=0).astype(BF16)
    y = y + jnp.dot(sgu, wm_ref[len(DILATIONS) * GROUP_WIDTH:, :], preferred_element_type=F32)
    out_ref[0] = _layer_norm(y, g_ref[...], b_ref[...])


def _mix(os, lses, group_blocks, u, gv, x, w_sp, b_sp_tile, w_mix_b, g, b, tm, alpha):
    bk, tk, d = x.shape
    sgu_width = u.shape[2]
    row = lambda w, blk=0: pl.BlockSpec((1, tm, w), lambda i, j: (i, j, blk))
    grp = [row(GROUP_WIDTH, blk) for blk in group_blocks]
    return pl.pallas_call(
        functools.partial(_mix_kernel, tm=tm, alpha=alpha),
        grid=(bk, tk // tm),
        in_specs=grp + grp + [row(sgu_width), row(sgu_width), row(d),
                              _const_spec(w_sp.shape), _const_spec(b_sp_tile.shape), _const_spec(w_mix_b.shape),
                              _const_spec((1, d)), _const_spec((1, d))],
        out_specs=row(d),
        out_shape=jax.ShapeDtypeStruct((bk, tk, d), F32),
        compiler_params=_params(2),
        name="mix",
    )(*os, *lses, u, gv, x, w_sp, b_sp_tile, w_mix_b, g, b)


def _memkv_kernel(m_ref, w_ref, out_ref):
    out_ref[0] = jnp.dot(m_ref[0].astype(BF16), w_ref[...], preferred_element_type=F32)


def _memkv(mem, w_xkv_b):
    bk, n_mem, d = mem.shape
    width = w_xkv_b.shape[1]
    return pl.pallas_call(
        _memkv_kernel,
        grid=(bk,),
        in_specs=[pl.BlockSpec((1, n_mem, d), lambda i: (i, 0, 0)), _const_spec(w_xkv_b.shape)],
        out_specs=pl.BlockSpec((1, n_mem, width), lambda i: (i, 0, 0)),
        out_shape=jax.ShapeDtypeStruct((bk, n_mem, width), F32),
        compiler_params=_params(1),
        name="memkv",
    )(mem, w_xkv_b)


def _softmax_rows(s):
    m = jnp.max(s, axis=-1, keepdims=True)
    p = jnp.exp(s - m)
    return p * (1.0 / jnp.sum(p, axis=-1, keepdims=True))


def _xattn_kernel(x_ref, mkv_ref, wq_ref, wo_ref, g_ref, b_ref, out_ref, *, alpha):
    x = x_ref[0]
    d = x.shape[1]
    hd = d // MEM_HEADS
    qx = jnp.dot(x.astype(BF16), wq_ref[...], preferred_element_type=F32)
    y = alpha * x
    for h in range(MEM_HEADS):
        qh = qx[:, h * hd:(h + 1) * hd].astype(BF16)
        kh = mkv_ref[0, :, h * hd:(h + 1) * hd].astype(BF16)
        vh = mkv_ref[0, :, d + h * hd:d + (h + 1) * hd].astype(BF16)
        s = lax.dot_general(qh, kh, (((1,), (1,)), ((), ())), preferred_element_type=F32)
        oh = jnp.dot(_softmax_rows(s).astype(BF16), vh, preferred_element_type=F32)
        y = y + jnp.dot(oh.astype(BF16), wo_ref[h * hd:(h + 1) * hd, :], preferred_element_type=F32)
    out_ref[0] = _layer_norm(y, g_ref[...], b_ref[...])


def _xattn(x, mkv, wq_b, wo_b, g, b, tm, alpha):
    bk, tk, d = x.shape
    n_mem = mkv.shape[1]
    row = pl.BlockSpec((1, tm, d), lambda i, j: (i, j, 0))
    return pl.pallas_call(
        functools.partial(_xattn_kernel, alpha=alpha),
        grid=(bk, tk // tm),
        in_specs=[row, pl.BlockSpec((1, n_mem, 2 * d), lambda i, j: (i, 0, 0)),
                  _const_spec(wq_b.shape), _const_spec(wo_b.shape), _const_spec((1, d)), _const_spec((1, d))],
        out_specs=row,
        out_shape=jax.ShapeDtypeStruct((bk, tk, d), F32),
        compiler_params=_params(2),
        name="xattn",
    )(x, mkv, wq_b, wo_b, g, b)


def _sample_xattn_kernel(x_ref, mkv_ref, wq_ref, wo_ref, g_ref, b_ref, out_ref, qx_ref, ox_ref,
                         *, dec_seq, seqs_per_step, alpha):
    step = pl.program_id(0)
    d = x_ref.shape[1]
    hd = d // MEM_HEADS
    seqs_per_slab = SUBLANES // dec_seq

    @pl.when(step == 0)
    def _():
        qx_ref[...] = jnp.dot(x_ref[...].astype(BF16), wq_ref[...], preferred_element_type=F32)

    slab_row = lax.broadcasted_iota(jnp.int32, (SUBLANES, 1), 0)
    for j in range(seqs_per_step // seqs_per_slab):
        slab = pl.ds(pl.multiple_of((step * (seqs_per_step // seqs_per_slab) + j) * SUBLANES, SUBLANES), SUBLANES)
        for h in range(MEM_HEADS):
            qh = qx_ref[slab, h * hd:(h + 1) * hd].astype(BF16)
            o8 = jnp.zeros((SUBLANES, hd), F32)
            for e in range(seqs_per_slab):
                b_local = j * seqs_per_slab + e
                kh = mkv_ref[b_local, :, h * hd:(h + 1) * hd].astype(BF16)
                vh = mkv_ref[b_local, :, d + h * hd:d + (h + 1) * hd].astype(BF16)
                s = lax.dot_general(qh, kh, (((1,), (1,)), ((), ())), preferred_element_type=F32)
                oh = jnp.dot(_softmax_rows(s).astype(BF16), vh, preferred_element_type=F32)
                o8 = jnp.where(slab_row // dec_seq == e, oh, o8)
            ox_ref[slab, h * hd:(h + 1) * hd] = o8

    @pl.when(step == pl.num_programs(0) - 1)
    def _():
        y = alpha * x_ref[...] + jnp.dot(ox_ref[...].astype(BF16), wo_ref[...], preferred_element_type=F32)
        out_ref[...] = _layer_norm(y, g_ref[...], b_ref[...])


def _sample_xattn(x, mkv, wq_b, wo_b, g, b, dec_seq, seqs_per_step, alpha):
    rows, d = x.shape
    n_seq, n_mem, _ = mkv.shape
    return pl.pallas_call(
        functools.partial(_sample_xattn_kernel, dec_seq=dec_seq, seqs_per_step=seqs_per_step, alpha=alpha),
        grid=(n_seq // seqs_per_step,),
        in_specs=[_const_spec(x.shape), pl.BlockSpec((seqs_per_step, n_mem, 2 * d), lambda i: (i, 0, 0)),
                  _const_spec(wq_b.shape), _const_spec(wo_b.shape), _const_spec((1, d)), _const_spec((1, d))],
        out_specs=pl.BlockSpec((rows, d), lambda i: (0, 0)),
        out_shape=jax.ShapeDtypeStruct((rows, d), F32),
        scratch_shapes=[pltpu.VMEM((rows, d), F32), pltpu.VMEM((rows, d), F32)],
        compiler_params=_params(1),
        name="sample_xattn",
    )(x, mkv, wq_b, wo_b, g, b)


def _mlp_kernel(x_ref, wu_ref, wd_ref, g_ref, b_ref, out_ref, *, ff_chunk, alpha):
    x = x_ref[...]
    xb = x.astype(BF16)
    y = alpha * x
    for c in range(wu_ref.shape[1] // ff_chunk):
        h = jnp.dot(xb, wu_ref[:, c * ff_chunk:(c + 1) * ff_chunk], preferred_element_type=F32)
        h = jnp.square(jnp.maximum(h, 0.0)).astype(BF16)
        y = y + jnp.dot(h, wd_ref[c * ff_chunk:(c + 1) * ff_chunk, :], preferred_element_type=F32)
    out_ref[...] = _layer_norm(y, g_ref[...], b_ref[...])


def _mlp(x, wu_b, wd_b, g, b, tm, alpha, ff_chunk=1024):
    rows, d = x.shape
    row = pl.BlockSpec((tm, d), lambda i: (i, 0))
    return pl.pallas_call(
        functools.partial(_mlp_kernel, ff_chunk=ff_chunk, alpha=alpha),
        grid=(rows // tm,),
        in_specs=[row, _const_spec(wu_b.shape), _const_spec(wd_b.shape), _const_spec((1, d)), _const_spec((1, d))],
        out_specs=row,
        out_shape=jax.ShapeDtypeStruct((rows, d), F32),
        compiler_params=_params(1),
        name="mlp",
    )(x, wu_b, wd_b, g, b)


def kernel(x_prompt, x_sample, cache_kv_w128, cache_kv_w512, cache_kv_w2048, cache_mem_kv, mem_prompt,
           w_in, sgu_ln_g, sgu_ln_b, w_spatial, b_spatial, w_mix_out, ln1_g, ln1_b,
           w_xq, w_xkv, w_xo, ln2_g, ln2_b, w_up, w_down, ln3_g, ln3_b):
    depth = w_in.shape[0]
    bp, tp, d = x_prompt.shape
    bs, ts, _ = x_sample.shape
    past_len = 8192
    alpha = float((2 * depth) ** 0.25)
    att_width = len(DILATIONS) * GROUP_WIDTH
    sgu_width = sgu_ln_g.shape[1]
    sgu_groups = w_spatial.shape[1]
    n_mem = mem_prompt.shape[1]
    rows_s = bs * ts
    assert tp % (DILATIONS[-1] * WINDOW_STEPS) == 0 and SUBLANES % ts == 0 and rows_s % SGU_CHUNK == 0
    assert d // MEM_HEADS == GROUP_WIDTH and sgu_width == GROUP_WIDTH

    col_scale = jnp.concatenate([jnp.full((att_width,), HEAD_DIM ** -0.5, F32),
                                 jnp.ones((w_in.shape[2] - att_width,), F32)])
    w_in_b = (w_in * col_scale).astype(BF16)
    w_xq_b = (w_xq * (d // MEM_HEADS) ** -0.5).astype(BF16)
    w_mix_b, w_xkv_b, w_xo_b = w_mix_out.astype(BF16), w_xkv.astype(BF16), w_xo.astype(BF16)
    w_up_b, w_down_b = w_up.astype(BF16), w_down.astype(BF16)
    vec = lambda p, l: p[l][None, :]

    tables_p = _rope_tables(jnp.arange(tp, dtype=jnp.int32))
    tables_s = _rope_tables(past_len + (jnp.arange(rows_s, dtype=jnp.int32) % ts))

    eye = jnp.eye(rows_s // ts, dtype=F32)
    tri = jnp.tril(jnp.ones((ts, ts), F32))

    hp = x_prompt
    hs = x_sample.reshape(1, rows_s, d)
    out_rows_p = [[] for _ in DILATIONS]
    out_rows_s = [[] for _ in DILATIONS]
    out_mem, out_gv = [], []
    caches = (cache_kv_w128, cache_kv_w512, cache_kv_w2048)
    for l in range(depth):
        mkv = _memkv(mem_prompt, w_xkv_b[l])
        out_mem.append(mkv.reshape(bp, n_mem, 2, MEM_HEADS, d // MEM_HEADS))
        q, kv0, kv1, kv2, u, gv = _inproj(hp, w_in_b[l], tables_p, vec(sgu_ln_g, l), vec(sgu_ln_b, l), tm=512)
        os, lses = [], []
        for g, kv in enumerate((kv0, kv1, kv2)):
            o, lse = _prompt_attention(q, kv, g)
            os.append(o)
            lses.append(lse)
            keep = min(DILATIONS[g] * WINDOW_STEPS, tp)
            out_rows_p[g].append(kv[:, tp - keep:].reshape(bp, keep, 2, HEADS_PER_GROUP, HEAD_DIM))
        b_tile = jnp.repeat(b_spatial[l][:, :SGU_CHUNK].T, sgu_width // sgu_groups, axis=1)
        hp = _mix(os, lses, (0, 0, 0), u, gv, hp, w_spatial[l], b_tile, w_mix_b[l],
                  vec(ln1_g, l), vec(ln1_b, l), tm=512, alpha=alpha)
        hp = _xattn(hp, mkv, w_xq_b[l], w_xo_b[l], vec(ln2_g, l), vec(ln2_b, l), tm=512, alpha=alpha)
        hp = _mlp(hp.reshape(bp * tp, d), w_up_b[l], w_down_b[l], vec(ln3_g, l), vec(ln3_b, l),
                  tm=512, alpha=alpha).reshape(bp, tp, d)

        q, kv0, kv1, kv2, u, gv = _inproj(hs, w_in_b[l], tables_s, vec(sgu_ln_g, l), vec(sgu_ln_b, l), tm=rows_s)
        out_gv.append(gv.reshape(bs, ts, sgu_width))
        for g, kv in enumerate((kv0, kv1, kv2)):
            out_rows_s[g].append(kv.reshape(bs, ts, 2, HEADS_PER_GROUP, HEAD_DIM))
        kvn = jnp.concatenate([kv0[0], kv1[0], kv2[0]], axis=1)
        c0 = caches[0][l].reshape(bs, WINDOW_STEPS, -1)
        c1 = caches[1][l].reshape(bs, WINDOW_STEPS, -1)
        c2 = caches[2][l].reshape(bs, WINDOW_STEPS, -1)
        o, lse = _sample_attention(q[0], kvn, c0, c1, c2, dec_seq=ts, seqs_per_step=4)
        w_sp_s = jnp.einsum("ab,gts->gatbs", eye, w_spatial[l][:, :ts, :ts] * tri).reshape(sgu_groups, rows_s, rows_s)
        b_tile_s = jnp.repeat(jnp.tile(b_spatial[l][:, :ts].T, (rows_s // ts, 1)), sgu_width // sgu_groups, axis=1)
        hs = _mix([o[None]] * 3, [lse[None]] * 3, (0, 1, 2), u, gv, hs, w_sp_s, b_tile_s, w_mix_b[l],
                  vec(ln1_g, l), vec(ln1_b, l), tm=rows_s, alpha=alpha)
        mkv_s = cache_mem_kv[l].reshape(bs, n_mem, 2 * d)
        hs2 = _sample_xattn(hs[0], mkv_s, w_xq_b[l], w_xo_b[l], vec(ln2_g, l), vec(ln2_b, l),
                            dec_seq=ts, seqs_per_step=4, alpha=alpha)
        hs = _mlp(hs2, w_up_b[l], w_down_b[l], vec(ln3_g, l), vec(ln3_b, l), tm=rows_s, alpha=alpha)[None]

    stack = lambda xs: jnp.stack(xs)
    return (hp, hs.reshape(bs, ts, d),
            stack(out_rows_p[0]), stack(out_rows_p[1]), stack(out_rows_p[2]), stack(out_mem),
            stack(out_rows_s[0]), stack(out_rows_s[1]), stack(out_rows_s[2]), stack(out_gv))
```

```python
import functools
import math

import jax
import jax.numpy as jnp
from jax import lax
from jax.experimental import pallas as pl
from jax.experimental.pallas import tpu as pltpu

F32 = jnp.float32
BF16 = jnp.bfloat16

HEAD_DIM = 64
HEADS_PER_GROUP = 4
GROUP_WIDTH = HEAD_DIM * HEADS_PER_GROUP
DILATIONS = (1, 4, 16)
WINDOW_STEPS = 128
ROT_DIM = 16
ROPE_THETA = 500000.0
SGU_CHUNK = 128
MEM_HEADS = 4
LN_EPS = 1e-5
NEG = -1e30
LANES = 128
SUBLANES = 8
VMEM_LIMIT = 56 * 1024 * 1024


def _params(n_grid_dims):
    return pltpu.CompilerParams(
        dimension_semantics=("arbitrary",) * n_grid_dims,
        vmem_limit_bytes=VMEM_LIMIT)


def _const_spec(shape):
    nd = len(shape)
    return pl.BlockSpec(shape, lambda *_: (0,) * nd, pipeline_mode=pl.Buffered(1))


def _layer_norm(y, g, b):
    mu = jnp.mean(y, axis=-1, keepdims=True)
    yc = y - mu
    var = jnp.mean(yc * yc, axis=-1, keepdims=True)
    return yc * lax.rsqrt(var + LN_EPS) * g + b


def _gelu_tanh(x):
    return 0.5 * x * (1.0 + jnp.tanh(0.7978845608028654 * (x + 0.044715 * (x * x * x))))


def _head_half_mask(width, half):
    lane = lax.broadcasted_iota(jnp.int32, (1, width), 1)
    return (lane // HEAD_DIM) % 2 == half


def _inproj_kernel(x_ref, w_ref, cos_ref, sin_lo_ref, sin_hi_ref, g_ref, b_ref,
                   q_ref, kv0_ref, kv1_ref, kv2_ref, u_ref, gv_ref, *, att_width, sgu_width):
    xb = x_ref[0].astype(BF16)
    cos = cos_ref[...]
    sin_lo = sin_lo_ref[...]
    sin_hi = sin_hi_ref[...]

    def proj(c0, width):
        return jnp.dot(xb, w_ref[:, c0:c0 + width], preferred_element_type=F32)

    def rope(t):
        return t * cos + pltpu.roll(t, LANES - ROT_DIM // 2, 1) * sin_lo + pltpu.roll(t, ROT_DIM // 2, 1) * sin_hi

    def rope_group(t):
        return jnp.concatenate([rope(t[:, s:s + LANES]) for s in range(0, GROUP_WIDTH, LANES)], axis=1)

    for grp, kv_ref in enumerate((kv0_ref, kv1_ref, kv2_ref)):
        c = grp * GROUP_WIDTH
        q_ref[0, :, c:c + GROUP_WIDTH] = rope_group(proj(c, GROUP_WIDTH))
        kv_ref[0, :, 0:GROUP_WIDTH] = rope_group(proj(att_width + c, GROUP_WIDTH))
        kv_ref[0, :, GROUP_WIDTH:2 * GROUP_WIDTH] = proj(2 * att_width + c, GROUP_WIDTH)
    u_ref[0] = _gelu_tanh(proj(3 * att_width, sgu_width))
    gate = _gelu_tanh(proj(3 * att_width + sgu_width, sgu_width))
    gv_ref[0] = _layer_norm(gate, g_ref[...], b_ref[...])


def _inproj(x, w_in_b, tables, g, b, tm):
    bk, tk, d = x.shape
    att_width = len(DILATIONS) * GROUP_WIDTH
    sgu_width = (w_in_b.shape[1] - 3 * att_width) // 2
    cos, sin_lo, sin_hi = tables
    row = lambda w: pl.BlockSpec((1, tm, w), lambda i, j: (i, j, 0))
    tab = pl.BlockSpec((tm, LANES), lambda i, j: (j, 0))
    out_shape = [jax.ShapeDtypeStruct((bk, tk, w), F32)
                 for w in (att_width, 2 * GROUP_WIDTH, 2 * GROUP_WIDTH, 2 * GROUP_WIDTH, sgu_width, sgu_width)]
    return pl.pallas_call(
        functools.partial(_inproj_kernel, att_width=att_width, sgu_width=sgu_width),
        grid=(bk, tk // tm),
        in_specs=[row(d), _const_spec(w_in_b.shape), tab, tab, tab,
                  _const_spec((1, sgu_width)), _const_spec((1, sgu_width))],
        out_specs=[row(s.shape[2]) for s in out_shape],
        out_shape=out_shape,
        compiler_params=_params(2),
        name="inproj",
    )(x, w_in_b, cos, sin_lo, sin_hi, g, b)


def _rope_tables(pos):
    half = ROT_DIM // 2
    inv = ROPE_THETA ** (-jnp.arange(half, dtype=F32) / half)
    ang = pos.astype(F32)[:, None] * inv[None, :]
    cos, sin = jnp.cos(ang), jnp.sin(ang)
    zeros = jnp.zeros((pos.shape[0], HEAD_DIM - ROT_DIM), F32)
    zero_half = jnp.zeros_like(sin)
    cos_head = jnp.concatenate([cos, cos, zeros + 1.0], axis=1)
    lo_head = jnp.concatenate([-sin, zero_half, zeros], axis=1)
    hi_head = jnp.concatenate([zero_half, sin, zeros], axis=1)
    two = lambda t: jnp.concatenate([t, t], axis=1)
    return two(cos_head), two(lo_head), two(hi_head)


def _attn_kernel(q_ref, kv_ref, o_ref, lse_ref, q_s, kv_s, o_s, lse_s, *, dilation, seq):
    blk = WINDOW_STEPS
    n_blocks = seq // dilation // blk
    n_q, n_kv = GROUP_WIDTH // LANES, 2 * GROUP_WIDTH // LANES
    for s in range(n_q):
        q_s[s] = q_ref[0, :, s * LANES:(s + 1) * LANES]
    for s in range(n_kv):
        kv_s[s] = kv_ref[0, :, s * LANES:(s + 1) * LANES]
    x_idx = lax.broadcasted_iota(jnp.int32, (blk, 2 * blk), 0)
    k_idx = lax.broadcasted_iota(jnp.int32, (blk, 2 * blk), 1)
    band_mask = (k_idx >= x_idx) & (k_idx <= x_idx + blk)
    causal_mask = (lax.broadcasted_iota(jnp.int32, (blk, blk), 1)
                   <= lax.broadcasted_iota(jnp.int32, (blk, blk), 0))
    half_masks = [_head_half_mask(LANES, hh) for hh in range(2)]

    def rows(r, first_block, n):
        start = r + dilation * blk * first_block
        return pl.ds(start, n) if dilation == 1 else pl.ds(start, n, stride=dilation)

    for r in range(dilation):
        for c in range(n_blocks):
            q_rows = rows(r, c, blk)
            k_rows = rows(r, 0, blk) if c == 0 else rows(r, c - 1, 2 * blk)
            mask = causal_mask if c == 0 else band_mask
            for pair in range(n_q):
                qp = q_s[pair, q_rows, :].astype(BF16)
                kp = kv_s[pair, k_rows, :]
                vp = kv_s[n_q + pair, k_rows, :]
                o_pair = jnp.zeros((blk, LANES), F32)
                lse_pair = jnp.zeros((blk, LANES), F32)
                for hh in range(2):
                    kh = jnp.where(half_masks[hh], kp, 0.0).astype(BF16)
                    vh = jnp.where(half_masks[hh], vp, 0.0).astype(BF16)
                    s = lax.dot_general(qp, kh, (((1,), (1,)), ((), ())), preferred_element_type=F32)
                    s = jnp.where(mask, s, NEG)
                    m = jnp.max(s, axis=-1, keepdims=True)
                    p = jnp.exp(s - m)
                    den = jnp.sum(p, axis=-1, keepdims=True)
                    pn = (p * (1.0 / den)).astype(BF16)
                    o_pair = o_pair + jnp.dot(pn, vh, preferred_element_type=F32)
                    lse_pair = jnp.where(half_masks[hh], m + jnp.log(den), lse_pair)
                o_s[pair, q_rows, :] = o_pair
                lse_s[pair, q_rows, :] = lse_pair
    for s in range(n_q):
        o_ref[0, :, s * LANES:(s + 1) * LANES] = o_s[s]
        lse_ref[0, :, s * LANES:(s + 1) * LANES] = lse_s[s]


def _prompt_attention(q, kv, group):
    bk, tk, _ = q.shape
    out = jax.ShapeDtypeStruct((bk, tk, GROUP_WIDTH), F32)
    slabs = lambda n: pltpu.VMEM((n, tk, LANES), F32)
    return pl.pallas_call(
        functools.partial(_attn_kernel, dilation=DILATIONS[group], seq=tk),
        grid=(bk,),
        in_specs=[pl.BlockSpec((1, tk, GROUP_WIDTH), lambda i: (i, 0, group)),
                  pl.BlockSpec((1, tk, 2 * GROUP_WIDTH), lambda i: (i, 0, 0))],
        out_specs=[pl.BlockSpec((1, tk, GROUP_WIDTH), lambda i: (i, 0, 0))] * 2,
        out_shape=[out, out],
        scratch_shapes=[slabs(GROUP_WIDTH // LANES), slabs(2 * GROUP_WIDTH // LANES),
                        slabs(GROUP_WIDTH // LANES), slabs(GROUP_WIDTH // LANES)],
        compiler_params=_params(1),
        name=f"attn_d{DILATIONS[group]}",
    )(q, kv)


def _sample_attn_kernel(q_ref, kvn_ref, c0_ref, c1_ref, c2_ref, att_ref, *, dec_seq):
    cache_refs = (c0_ref, c1_ref, c2_ref)
    seqs_per_slab = SUBLANES // dec_seq
    n_groups = len(DILATIONS)
    widest = DILATIONS[-1]
    slab = pl.ds(pl.multiple_of(pl.program_id(0) * SUBLANES, SUBLANES), SUBLANES)
    lane = lax.broadcasted_iota(jnp.int32, (1, LANES), 1)
    row8 = lax.broadcasted_iota(jnp.int32, (SUBLANES, LANES), 0)
    sel_rows = 2 * SUBLANES
    sel_row = lax.broadcasted_iota(jnp.int32, (sel_rows, LANES), 0)
    sel_lane = lax.broadcasted_iota(jnp.int32, (sel_rows, LANES), 1)
    query_class = lane % widest

    def one_hot(cond):
        return jnp.where(cond, 1.0, 0.0).astype(BF16)

    def spread(rows8, sel):
        padded = jnp.concatenate([rows8, jnp.zeros_like(rows8)], axis=0).astype(BF16)
        return lax.dot_general(padded, sel, (((0,), (0,)), ((), ())), preferred_element_type=F32)

    def class_fold(x, period, op):
        shift = LANES // 2
        while shift >= period:
            x = op(x, pltpu.roll(x, shift, 1))
            shift //= 2
        return x

    def valid_mask(n_tiles, cache_ok, new_ok):
        rows = SUBLANES * (n_tiles // SUBLANES + 1)
        r = lax.broadcasted_iota(jnp.int32, (rows, LANES), 0)
        p = lax.broadcasted_iota(jnp.int32, (rows, LANES), 1)
        return ((r < n_tiles) & cache_ok(p)) | ((r == n_tiles) & new_ok(p))

    def attend(qm, cache_ref, seq, head, knt, vnt, n_tiles, valid, period):
        groups = []
        for t0 in range(0, n_tiles + 1, SUBLANES):
            acc = jnp.full((SUBLANES, LANES), NEG, F32)
            for t in range(t0, min(t0 + SUBLANES, n_tiles + 1)):
                keys = knt if t == n_tiles else cache_ref[0, seq, 0, head, :, t * LANES:(t + 1) * LANES]
                acc = jnp.where(row8 == t - t0, jnp.sum(qm * keys, axis=0, keepdims=True), acc)
            groups.append(acc)
        scores = jnp.where(valid, groups[0] if len(groups) == 1 else jnp.concatenate(groups, axis=0), NEG)
        top = jnp.broadcast_to(jnp.max(scores, axis=0, keepdims=True), (SUBLANES, LANES))
        if period:
            top = class_fold(top, period, jnp.maximum)
        else:
            top = jnp.broadcast_to(jnp.max(top, axis=1, keepdims=True), (SUBLANES, LANES))
        probs = jnp.exp(scores - jnp.concatenate([top] * len(groups), axis=0))
        den = jnp.broadcast_to(jnp.sum(probs, axis=0, keepdims=True), (SUBLANES, LANES))
        num = jnp.zeros((HEAD_DIM, LANES), F32)
        for t in range(n_tiles + 1):
            vals = vnt if t == n_tiles else cache_ref[0, seq, 1, head, :, t * LANES:(t + 1) * LANES]
            num = num + vals * probs[t:t + 1, :]
        if period:
            return class_fold(num, period, jnp.add), top, class_fold(den, period, jnp.add)
        return (jnp.broadcast_to(jnp.sum(num, axis=1, keepdims=True), (HEAD_DIM, LANES)), top,
                jnp.broadcast_to(jnp.sum(den, axis=1, keepdims=True), (SUBLANES, LANES)))

    for pair in range(GROUP_WIDTH // LANES):
        mixed = [[None] * seqs_per_slab for _ in range(n_groups)]
        for e in range(seqs_per_slab):
            base = e * dec_seq
            sel_new = one_hot((sel_row == base + sel_lane) & (sel_lane < dec_seq))
            per_group = []
            for g, dil in enumerate(DILATIONS):
                off = pair * LANES
                knt = spread(kvn_ref[slab, 2 * g * GROUP_WIDTH + off:2 * g * GROUP_WIDTH + off + LANES], sel_new)
                vnt = spread(kvn_ref[slab, (2 * g + 1) * GROUP_WIDTH + off:(2 * g + 1) * GROUP_WIDTH + off + LANES], sel_new)
                q8 = q_ref[slab, g * GROUP_WIDTH + off:g * GROUP_WIDTH + off + LANES]
                n_tiles = cache_refs[g].shape[5] // LANES
                if dil == 1:
                    qms = [spread(q8, one_hot(sel_row == base + i)) for i in range(dec_seq)]
                    valids = [valid_mask(n_tiles, lambda p, i=i: p >= i, lambda p, i=i: p <= i) for i in range(dec_seq)]
                else:
                    qms = [spread(q8, one_hot((sel_row == base + sel_lane % dil) & (sel_lane % dil < dec_seq)))]
                    valids = [valid_mask(n_tiles, lambda p, dil=dil: p % dil < dec_seq, lambda p: p < dec_seq)]
                heads = []
                for hh in range(2):
                    hrows = slice(hh * HEAD_DIM, (hh + 1) * HEAD_DIM)
                    num = top = den = None
                    for i, (qm, valid) in enumerate(zip(qms, valids)):
                        n_i, t_i, d_i = attend(qm[hrows], cache_refs[g], e, 2 * pair + hh, knt[hrows], vnt[hrows],
                                               n_tiles, valid, dil if dil > 1 else 0)
                        if num is None:
                            num, top, den = n_i, t_i, d_i
                        else:
                            mine = query_class == i
                            num, top, den = jnp.where(mine, n_i, num), jnp.where(mine, t_i, top), jnp.where(mine, d_i, den)
                    heads.append([num, top, den])
                per_group.append(heads)
            for hh in range(2):
                tops = [per_group[g][hh][1] for g in range(n_groups)]
                peak = functools.reduce(jnp.maximum, tops)
                scales = [jnp.exp(t - peak) for t in tops]
                total = functools.reduce(jnp.add, [per_group[g][hh][2] * scales[g] for g in range(n_groups)])
                for g in range(n_groups):
                    per_group[g][hh] = per_group[g][hh][0] * (scales[g] / total)[0:1, :]
            for g in range(n_groups):
                mixed[g][e] = jnp.concatenate(per_group[g], axis=0)
        sel_out = one_hot(sel_lane == widest * (sel_row // dec_seq) + sel_row % dec_seq)
        for g in range(n_groups):
            both = mixed[g][0]
            for e in range(1, seqs_per_slab):
                both = jnp.where((lane >= widest * e) & (lane < widest * (e + 1)), mixed[g][e], both)
            hi = both.astype(BF16)
            lo = (both - hi.astype(F32)).astype(BF16)
            dims = (((1,), (1,)), ((), ()))
            rows = (lax.dot_general(sel_out, hi, dims, preferred_element_type=F32)
                    + lax.dot_general(sel_out, lo, dims, preferred_element_type=F32))
            att_ref[slab, g * GROUP_WIDTH + pair * LANES:g * GROUP_WIDTH + (pair + 1) * LANES] = rows[:SUBLANES]


def _sample_attention(q, kvn, caches_t, layer, dec_seq):
    rows, width = q.shape
    seqs_per_slab = SUBLANES // dec_seq
    cache_spec = lambda c: pl.BlockSpec((1, seqs_per_slab) + c.shape[2:], lambda i: (layer, i, 0, 0, 0, 0))
    return pl.pallas_call(
        functools.partial(_sample_attn_kernel, dec_seq=dec_seq),
        grid=(rows // SUBLANES,),
        in_specs=[_const_spec(q.shape), _const_spec(kvn.shape)] + [cache_spec(c) for c in caches_t],
        out_specs=pl.BlockSpec((rows, width), lambda i: (0, 0)),
        out_shape=jax.ShapeDtypeStruct((rows, width), F32),
        compiler_params=_params(1),
        name="sample_attn",
    )(q, kvn, *caches_t)


def _mix_kernel(*refs, tm, alpha, premixed):
    n_groups = len(DILATIONS)
    n_att = 1 if premixed else 2 * n_groups
    u_ref, gv_ref, x_ref, wsp_ref, bsp_ref, wm_ref, g_ref, b_ref, out_ref = refs[n_att:]
    if premixed:
        atts = [refs[0][0, :, g * GROUP_WIDTH:(g + 1) * GROUP_WIDTH] for g in range(n_groups)]
    else:
        lses = [r[0] for r in refs[n_groups:n_att]]
        top = functools.reduce(jnp.maximum, lses)
        es = [jnp.exp(l - top) for l in lses]
        inv = 1.0 / functools.reduce(jnp.add, es)
        atts = [refs[g][0] * (es[g] * inv) for g in range(n_groups)]
    y = alpha * x_ref[0]
    for g, att in enumerate(atts):
        y = y + jnp.dot(att.astype(BF16), wm_ref[g * GROUP_WIDTH:(g + 1) * GROUP_WIDTH, :],
                        preferred_element_type=F32)

    t_idx = lax.broadcasted_iota(jnp.int32, (SGU_CHUNK, SGU_CHUNK), 0)
    s_idx = lax.broadcasted_iota(jnp.int32, (SGU_CHUNK, SGU_CHUNK), 1)
    w_sp = [jnp.where(s_idx <= t_idx, wsp_ref[k], 0.0).astype(BF16) for k in range(wsp_ref.shape[0])]
    sgu_width = gv_ref.shape[2]
    chunks = []
    for c in range(tm // SGU_CHUNK):
        crow = slice(c * SGU_CHUNK, (c + 1) * SGU_CHUNK)
        slabs = []
        for pair in range(sgu_width // LANES):
            gp = gv_ref[0, crow, pair * LANES:(pair + 1) * LANES]
            mixed = jnp.zeros((SGU_CHUNK, LANES), F32)
            for hh in range(2):
                gm = jnp.where(_head_half_mask(LANES, hh), gp, 0.0).astype(BF16)
                mixed = mixed + jnp.dot(w_sp[2 * pair + hh], gm, preferred_element_type=F32)
            slabs.append(mixed)
        mixed = jnp.concatenate(slabs, axis=1) + bsp_ref[...]
        chunks.append(u_ref[0, crow, :] * mixed)
    sgu = jnp.concatenate(chunks, axis=0).astype(BF16)
    y = y + jnp.dot(sgu, wm_ref[len(DILATIONS) * GROUP_WIDTH:, :], preferred_element_type=F32)
    out_ref[0] = _layer_norm(y, g_ref[...], b_ref[...])


def _mix(att_inputs, u, gv, x, w_sp, b_sp_tile, w_mix_b, g, b, tm, alpha):
    bk, tk, d = x.shape
    sgu_width = u.shape[2]
    row = lambda w: pl.BlockSpec((1, tm, w), lambda i, j: (i, j, 0))
    return pl.pallas_call(
        functools.partial(_mix_kernel, tm=tm, alpha=alpha, premixed=len(att_inputs) == 1),
        grid=(bk, tk // tm),
        in_specs=[row(a.shape[2]) for a in att_inputs] + [
            row(sgu_width), row(sgu_width), row(d),
            _const_spec(w_sp.shape), _const_spec(b_sp_tile.shape), _const_spec(w_mix_b.shape),
            _const_spec((1, d)), _const_spec((1, d))],
        out_specs=row(d),
        out_shape=jax.ShapeDtypeStruct((bk, tk, d), F32),
        compiler_params=_params(2),
        name="mix",
    )(*att_inputs, u, gv, x, w_sp, b_sp_tile, w_mix_b, g, b)


def _memkv_kernel(m_ref, w_ref, out_ref):
    out_ref[0] = jnp.dot(m_ref[0].astype(BF16), w_ref[...], preferred_element_type=F32)


def _memkv(mem, w_xkv_b):
    bk, n_mem, d = mem.shape
    width = w_xkv_b.shape[1]
    return pl.pallas_call(
        _memkv_kernel,
        grid=(bk,),
        in_specs=[pl.BlockSpec((1, n_mem, d), lambda i: (i, 0, 0)), _const_spec(w_xkv_b.shape)],
        out_specs=pl.BlockSpec((1, n_mem, width), lambda i: (i, 0, 0)),
        out_shape=jax.ShapeDtypeStruct((bk, n_mem, width), F32),
        compiler_params=_params(1),
        name="memkv",
    )(mem, w_xkv_b)


def _softmax_rows(s):
    m = jnp.max(s, axis=-1, keepdims=True)
    p = jnp.exp(s - m)
    return p * (1.0 / jnp.sum(p, axis=-1, keepdims=True))


def _xattn_kernel(x_ref, mkv_ref, wq_ref, wo_ref, g_ref, b_ref, out_ref, *, alpha):
    x = x_ref[0]
    d = x.shape[1]
    hd = d // MEM_HEADS
    qx = jnp.dot(x.astype(BF16), wq_ref[...], preferred_element_type=F32)
    y = alpha * x
    for h in range(MEM_HEADS):
        qh = qx[:, h * hd:(h + 1) * hd].astype(BF16)
        kh = mkv_ref[0, :, h * hd:(h + 1) * hd].astype(BF16)
        vh = mkv_ref[0, :, d + h * hd:d + (h + 1) * hd].astype(BF16)
        s = lax.dot_general(qh, kh, (((1,), (1,)), ((), ())), preferred_element_type=F32)
        oh = jnp.dot(_softmax_rows(s).astype(BF16), vh, preferred_element_type=F32)
        y = y + jnp.dot(oh.astype(BF16), wo_ref[h * hd:(h + 1) * hd, :], preferred_element_type=F32)
    out_ref[0] = _layer_norm(y, g_ref[...], b_ref[...])


def _xattn(x, mkv, wq_b, wo_b, g, b, tm, alpha):
    bk, tk, d = x.shape
    n_mem = mkv.shape[1]
    row = pl.BlockSpec((1, tm, d), lambda i, j: (i, j, 0))
    return pl.pallas_call(
        functools.partial(_xattn_kernel, alpha=alpha),
        grid=(bk, tk // tm),
        in_specs=[row, pl.BlockSpec((1, n_mem, 2 * d), lambda i, j: (i, 0, 0)),
                  _const_spec(wq_b.shape), _const_spec(wo_b.shape), _const_spec((1, d)), _const_spec((1, d))],
        out_specs=row,
        out_shape=jax.ShapeDtypeStruct((bk, tk, d), F32),
        compiler_params=_params(2),
        name="xattn",
    )(x, mkv, wq_b, wo_b, g, b)


def _sample_xattn_kernel(x_ref, mkv_ref, wq_ref, wo_ref, g_ref, b_ref, out_ref, qx_ref, ox_ref,
                         *, dec_seq, seqs_per_step, alpha):
    step = pl.program_id(0)
    d = x_ref.shape[1]
    hd = d // MEM_HEADS
    seqs_per_slab = SUBLANES // dec_seq

    @pl.when(step == 0)
    def _():
        qx_ref[...] = jnp.dot(x_ref[...].astype(BF16), wq_ref[...], preferred_element_type=F32)

    n_chunks = hd // LANES
    rows_per_mem = 2 * n_chunks * MEM_HEADS
    n_mem = mkv_ref.shape[2] // rows_per_mem

    def head_matrix(b_local, kv, h):
        chunks = [mkv_ref[0, b_local, pl.ds((kv * n_chunks + c) * MEM_HEADS + h, n_mem, stride=rows_per_mem), :]
                  for c in range(n_chunks)]
        return jnp.concatenate(chunks, axis=1).astype(BF16)

    slab_row = lax.broadcasted_iota(jnp.int32, (SUBLANES, 1), 0)
    for j in range(seqs_per_step // seqs_per_slab):
        slab = pl.ds(pl.multiple_of((step * (seqs_per_step // seqs_per_slab) + j) * SUBLANES, SUBLANES), SUBLANES)
        for h in range(MEM_HEADS):
            qh = qx_ref[slab, h * hd:(h + 1) * hd].astype(BF16)
            o8 = jnp.zeros((SUBLANES, hd), F32)
            for e in range(seqs_per_slab):
                b_local = j * seqs_per_slab + e
                kh = head_matrix(b_local, 0, h)
                vh = head_matrix(b_local, 1, h)
                s = lax.dot_general(qh, kh, (((1,), (1,)), ((), ())), preferred_element_type=F32)
                oh = jnp.dot(_softmax_rows(s).astype(BF16), vh, preferred_element_type=F32)
                o8 = jnp.where(slab_row // dec_seq == e, oh, o8)
            ox_ref[slab, h * hd:(h + 1) * hd] = o8

    @pl.when(step == pl.num_programs(0) - 1)
    def _():
        y = alpha * x_ref[...] + jnp.dot(ox_ref[...].astype(BF16), wo_ref[...], preferred_element_type=F32)
        out_ref[...] = _layer_norm(y, g_ref[...], b_ref[...])


def _sample_xattn(x, mkv_flat, layer, wq_b, wo_b, g, b, dec_seq, seqs_per_step, alpha):
    rows, d = x.shape
    _, n_seq, flat_rows, _ = mkv_flat.shape
    return pl.pallas_call(
        functools.partial(_sample_xattn_kernel, dec_seq=dec_seq, seqs_per_step=seqs_per_step, alpha=alpha),
        grid=(n_seq // seqs_per_step,),
        in_specs=[_const_spec(x.shape),
                  pl.BlockSpec((1, seqs_per_step, flat_rows, LANES), lambda i: (layer, i, 0, 0)),
                  _const_spec(wq_b.shape), _const_spec(wo_b.shape), _const_spec((1, d)), _const_spec((1, d))],
        out_specs=pl.BlockSpec((rows, d), lambda i: (0, 0)),
        out_shape=jax.ShapeDtypeStruct((rows, d), F32),
        scratch_shapes=[pltpu.VMEM((rows, d), F32), pltpu.VMEM((rows, d), F32)],
        compiler_params=_params(1),
        name="sample_xattn",
    )(x, mkv_flat, wq_b, wo_b, g, b)


def _mlp_kernel(x_ref, wu_ref, wd_ref, g_ref, b_ref, out_ref, *, ff_chunk, alpha):
    x = x_ref[...]
    xb = x.astype(BF16)
    y = alpha * x
    for c in range(wu_ref.shape[1] // ff_chunk):
        h = jnp.dot(xb, wu_ref[:, c * ff_chunk:(c + 1) * ff_chunk], preferred_element_type=F32)
        h = jnp.square(jnp.maximum(h, 0.0)).astype(BF16)
        y = y + jnp.dot(h, wd_ref[c * ff_chunk:(c + 1) * ff_chunk, :], preferred_element_type=F32)
    out_ref[...] = _layer_norm(y, g_ref[...], b_ref[...])


def _mlp(x, wu_b, wd_b, g, b, tm, alpha, ff_chunk=1024):
    rows, d = x.shape
    row = pl.BlockSpec((tm, d), lambda i: (i, 0))
    return pl.pallas_call(
        functools.partial(_mlp_kernel, ff_chunk=ff_chunk, alpha=alpha),
        grid=(rows // tm,),
        in_specs=[row, _const_spec(wu_b.shape), _const_spec(wd_b.shape), _const_spec((1, d)), _const_spec((1, d))],
        out_specs=row,
        out_shape=jax.ShapeDtypeStruct((rows, d), F32),
        compiler_params=_params(1),
        name="mlp",
    )(x, wu_b, wd_b, g, b)


def kernel(x_prompt, x_sample, cache_kv_w128, cache_kv_w512, cache_kv_w2048, cache_mem_kv, mem_prompt,
           w_in, sgu_ln_g, sgu_ln_b, w_spatial, b_spatial, w_mix_out, ln1_g, ln1_b,
           w_xq, w_xkv, w_xo, ln2_g, ln2_b, w_up, w_down, ln3_g, ln3_b):
    depth = w_in.shape[0]
    bp, tp, d = x_prompt.shape
    bs, ts, _ = x_sample.shape
    past_len = 8192
    alpha = float((2 * depth) ** 0.25)
    att_width = len(DILATIONS) * GROUP_WIDTH
    sgu_width = sgu_ln_g.shape[1]
    sgu_groups = w_spatial.shape[1]
    n_mem = mem_prompt.shape[1]
    rows_s = bs * ts
    assert tp % (DILATIONS[-1] * WINDOW_STEPS) == 0 and SUBLANES % ts == 0 and rows_s % SGU_CHUNK == 0
    assert d // MEM_HEADS == GROUP_WIDTH and sgu_width == GROUP_WIDTH

    col_scale = jnp.concatenate([jnp.full((att_width,), HEAD_DIM ** -0.5, F32),
                                 jnp.ones((w_in.shape[2] - att_width,), F32)])
    w_in_b = (w_in * col_scale).astype(BF16)
    w_xq_b = (w_xq * (d // MEM_HEADS) ** -0.5).astype(BF16)
    w_mix_b, w_xkv_b, w_xo_b = w_mix_out.astype(BF16), w_xkv.astype(BF16), w_xo.astype(BF16)
    w_up_b, w_down_b = w_up.astype(BF16), w_down.astype(BF16)
    vec = lambda p, l: p[l][None, :]

    tables_p = _rope_tables(jnp.arange(tp, dtype=jnp.int32))
    tables_s = _rope_tables(past_len + (jnp.arange(rows_s, dtype=jnp.int32) % ts))

    eye = jnp.eye(rows_s // ts, dtype=F32)
    tri = jnp.tril(jnp.ones((ts, ts), F32))

    hp = x_prompt
    hs = x_sample.reshape(1, rows_s, d)
    out_rows_p = [[] for _ in DILATIONS]
    out_rows_s = [[] for _ in DILATIONS]
    out_mem, out_gv = [], []
    caches_t = [jnp.transpose(c, (0, 1, 3, 4, 5, 2)) for c in (cache_kv_w128, cache_kv_w512, cache_kv_w2048)]
    mem_chunks = d // MEM_HEADS // LANES
    mem_flat = (cache_mem_kv.reshape(depth, bs, n_mem, 2, MEM_HEADS, mem_chunks, LANES)
                .transpose(0, 1, 2, 3, 5, 4, 6).reshape(depth, bs, n_mem * 2 * mem_chunks * MEM_HEADS, LANES))
    for l in range(depth):
        mkv = _memkv(mem_prompt, w_xkv_b[l])
        out_mem.append(mkv.reshape(bp, n_mem, 2, MEM_HEADS, d // MEM_HEADS))
        q, kv0, kv1, kv2, u, gv = _inproj(hp, w_in_b[l], tables_p, vec(sgu_ln_g, l), vec(sgu_ln_b, l), tm=512)
        os, lses = [], []
        for g, kv in enumerate((kv0, kv1, kv2)):
            o, lse = _prompt_attention(q, kv, g)
            os.append(o)
            lses.append(lse)
            keep = min(DILATIONS[g] * WINDOW_STEPS, tp)
            out_rows_p[g].append(kv[:, tp - keep:].reshape(bp, keep, 2, HEADS_PER_GROUP, HEAD_DIM))
        b_tile = jnp.repeat(b_spatial[l][:, :SGU_CHUNK].T, sgu_width // sgu_groups, axis=1)
        hp = _mix(os + lses, u, gv, hp, w_spatial[l], b_tile, w_mix_b[l],
                  vec(ln1_g, l), vec(ln1_b, l), tm=512, alpha=alpha)
        hp = _xattn(hp, mkv, w_xq_b[l], w_xo_b[l], vec(ln2_g, l), vec(ln2_b, l), tm=512, alpha=alpha)
        hp = _mlp(hp.reshape(bp * tp, d), w_up_b[l], w_down_b[l], vec(ln3_g, l), vec(ln3_b, l),
                  tm=512, alpha=alpha).reshape(bp, tp, d)

        q, kv0, kv1, kv2, u, gv = _inproj(hs, w_in_b[l], tables_s, vec(sgu_ln_g, l), vec(sgu_ln_b, l), tm=rows_s)
        out_gv.append(gv.reshape(bs, ts, sgu_width))
        for g, kv in enumerate((kv0, kv1, kv2)):
            out_rows_s[g].append(kv.reshape(bs, ts, 2, HEADS_PER_GROUP, HEAD_DIM))
        kvn = jnp.concatenate([kv0[0], kv1[0], kv2[0]], axis=1)
        att = _sample_attention(q[0], kvn, caches_t, l, dec_seq=ts)
        w_sp_s = jnp.einsum("ab,gts->gatbs", eye, w_spatial[l][:, :ts, :ts] * tri).reshape(sgu_groups, rows_s, rows_s)
        b_tile_s = jnp.repeat(jnp.tile(b_spatial[l][:, :ts].T, (rows_s // ts, 1)), sgu_width // sgu_groups, axis=1)
        hs = _mix([att[None]], u, gv, hs, w_sp_s, b_tile_s, w_mix_b[l],
                  vec(ln1_g, l), vec(ln1_b, l), tm=rows_s, alpha=alpha)
        hs2 = _sample_xattn(hs[0], mem_flat, l, w_xq_b[l], w_xo_b[l], vec(ln2_g, l), vec(ln2_b, l),
                            dec_seq=ts, seqs_per_step=4, alpha=alpha)
        hs = _mlp(hs2, w_up_b[l], w_down_b[l], vec(ln3_g, l), vec(ln3_b, l), tm=rows_s, alpha=alpha)[None]

    stack = lambda xs: jnp.stack(xs)
    return (hp, hs.reshape(bs, ts, d),
            stack(out_rows_p[0]), stack(out_rows_p[1]), stack(out_rows_p[2]), stack(out_mem),
            stack(out_rows_s[0]), stack(out_rows_s[1]), stack(out_rows_s[2]), stack(out_gv))
```

```python
import functools
import math

import jax
import jax.numpy as jnp
from jax import lax
from jax.experimental import pallas as pl
from jax.experimental.pallas import tpu as pltpu

F32 = jnp.float32
BF16 = jnp.bfloat16

HEAD_DIM = 64
HEADS_PER_GROUP = 4
GROUP_WIDTH = HEAD_DIM * HEADS_PER_GROUP
DILATIONS = (1, 4, 16)
WINDOW_STEPS = 128
ROT_DIM = 16
ROPE_THETA = 500000.0
SGU_CHUNK = 128
MEM_HEADS = 4
LN_EPS = 1e-5
NEG = -1e30
LANES = 128
SUBLANES = 8
VMEM_LIMIT = 56 * 1024 * 1024


def _params(n_grid_dims):
    return pltpu.CompilerParams(
        dimension_semantics=("arbitrary",) * n_grid_dims,
        vmem_limit_bytes=VMEM_LIMIT)


def _const_spec(shape):
    nd = len(shape)
    return pl.BlockSpec(shape, lambda *_: (0,) * nd, pipeline_mode=pl.Buffered(1))


def _layer_norm(y, g, b):
    mu = jnp.mean(y, axis=-1, keepdims=True)
    yc = y - mu
    var = jnp.mean(yc * yc, axis=-1, keepdims=True)
    return yc * lax.rsqrt(var + LN_EPS) * g + b


def _gelu_tanh(x):
    return 0.5 * x * (1.0 + jnp.tanh(0.7978845608028654 * (x + 0.044715 * (x * x * x))))


def _head_half_mask(width, half):
    lane = lax.broadcasted_iota(jnp.int32, (1, width), 1)
    return (lane // HEAD_DIM) % 2 == half


def _inproj_kernel(x_ref, w_ref, cos_ref, sin_lo_ref, sin_hi_ref, g_ref, b_ref,
                   q_ref, kv0_ref, kv1_ref, kv2_ref, u_ref, gv_ref, *, att_width, sgu_width):
    xb = x_ref[0].astype(BF16)
    cos = cos_ref[...]
    sin_lo = sin_lo_ref[...]
    sin_hi = sin_hi_ref[...]

    def proj(c0, width):
        return jnp.dot(xb, w_ref[:, c0:c0 + width], preferred_element_type=F32)

    def rope(t):
        return t * cos + pltpu.roll(t, LANES - ROT_DIM // 2, 1) * sin_lo + pltpu.roll(t, ROT_DIM // 2, 1) * sin_hi

    def rope_group(t):
        return jnp.concatenate([rope(t[:, s:s + LANES]) for s in range(0, GROUP_WIDTH, LANES)], axis=1)

    for grp, kv_ref in enumerate((kv0_ref, kv1_ref, kv2_ref)):
        c = grp * GROUP_WIDTH
        q_ref[0, :, c:c + GROUP_WIDTH] = rope_group(proj(c, GROUP_WIDTH))
        kv_ref[0, :, 0:GROUP_WIDTH] = rope_group(proj(att_width + c, GROUP_WIDTH))
        kv_ref[0, :, GROUP_WIDTH:2 * GROUP_WIDTH] = proj(2 * att_width + c, GROUP_WIDTH)
    u_ref[0] = _gelu_tanh(proj(3 * att_width, sgu_width))
    gate = _gelu_tanh(proj(3 * att_width + sgu_width, sgu_width))
    gv_ref[0] = _layer_norm(gate, g_ref[...], b_ref[...])


def _inproj(x, w_in_b, tables, g, b, tm):
    bk, tk, d = x.shape
    att_width = len(DILATIONS) * GROUP_WIDTH
    sgu_width = (w_in_b.shape[1] - 3 * att_width) // 2
    cos, sin_lo, sin_hi = tables
    row = lambda w: pl.BlockSpec((1, tm, w), lambda i, j: (i, j, 0))
    tab = pl.BlockSpec((tm, LANES), lambda i, j: (j, 0))
    out_shape = [jax.ShapeDtypeStruct((bk, tk, w), F32)
                 for w in (att_width, 2 * GROUP_WIDTH, 2 * GROUP_WIDTH, 2 * GROUP_WIDTH, sgu_width, sgu_width)]
    return pl.pallas_call(
        functools.partial(_inproj_kernel, att_width=att_width, sgu_width=sgu_width),
        grid=(bk, tk // tm),
        in_specs=[row(d), _const_spec(w_in_b.shape), tab, tab, tab,
                  _const_spec((1, sgu_width)), _const_spec((1, sgu_width))],
        out_specs=[row(s.shape[2]) for s in out_shape],
        out_shape=out_shape,
        compiler_params=_params(2),
        name="inproj",
    )(x, w_in_b, cos, sin_lo, sin_hi, g, b)


def _rope_tables(pos):
    half = ROT_DIM // 2
    inv = ROPE_THETA ** (-jnp.arange(half, dtype=F32) / half)
    ang = pos.astype(F32)[:, None] * inv[None, :]
    cos, sin = jnp.cos(ang), jnp.sin(ang)
    zeros = jnp.zeros((pos.shape[0], HEAD_DIM - ROT_DIM), F32)
    zero_half = jnp.zeros_like(sin)
    cos_head = jnp.concatenate([cos, cos, zeros + 1.0], axis=1)
    lo_head = jnp.concatenate([-sin, zero_half, zeros], axis=1)
    hi_head = jnp.concatenate([zero_half, sin, zeros], axis=1)
    two = lambda t: jnp.concatenate([t, t], axis=1)
    return two(cos_head), two(lo_head), two(hi_head)


ATTN_MIX_ROWS = 256
ATTN_TILES_PER_BATCH = 8


def _attn_kernel(*refs, seq):
    n_groups = len(DILATIONS)
    q_refs, k_refs, v_refs = refs[0:n_groups], refs[n_groups:2 * n_groups], refs[2 * n_groups:3 * n_groups]
    att_refs = refs[3 * n_groups:4 * n_groups]
    o_s, lse_s = refs[4 * n_groups:]
    blk = WINDOW_STEPS
    x_idx = lax.broadcasted_iota(jnp.int32, (blk, 2 * blk), 0)
    k_idx = lax.broadcasted_iota(jnp.int32, (blk, 2 * blk), 1)
    band_mask = (k_idx >= x_idx) & (k_idx <= x_idx + blk)
    causal_mask = (lax.broadcasted_iota(jnp.int32, (blk, blk), 1)
                   <= lax.broadcasted_iota(jnp.int32, (blk, blk), 0))
    half_masks = [_head_half_mask(LANES, hh) for hh in range(2)]

    def batch(g, dil, blocks):
        def rows(r, first_block, n):
            start = r + dil * blk * first_block
            return pl.ds(start, n) if dil == 1 else pl.ds(start, n, stride=dil)

        first = blocks[0][1] == 0
        mask = causal_mask if first else band_mask
        scores, values = [], []
        for r, c in blocks:
            k_rows = rows(r, 0, blk) if first else rows(r, c - 1, 2 * blk)
            qp = q_refs[g][0, rows(r, c, blk), :]
            kp = k_refs[g][0, k_rows, :].astype(BF16)
            values.append(v_refs[g][0, k_rows, :].astype(BF16))
            for hh in range(2):
                qh = jnp.where(half_masks[hh], qp, 0.0).astype(BF16)
                s = lax.dot_general(qh, kp, (((1,), (1,)), ((), ())), preferred_element_type=F32)
                scores.append(jnp.where(mask, s, NEG))
        s_all = jnp.concatenate(scores, axis=0)
        m = jnp.max(s_all, axis=-1, keepdims=True)
        p = jnp.exp(s_all - m)
        den = jnp.sum(p, axis=-1, keepdims=True)
        p = p.astype(BF16)
        inv = 1.0 / den
        lse = m + jnp.log(den)
        for n, (r, c) in enumerate(blocks):
            piece = lambda t, hh: t[(2 * n + hh) * blk:(2 * n + hh + 1) * blk]
            outs = [jnp.dot(piece(p, hh), values[n], preferred_element_type=F32) * piece(inv, hh) for hh in range(2)]
            o_s[g, rows(r, c, blk), :] = jnp.where(half_masks[0], outs[0], outs[1])
            lse_s[g, rows(r, c, blk), :] = jnp.where(half_masks[0], piece(lse, 0), piece(lse, 1))

    for g, dil in enumerate(DILATIONS):
        n_blocks = seq // dil // blk
        first_blocks = [(r, 0) for r in range(dil)]
        later_blocks = [(r, c) for r in range(dil) for c in range(1, n_blocks)]
        per_first = ATTN_TILES_PER_BATCH // 2
        per_later = ATTN_TILES_PER_BATCH // 4
        for i in range(0, len(first_blocks), per_first):
            batch(g, dil, first_blocks[i:i + per_first])
        for i in range(0, len(later_blocks), per_later):
            batch(g, dil, later_blocks[i:i + per_later])

    for t0 in range(0, seq, ATTN_MIX_ROWS):
        rws = slice(t0, t0 + ATTN_MIX_ROWS)
        lses = [lse_s[g, rws, :] for g in range(n_groups)]
        top = functools.reduce(jnp.maximum, lses)
        es = [jnp.exp(l - top) for l in lses]
        inv = 1.0 / functools.reduce(jnp.add, es)
        for g in range(n_groups):
            att_refs[g][0, rws, :] = (o_s[g, rws, :] * (es[g] * inv)).astype(att_refs[g].dtype)


def _prompt_attention(q, kvs):
    bk, tk, _ = q.shape
    n_groups = len(DILATIONS)
    pairs = GROUP_WIDTH // LANES
    col = lambda first: pl.BlockSpec((1, tk, LANES), lambda i, j: (i, 0, first + j))
    out = jax.ShapeDtypeStruct((bk, tk, GROUP_WIDTH), BF16)
    return pl.pallas_call(
        functools.partial(_attn_kernel, seq=tk),
        grid=(bk, pairs),
        in_specs=([col(g * pairs) for g in range(n_groups)]
                  + [col(0)] * n_groups
                  + [col(pairs)] * n_groups),
        out_specs=[col(0)] * n_groups,
        out_shape=[out] * n_groups,
        scratch_shapes=[pltpu.VMEM((n_groups, tk, LANES), F32)] * 2,
        compiler_params=_params(2),
        name="attn",
    )(*([q] * n_groups), *kvs, *kvs)


def _sample_attn_kernel(q_ref, kvn_ref, c0_ref, c1_ref, c2_ref, att_ref, *, dec_seq):
    cache_refs = (c0_ref, c1_ref, c2_ref)
    seqs_per_slab = SUBLANES // dec_seq
    n_groups = len(DILATIONS)
    widest = DILATIONS[-1]
    slab = pl.ds(pl.multiple_of(pl.program_id(0) * SUBLANES, SUBLANES), SUBLANES)
    lane = lax.broadcasted_iota(jnp.int32, (1, LANES), 1)
    row8 = lax.broadcasted_iota(jnp.int32, (SUBLANES, LANES), 0)
    sel_rows = 2 * SUBLANES
    sel_row = lax.broadcasted_iota(jnp.int32, (sel_rows, LANES), 0)
    sel_lane = lax.broadcasted_iota(jnp.int32, (sel_rows, LANES), 1)
    query_class = lane % widest

    def one_hot(cond):
        return jnp.where(cond, 1.0, 0.0).astype(BF16)

    def spread(rows8, sel):
        padded = jnp.concatenate([rows8, jnp.zeros_like(rows8)], axis=0).astype(BF16)
        return lax.dot_general(padded, sel, (((0,), (0,)), ((), ())), preferred_element_type=F32)

    def class_fold(x, period, op):
        shift = LANES // 2
        while shift >= period:
            x = op(x, pltpu.roll(x, shift, 1))
            shift //= 2
        return x

    def valid_mask(n_tiles, cache_ok, new_ok):
        rows = SUBLANES * (n_tiles // SUBLANES + 1)
        r = lax.broadcasted_iota(jnp.int32, (rows, LANES), 0)
        p = lax.broadcasted_iota(jnp.int32, (rows, LANES), 1)
        return ((r < n_tiles) & cache_ok(p)) | ((r == n_tiles) & new_ok(p))

    def attend(qm, cache_ref, seq, head, knt, vnt, n_tiles, valid, period):
        groups = []
        for t0 in range(0, n_tiles + 1, SUBLANES):
            acc = jnp.full((SUBLANES, LANES), NEG, F32)
            for t in range(t0, min(t0 + SUBLANES, n_tiles + 1)):
                keys = knt if t == n_tiles else cache_ref[0, seq, 0, head, :, t * LANES:(t + 1) * LANES]
                acc = jnp.where(row8 == t - t0, jnp.sum(qm * keys, axis=0, keepdims=True), acc)
            groups.append(acc)
        scores = jnp.where(valid, groups[0] if len(groups) == 1 else jnp.concatenate(groups, axis=0), NEG)
        top = jnp.broadcast_to(jnp.max(scores, axis=0, keepdims=True), (SUBLANES, LANES))
        if period:
            top = class_fold(top, period, jnp.maximum)
        else:
            top = jnp.broadcast_to(jnp.max(top, axis=1, keepdims=True), (SUBLANES, LANES))
        probs = jnp.exp(scores - jnp.concatenate([top] * len(groups), axis=0))
        den = jnp.broadcast_to(jnp.sum(probs, axis=0, keepdims=True), (SUBLANES, LANES))
        num = jnp.zeros((HEAD_DIM, LANES), F32)
        for t in range(n_tiles + 1):
            vals = vnt if t == n_tiles else cache_ref[0, seq, 1, head, :, t * LANES:(t + 1) * LANES]
            num = num + vals * probs[t:t + 1, :]
        if period:
            return class_fold(num, period, jnp.add), top, class_fold(den, period, jnp.add)
        return (jnp.broadcast_to(jnp.sum(num, axis=1, keepdims=True), (HEAD_DIM, LANES)), top,
                jnp.broadcast_to(jnp.sum(den, axis=1, keepdims=True), (SUBLANES, LANES)))

    for pair in range(GROUP_WIDTH // LANES):
        mixed = [[None] * seqs_per_slab for _ in range(n_groups)]
        for e in range(seqs_per_slab):
            base = e * dec_seq
            sel_new = one_hot((sel_row == base + sel_lane) & (sel_lane < dec_seq))
            per_group = []
            for g, dil in enumerate(DILATIONS):
                off = pair * LANES
                knt = spread(kvn_ref[slab, 2 * g * GROUP_WIDTH + off:2 * g * GROUP_WIDTH + off + LANES], sel_new)
                vnt = spread(kvn_ref[slab, (2 * g + 1) * GROUP_WIDTH + off:(2 * g + 1) * GROUP_WIDTH + off + LANES], sel_new)
                q8 = q_ref[slab, g * GROUP_WIDTH + off:g * GROUP_WIDTH + off + LANES]
                n_tiles = cache_refs[g].shape[5] // LANES
                if dil == 1:
                    qms = [spread(q8, one_hot(sel_row == base + i)) for i in range(dec_seq)]
                    valids = [valid_mask(n_tiles, lambda p, i=i: p >= i, lambda p, i=i: p <= i) for i in range(dec_seq)]
                else:
                    qms = [spread(q8, one_hot((sel_row == base + sel_lane % dil) & (sel_lane % dil < dec_seq)))]
                    valids = [valid_mask(n_tiles, lambda p, dil=dil: p % dil < dec_seq, lambda p: p < dec_seq)]
                heads = []
                for hh in range(2):
                    hrows = slice(hh * HEAD_DIM, (hh + 1) * HEAD_DIM)
                    num = top = den = None
                    for i, (qm, valid) in enumerate(zip(qms, valids)):
                        n_i, t_i, d_i = attend(qm[hrows], cache_refs[g], e, 2 * pair + hh, knt[hrows], vnt[hrows],
                                               n_tiles, valid, dil if dil > 1 else 0)
                        if num is None:
                            num, top, den = n_i, t_i, d_i
                        else:
                            mine = query_class == i
                            num, top, den = jnp.where(mine, n_i, num), jnp.where(mine, t_i, top), jnp.where(mine, d_i, den)
                    heads.append([num, top, den])
                per_group.append(heads)
            for hh in range(2):
                tops = [per_group[g][hh][1] for g in range(n_groups)]
                peak = functools.reduce(jnp.maximum, tops)
                scales = [jnp.exp(t - peak) for t in tops]
                total = functools.reduce(jnp.add, [per_group[g][hh][2] * scales[g] for g in range(n_groups)])
                for g in range(n_groups):
                    per_group[g][hh] = per_group[g][hh][0] * (scales[g] / total)[0:1, :]
            for g in range(n_groups):
                mixed[g][e] = jnp.concatenate(per_group[g], axis=0)
        sel_out = one_hot(sel_lane == widest * (sel_row // dec_seq) + sel_row % dec_seq)
        for g in range(n_groups):
            both = mixed[g][0]
            for e in range(1, seqs_per_slab):
                both = jnp.where((lane >= widest * e) & (lane < widest * (e + 1)), mixed[g][e], both)
            hi = both.astype(BF16)
            lo = (both - hi.astype(F32)).astype(BF16)
            dims = (((1,), (1,)), ((), ()))
            rows = (lax.dot_general(sel_out, hi, dims, preferred_element_type=F32)
                    + lax.dot_general(sel_out, lo, dims, preferred_element_type=F32))
            att_ref[slab, g * GROUP_WIDTH + pair * LANES:g * GROUP_WIDTH + (pair + 1) * LANES] = rows[:SUBLANES]


def _sample_attention(q, kvn, caches_t, layer, dec_seq):
    rows, width = q.shape
    seqs_per_slab = SUBLANES // dec_seq
    cache_spec = lambda c: pl.BlockSpec((1, seqs_per_slab) + c.shape[2:], lambda i: (layer, i, 0, 0, 0, 0))
    return pl.pallas_call(
        functools.partial(_sample_attn_kernel, dec_seq=dec_seq),
        grid=(rows // SUBLANES,),
        in_specs=[_const_spec(q.shape), _const_spec(kvn.shape)] + [cache_spec(c) for c in caches_t],
        out_specs=pl.BlockSpec((rows, width), lambda i: (0, 0)),
        out_shape=jax.ShapeDtypeStruct((rows, width), F32),
        compiler_params=_params(1),
        name="sample_attn",
    )(q, kvn, *caches_t)


def _mix_kernel(*refs, tm, alpha):
    n_att = len(refs) - 9
    u_ref, gv_ref, x_ref, wsp_ref, bsp_ref, wm_ref, g_ref, b_ref, out_ref = refs[n_att:]
    y = alpha * x_ref[0]
    col = 0
    for att_ref in refs[:n_att]:
        width = att_ref.shape[2]
        y = y + jnp.dot(att_ref[0].astype(BF16), wm_ref[col:col + width, :], preferred_element_type=F32)
        col += width

    t_idx = lax.broadcasted_iota(jnp.int32, (SGU_CHUNK, SGU_CHUNK), 0)
    s_idx = lax.broadcasted_iota(jnp.int32, (SGU_CHUNK, SGU_CHUNK), 1)
    w_sp = [jnp.where(s_idx <= t_idx, wsp_ref[k], 0.0).astype(BF16) for k in range(wsp_ref.shape[0])]
    sgu_width = gv_ref.shape[2]
    chunks = []
    for c in range(tm // SGU_CHUNK):
        crow = slice(c * SGU_CHUNK, (c + 1) * SGU_CHUNK)
        slabs = []
        for pair in range(sgu_width // LANES):
            gp = gv_ref[0, crow, pair * LANES:(pair + 1) * LANES]
            mixed = jnp.zeros((SGU_CHUNK, LANES), F32)
            for hh in range(2):
                gm = jnp.where(_head_half_mask(LANES, hh), gp, 0.0).astype(BF16)
                mixed = mixed + jnp.dot(w_sp[2 * pair + hh], gm, preferred_element_type=F32)
            slabs.append(mixed)
        mixed = jnp.concatenate(slabs, axis=1) + bsp_ref[...]
        chunks.append(u_ref[0, crow, :] * mixed)
    sgu = jnp.concatenate(chunks, axis=0).astype(BF16)
    y = y + jnp.dot(sgu, wm_ref[col:, :], preferred_element_type=F32)
    out_ref[0] = _layer_norm(y, g_ref[...], b_ref[...])


def _mix(att_inputs, u, gv, x, w_sp, b_sp_tile, w_mix_b, g, b, tm, alpha):
    bk, tk, d = x.shape
    sgu_width = u.shape[2]
    row = lambda w: pl.BlockSpec((1, tm, w), lambda i, j: (i, j, 0))
    return pl.pallas_call(
        functools.partial(_mix_kernel, tm=tm, alpha=alpha),
        grid=(bk, tk // tm),
        in_specs=[row(a.shape[2]) for a in att_inputs] + [
            row(sgu_width), row(sgu_width), row(d),
            _const_spec(w_sp.shape), _const_spec(b_sp_tile.shape), _const_spec(w_mix_b.shape),
            _const_spec((1, d)), _const_spec((1, d))],
        out_specs=row(d),
        out_shape=jax.ShapeDtypeStruct((bk, tk, d), F32),
        compiler_params=_params(2),
        name="mix",
    )(*att_inputs, u, gv, x, w_sp, b_sp_tile, w_mix_b, g, b)


def _memkv_kernel(m_ref, w_ref, out_ref):
    out_ref[0] = jnp.dot(m_ref[0].astype(BF16), w_ref[...], preferred_element_type=F32)


def _memkv(mem, w_xkv_b):
    bk, n_mem, d = mem.shape
    width = w_xkv_b.shape[1]
    return pl.pallas_call(
        _memkv_kernel,
        grid=(bk,),
        in_specs=[pl.BlockSpec((1, n_mem, d), lambda i: (i, 0, 0)), _const_spec(w_xkv_b.shape)],
        out_specs=pl.BlockSpec((1, n_mem, width), lambda i: (i, 0, 0)),
        out_shape=jax.ShapeDtypeStruct((bk, n_mem, width), F32),
        compiler_params=_params(1),
        name="memkv",
    )(mem, w_xkv_b)


def _softmax_rows(s):
    m = jnp.max(s, axis=-1, keepdims=True)
    p = jnp.exp(s - m)
    return p * (1.0 / jnp.sum(p, axis=-1, keepdims=True))


def _xattn_kernel(x_ref, mkv_ref, wq_ref, wo_ref, g_ref, b_ref, out_ref, *, alpha):
    x = x_ref[0]
    d = x.shape[1]
    hd = d // MEM_HEADS
    qx = jnp.dot(x.astype(BF16), wq_ref[...], preferred_element_type=F32)
    y = alpha * x
    for h in range(MEM_HEADS):
        qh = qx[:, h * hd:(h + 1) * hd].astype(BF16)
        kh = mkv_ref[0, :, h * hd:(h + 1) * hd].astype(BF16)
        vh = mkv_ref[0, :, d + h * hd:d + (h + 1) * hd].astype(BF16)
        s = lax.dot_general(qh, kh, (((1,), (1,)), ((), ())), preferred_element_type=F32)
        oh = jnp.dot(_softmax_rows(s).astype(BF16), vh, preferred_element_type=F32)
        y = y + jnp.dot(oh.astype(BF16), wo_ref[h * hd:(h + 1) * hd, :], preferred_element_type=F32)
    out_ref[0] = _layer_norm(y, g_ref[...], b_ref[...])


def _xattn(x, mkv, wq_b, wo_b, g, b, tm, alpha):
    bk, tk, d = x.shape
    n_mem = mkv.shape[1]
    row = pl.BlockSpec((1, tm, d), lambda i, j: (i, j, 0))
    return pl.pallas_call(
        functools.partial(_xattn_kernel, alpha=alpha),
        grid=(bk, tk // tm),
        in_specs=[row, pl.BlockSpec((1, n_mem, 2 * d), lambda i, j: (i, 0, 0)),
                  _const_spec(wq_b.shape), _const_spec(wo_b.shape), _const_spec((1, d)), _const_spec((1, d))],
        out_specs=row,
        out_shape=jax.ShapeDtypeStruct((bk, tk, d), F32),
        compiler_params=_params(2),
        name="xattn",
    )(x, mkv, wq_b, wo_b, g, b)


def _sample_xattn_kernel(x_ref, mkv_ref, wq_ref, wo_ref, g_ref, b_ref, out_ref, qx_ref, ox_ref,
                         *, dec_seq, seqs_per_step, alpha):
    step = pl.program_id(0)
    d = x_ref.shape[1]
    hd = d // MEM_HEADS
    seqs_per_slab = SUBLANES // dec_seq

    @pl.when(step == 0)
    def _():
        qx_ref[...] = jnp.dot(x_ref[...].astype(BF16), wq_ref[...], preferred_element_type=F32)

    n_chunks = hd // LANES
    rows_per_mem = 2 * n_chunks * MEM_HEADS
    n_mem = mkv_ref.shape[2] // rows_per_mem

    def head_matrix(b_local, kv, h):
        chunks = [mkv_ref[0, b_local, pl.ds((kv * n_chunks + c) * MEM_HEADS + h, n_mem, stride=rows_per_mem), :]
                  for c in range(n_chunks)]
        return jnp.concatenate(chunks, axis=1).astype(BF16)

    slab_row = lax.broadcasted_iota(jnp.int32, (SUBLANES, 1), 0)
    for j in range(seqs_per_step // seqs_per_slab):
        slab = pl.ds(pl.multiple_of((step * (seqs_per_step // seqs_per_slab) + j) * SUBLANES, SUBLANES), SUBLANES)
        for h in range(MEM_HEADS):
            qh = qx_ref[slab, h * hd:(h + 1) * hd].astype(BF16)
            o8 = jnp.zeros((SUBLANES, hd), F32)
            for e in range(seqs_per_slab):
                b_local = j * seqs_per_slab + e
                kh = head_matrix(b_local, 0, h)
                vh = head_matrix(b_local, 1, h)
                s = lax.dot_general(qh, kh, (((1,), (1,)), ((), ())), preferred_element_type=F32)
                oh = jnp.dot(_softmax_rows(s).astype(BF16), vh, preferred_element_type=F32)
                o8 = jnp.where(slab_row // dec_seq == e, oh, o8)
            ox_ref[slab, h * hd:(h + 1) * hd] = o8

    @pl.when(step == pl.num_programs(0) - 1)
    def _():
        y = alpha * x_ref[...] + jnp.dot(ox_ref[...].astype(BF16), wo_ref[...], preferred_element_type=F32)
        out_ref[...] = _layer_norm(y, g_ref[...], b_ref[...])


def _sample_xattn(x, mkv_flat, layer, wq_b, wo_b, g, b, dec_seq, seqs_per_step, alpha):
    rows, d = x.shape
    _, n_seq, flat_rows, _ = mkv_flat.shape
    return pl.pallas_call(
        functools.partial(_sample_xattn_kernel, dec_seq=dec_seq, seqs_per_step=seqs_per_step, alpha=alpha),
        grid=(n_seq // seqs_per_step,),
        in_specs=[_const_spec(x.shape),
                  pl.BlockSpec((1, seqs_per_step, flat_rows, LANES), lambda i: (layer, i, 0, 0)),
                  _const_spec(wq_b.shape), _const_spec(wo_b.shape), _const_spec((1, d)), _const_spec((1, d))],
        out_specs=pl.BlockSpec((rows, d), lambda i: (0, 0)),
        out_shape=jax.ShapeDtypeStruct((rows, d), F32),
        scratch_shapes=[pltpu.VMEM((rows, d), F32), pltpu.VMEM((rows, d), F32)],
        compiler_params=_params(1),
        name="sample_xattn",
    )(x, mkv_flat, wq_b, wo_b, g, b)


def _mlp_kernel(x_ref, wu_ref, wd_ref, g_ref, b_ref, out_ref, *, ff_chunk, alpha):
    x = x_ref[...]
    xb = x.astype(BF16)
    y = alpha * x
    for c in range(wu_ref.shape[1] // ff_chunk):
        h = jnp.dot(xb, wu_ref[:, c * ff_chunk:(c + 1) * ff_chunk], preferred_element_type=F32)
        h = jnp.square(jnp.maximum(h, 0.0)).astype(BF16)
        y = y + jnp.dot(h, wd_ref[c * ff_chunk:(c + 1) * ff_chunk, :], preferred_element_type=F32)
    out_ref[...] = _layer_norm(y, g_ref[...], b_ref[...])


def _mlp(x, wu_b, wd_b, g, b, tm, alpha, ff_chunk=1024):
    rows, d = x.shape
    row = pl.BlockSpec((tm, d), lambda i: (i, 0))
    return pl.pallas_call(
        functools.partial(_mlp_kernel, ff_chunk=ff_chunk, alpha=alpha),
        grid=(rows // tm,),
        in_specs=[row, _const_spec(wu_b.shape), _const_spec(wd_b.shape), _const_spec((1, d)), _const_spec((1, d))],
        out_specs=row,
        out_shape=jax.ShapeDtypeStruct((rows, d), F32),
        compiler_params=_params(1),
        name="mlp",
    )(x, wu_b, wd_b, g, b)


def kernel(x_prompt, x_sample, cache_kv_w128, cache_kv_w512, cache_kv_w2048, cache_mem_kv, mem_prompt,
           w_in, sgu_ln_g, sgu_ln_b, w_spatial, b_spatial, w_mix_out, ln1_g, ln1_b,
           w_xq, w_xkv, w_xo, ln2_g, ln2_b, w_up, w_down, ln3_g, ln3_b):
    depth = w_in.shape[0]
    bp, tp, d = x_prompt.shape
    bs, ts, _ = x_sample.shape
    past_len = 8192
    alpha = float((2 * depth) ** 0.25)
    att_width = len(DILATIONS) * GROUP_WIDTH
    sgu_width = sgu_ln_g.shape[1]
    sgu_groups = w_spatial.shape[1]
    n_mem = mem_prompt.shape[1]
    rows_s = bs * ts
    assert tp % (DILATIONS[-1] * WINDOW_STEPS) == 0 and SUBLANES % ts == 0 and rows_s % SGU_CHUNK == 0
    assert d // MEM_HEADS == GROUP_WIDTH and sgu_width == GROUP_WIDTH

    col_scale = jnp.concatenate([jnp.full((att_width,), HEAD_DIM ** -0.5, F32),
                                 jnp.ones((w_in.shape[2] - att_width,), F32)])
    w_in_b = (w_in * col_scale).astype(BF16)
    w_xq_b = (w_xq * (d // MEM_HEADS) ** -0.5).astype(BF16)
    w_mix_b, w_xkv_b, w_xo_b = w_mix_out.astype(BF16), w_xkv.astype(BF16), w_xo.astype(BF16)
    w_up_b, w_down_b = w_up.astype(BF16), w_down.astype(BF16)
    vec = lambda p, l: p[l][None, :]

    tables_p = _rope_tables(jnp.arange(tp, dtype=jnp.int32))
    tables_s = _rope_tables(past_len + (jnp.arange(rows_s, dtype=jnp.int32) % ts))

    eye = jnp.eye(rows_s // ts, dtype=F32)
    tri = jnp.tril(jnp.ones((ts, ts), F32))

    hp = x_prompt
    hs = x_sample.reshape(1, rows_s, d)
    out_rows_p = [[] for _ in DILATIONS]
    out_rows_s = [[] for _ in DILATIONS]
    out_mem, out_gv = [], []
    caches_t = [jnp.transpose(c, (0, 1, 3, 4, 5, 2)) for c in (cache_kv_w128, cache_kv_w512, cache_kv_w2048)]
    mem_chunks = d // MEM_HEADS // LANES
    mem_flat = (cache_mem_kv.reshape(depth, bs, n_mem, 2, MEM_HEADS, mem_chunks, LANES)
                .transpose(0, 1, 2, 3, 5, 4, 6).reshape(depth, bs, n_mem * 2 * mem_chunks * MEM_HEADS, LANES))
    for l in range(depth):
        mkv = _memkv(mem_prompt, w_xkv_b[l])
        out_mem.append(mkv.reshape(bp, n_mem, 2, MEM_HEADS, d // MEM_HEADS))
        q, kv0, kv1, kv2, u, gv = _inproj(hp, w_in_b[l], tables_p, vec(sgu_ln_g, l), vec(sgu_ln_b, l), tm=512)
        for g, kv in enumerate((kv0, kv1, kv2)):
            keep = min(DILATIONS[g] * WINDOW_STEPS, tp)
            out_rows_p[g].append(kv[:, tp - keep:].reshape(bp, keep, 2, HEADS_PER_GROUP, HEAD_DIM))
        atts = _prompt_attention(q, [kv0, kv1, kv2])
        b_tile = jnp.repeat(b_spatial[l][:, :SGU_CHUNK].T, sgu_width // sgu_groups, axis=1)
        hp = _mix(atts, u, gv, hp, w_spatial[l], b_tile, w_mix_b[l],
                  vec(ln1_g, l), vec(ln1_b, l), tm=512, alpha=alpha)
        hp = _xattn(hp, mkv, w_xq_b[l], w_xo_b[l], vec(ln2_g, l), vec(ln2_b, l), tm=512, alpha=alpha)
        hp = _mlp(hp.reshape(bp * tp, d), w_up_b[l], w_down_b[l], vec(ln3_g, l), vec(ln3_b, l),
                  tm=512, alpha=alpha).reshape(bp, tp, d)

        q, kv0, kv1, kv2, u, gv = _inproj(hs, w_in_b[l], tables_s, vec(sgu_ln_g, l), vec(sgu_ln_b, l), tm=rows_s)
        out_gv.append(gv.reshape(bs, ts, sgu_width))
        for g, kv in enumerate((kv0, kv1, kv2)):
            out_rows_s[g].append(kv.reshape(bs, ts, 2, HEADS_PER_GROUP, HEAD_DIM))
        kvn = jnp.concatenate([kv0[0], kv1[0], kv2[0]], axis=1)
        att = _sample_attention(q[0], kvn, caches_t, l, dec_seq=ts)
        w_sp_s = jnp.einsum("ab,gts->gatbs", eye, w_spatial[l][:, :ts, :ts] * tri).reshape(sgu_groups, rows_s, rows_s)
        b_tile_s = jnp.repeat(jnp.tile(b_spatial[l][:, :ts].T, (rows_s // ts, 1)), sgu_width // sgu_groups, axis=1)
        hs = _mix([att[None]], u, gv, hs, w_sp_s, b_tile_s, w_mix_b[l],
                  vec(ln1_g, l), vec(ln1_b, l), tm=rows_s, alpha=alpha)
        hs2 = _sample_xattn(hs[0], mem_flat, l, w_xq_b[l], w_xo_b[l], vec(ln2_g, l), vec(ln2_b, l),
                            dec_seq=ts, seqs_per_step=4, alpha=alpha)
        hs = _mlp(hs2, w_up_b[l], w_down_b[l], vec(ln3_g, l), vec(ln3_b, l), tm=rows_s, alpha=alpha)[None]

    stack = lambda xs: jnp.stack(xs)
    return (hp, hs.reshape(bs, ts, d),
            stack(out_rows_p[0]), stack(out_rows_p[1]), stack(out_rows_p[2]), stack(out_mem),
            stack(out_rows_s[0]), stack(out_rows_s[1]), stack(out_rows_s[2]), stack(out_gv))
```

```python
import functools
import math

import jax
import jax.numpy as jnp
from jax import lax
from jax.experimental import pallas as pl
from jax.experimental.pallas import tpu as pltpu

F32 = jnp.float32
BF16 = jnp.bfloat16

HEAD_DIM = 64
HEADS_PER_GROUP = 4
GROUP_WIDTH = HEAD_DIM * HEADS_PER_GROUP
DILATIONS = (1, 4, 16)
WINDOW_STEPS = 128
ROT_DIM = 16
ROPE_THETA = 500000.0
SGU_CHUNK = 128
MEM_HEADS = 4
LN_EPS = 1e-5
NEG = -1e30
LANES = 128
SUBLANES = 8
VMEM_LIMIT = 56 * 1024 * 1024


def _params(n_grid_dims):
    return pltpu.CompilerParams(
        dimension_semantics=("arbitrary",) * n_grid_dims,
        vmem_limit_bytes=VMEM_LIMIT)


def _const_spec(shape):
    nd = len(shape)
    return pl.BlockSpec(shape, lambda *_: (0,) * nd, pipeline_mode=pl.Buffered(1))


def _layer_norm(y, g, b):
    mu = jnp.mean(y, axis=-1, keepdims=True)
    yc = y - mu
    var = jnp.mean(yc * yc, axis=-1, keepdims=True)
    return yc * lax.rsqrt(var + LN_EPS) * g + b


def _gelu_tanh(x):
    return 0.5 * x * (1.0 + jnp.tanh(0.7978845608028654 * (x + 0.044715 * (x * x * x))))


def _head_half_mask(width, half):
    lane = lax.broadcasted_iota(jnp.int32, (1, width), 1)
    return (lane // HEAD_DIM) % 2 == half


def _inproj_kernel(x_ref, w_ref, cos_ref, sin_lo_ref, sin_hi_ref, g_ref, b_ref,
                   q_ref, kv0_ref, kv1_ref, kv2_ref, u_ref, gv_ref, *, att_width, sgu_width):
    xb = x_ref[0].astype(BF16)
    cos = cos_ref[...]
    sin_lo = sin_lo_ref[...]
    sin_hi = sin_hi_ref[...]

    def proj(c0, width):
        return jnp.dot(xb, w_ref[:, c0:c0 + width], preferred_element_type=F32)

    def rope(t):
        return t * cos + pltpu.roll(t, LANES - ROT_DIM // 2, 1) * sin_lo + pltpu.roll(t, ROT_DIM // 2, 1) * sin_hi

    def rope_group(t):
        return jnp.concatenate([rope(t[:, s:s + LANES]) for s in range(0, GROUP_WIDTH, LANES)], axis=1)

    u_ref[0] = _gelu_tanh(proj(3 * att_width, sgu_width)).astype(u_ref.dtype)
    gate = _gelu_tanh(proj(3 * att_width + sgu_width, sgu_width))
    gv_ref[0] = _layer_norm(gate, g_ref[...], b_ref[...]).astype(gv_ref.dtype)
    kv_refs = (kv0_ref, kv1_ref, kv2_ref)
    for grp, kv_ref in enumerate(kv_refs):
        c = grp * GROUP_WIDTH
        q_ref[0, :, c:c + GROUP_WIDTH] = rope_group(proj(c, GROUP_WIDTH))
        kv_ref[0, :, 0:GROUP_WIDTH] = rope_group(proj(att_width + c, GROUP_WIDTH))
    for grp, kv_ref in enumerate(kv_refs):
        kv_ref[0, :, GROUP_WIDTH:2 * GROUP_WIDTH] = proj(2 * att_width + grp * GROUP_WIDTH, GROUP_WIDTH)


def _inproj(x, w_in_b, tables, g, b, tm, gate_dtype):
    bk, tk, d = x.shape
    att_width = len(DILATIONS) * GROUP_WIDTH
    sgu_width = (w_in_b.shape[1] - 3 * att_width) // 2
    cos, sin_lo, sin_hi = tables
    row = lambda w: pl.BlockSpec((1, tm, w), lambda i, j: (i, j, 0))
    tab = pl.BlockSpec((tm, LANES), lambda i, j: (j, 0))
    out_shape = ([jax.ShapeDtypeStruct((bk, tk, w), F32) for w in (att_width,) + (2 * GROUP_WIDTH,) * len(DILATIONS)]
                 + [jax.ShapeDtypeStruct((bk, tk, sgu_width), gate_dtype)] * 2)
    return pl.pallas_call(
        functools.partial(_inproj_kernel, att_width=att_width, sgu_width=sgu_width),
        grid=(bk, tk // tm),
        in_specs=[row(d), _const_spec(w_in_b.shape), tab, tab, tab,
                  _const_spec((1, sgu_width)), _const_spec((1, sgu_width))],
        out_specs=[row(s.shape[2]) for s in out_shape],
        out_shape=out_shape,
        compiler_params=_params(2),
        name="inproj",
    )(x, w_in_b, cos, sin_lo, sin_hi, g, b)


def _rope_tables(pos):
    half = ROT_DIM // 2
    inv = ROPE_THETA ** (-jnp.arange(half, dtype=F32) / half)
    ang = pos.astype(F32)[:, None] * inv[None, :]
    cos, sin = jnp.cos(ang), jnp.sin(ang)
    zeros = jnp.zeros((pos.shape[0], HEAD_DIM - ROT_DIM), F32)
    zero_half = jnp.zeros_like(sin)
    cos_head = jnp.concatenate([cos, cos, zeros + 1.0], axis=1)
    lo_head = jnp.concatenate([-sin, zero_half, zeros], axis=1)
    hi_head = jnp.concatenate([zero_half, sin, zeros], axis=1)
    two = lambda t: jnp.concatenate([t, t], axis=1)
    return two(cos_head), two(lo_head), two(hi_head)


ATTN_MIX_ROWS = 256
ATTN_TILES_PER_BATCH = 8


def _attn_kernel(*refs, seq):
    n_groups = len(DILATIONS)
    q_refs, k_refs, v_refs = refs[0:n_groups], refs[n_groups:2 * n_groups], refs[2 * n_groups:3 * n_groups]
    att_refs = refs[3 * n_groups:4 * n_groups]
    o_s, lse_s = refs[4 * n_groups:]
    blk = WINDOW_STEPS
    x_idx = lax.broadcasted_iota(jnp.int32, (blk, 2 * blk), 0)
    k_idx = lax.broadcasted_iota(jnp.int32, (blk, 2 * blk), 1)
    band_mask = (k_idx >= x_idx) & (k_idx <= x_idx + blk)
    causal_mask = (lax.broadcasted_iota(jnp.int32, (blk, blk), 1)
                   <= lax.broadcasted_iota(jnp.int32, (blk, blk), 0))
    half_masks = [_head_half_mask(LANES, hh) for hh in range(2)]

    def batch(g, dil, blocks):
        def rows(r, first_block, n):
            start = r + dil * blk * first_block
            return pl.ds(start, n) if dil == 1 else pl.ds(start, n, stride=dil)

        first = blocks[0][1] == 0
        mask = causal_mask if first else band_mask
        scores, values = [], []
        for r, c in blocks:
            k_rows = rows(r, 0, blk) if first else rows(r, c - 1, 2 * blk)
            qp = q_refs[g][0, rows(r, c, blk), :]
            kp = k_refs[g][0, k_rows, :].astype(BF16)
            values.append(v_refs[g][0, k_rows, :].astype(BF16))
            for hh in range(2):
                qh = jnp.where(half_masks[hh], qp, 0.0).astype(BF16)
                s = lax.dot_general(qh, kp, (((1,), (1,)), ((), ())), preferred_element_type=F32)
                scores.append(jnp.where(mask, s, NEG))
        s_all = jnp.concatenate(scores, axis=0)
        m = jnp.max(s_all, axis=-1, keepdims=True)
        p = jnp.exp(s_all - m)
        den = jnp.sum(p, axis=-1, keepdims=True)
        p = p.astype(BF16)
        inv = 1.0 / den
        lse = m + jnp.log(den)
        for n, (r, c) in enumerate(blocks):
            piece = lambda t, hh: t[(2 * n + hh) * blk:(2 * n + hh + 1) * blk]
            outs = [jnp.dot(piece(p, hh), values[n], preferred_element_type=F32) * piece(inv, hh) for hh in range(2)]
            o_s[g, rows(r, c, blk), :] = jnp.where(half_masks[0], outs[0], outs[1])
            lse_s[g, rows(r, c, blk), :] = jnp.where(half_masks[0], piece(lse, 0), piece(lse, 1))

    for g, dil in enumerate(DILATIONS):
        n_blocks = seq // dil // blk
        first_blocks = [(r, 0) for r in range(dil)]
        later_blocks = [(r, c) for r in range(dil) for c in range(1, n_blocks)]
        per_first = ATTN_TILES_PER_BATCH // 2
        per_later = ATTN_TILES_PER_BATCH // 4
        for i in range(0, len(first_blocks), per_first):
            batch(g, dil, first_blocks[i:i + per_first])
        for i in range(0, len(later_blocks), per_later):
            batch(g, dil, later_blocks[i:i + per_later])

    for t0 in range(0, seq, ATTN_MIX_ROWS):
        rws = slice(t0, t0 + ATTN_MIX_ROWS)
        lses = [lse_s[g, rws, :] for g in range(n_groups)]
        top = functools.reduce(jnp.maximum, lses)
        es = [jnp.exp(l - top) for l in lses]
        inv = 1.0 / functools.reduce(jnp.add, es)
        for g in range(n_groups):
            att_refs[g][0, rws, :] = (o_s[g, rws, :] * (es[g] * inv)).astype(att_refs[g].dtype)


def _prompt_attention(q, kvs):
    bk, tk, _ = q.shape
    n_groups = len(DILATIONS)
    pairs = GROUP_WIDTH // LANES
    col = lambda first: pl.BlockSpec((1, tk, LANES), lambda i, j: (i, 0, first + j))
    out = jax.ShapeDtypeStruct((bk, tk, GROUP_WIDTH), BF16)
    return pl.pallas_call(
        functools.partial(_attn_kernel, seq=tk),
        grid=(bk, pairs),
        in_specs=([col(g * pairs) for g in range(n_groups)]
                  + [col(0)] * n_groups
                  + [col(pairs)] * n_groups),
        out_specs=[col(0)] * n_groups,
        out_shape=[out] * n_groups,
        scratch_shapes=[pltpu.VMEM((n_groups, tk, LANES), F32)] * 2,
        compiler_params=_params(2),
        name="attn",
    )(*([q] * n_groups), *kvs, *kvs)


def _sample_attn_kernel(q_ref, kvn_ref, c0_ref, c1_ref, c2_ref, att_ref, *, dec_seq):
    cache_refs = (c0_ref, c1_ref, c2_ref)
    seqs_per_slab = SUBLANES // dec_seq
    n_groups = len(DILATIONS)
    widest = DILATIONS[-1]
    slab = pl.ds(pl.multiple_of(pl.program_id(0) * SUBLANES, SUBLANES), SUBLANES)
    lane = lax.broadcasted_iota(jnp.int32, (1, LANES), 1)
    row8 = lax.broadcasted_iota(jnp.int32, (SUBLANES, LANES), 0)
    sel_rows = 2 * SUBLANES
    sel_row = lax.broadcasted_iota(jnp.int32, (sel_rows, LANES), 0)
    sel_lane = lax.broadcasted_iota(jnp.int32, (sel_rows, LANES), 1)
    query_class = lane % widest

    def one_hot(cond):
        return jnp.where(cond, 1.0, 0.0).astype(BF16)

    def spread(rows8, sel):
        padded = jnp.concatenate([rows8, jnp.zeros_like(rows8)], axis=0).astype(BF16)
        return lax.dot_general(padded, sel, (((0,), (0,)), ((), ())), preferred_element_type=F32)

    def class_fold(x, period, op):
        shift = LANES // 2
        while shift >= period:
            x = op(x, pltpu.roll(x, shift, 1))
            shift //= 2
        return x

    def valid_mask(n_tiles, cache_ok, new_ok):
        rows = SUBLANES * (n_tiles // SUBLANES + 1)
        r = lax.broadcasted_iota(jnp.int32, (rows, LANES), 0)
        p = lax.broadcasted_iota(jnp.int32, (rows, LANES), 1)
        return ((r < n_tiles) & cache_ok(p)) | ((r == n_tiles) & new_ok(p))

    def attend(qm, cache_ref, seq, head, knt, vnt, n_tiles, valid, period):
        groups = []
        for t0 in range(0, n_tiles + 1, SUBLANES):
            acc = jnp.full((SUBLANES, LANES), NEG, F32)
            for t in range(t0, min(t0 + SUBLANES, n_tiles + 1)):
                keys = knt if t == n_tiles else cache_ref[0, seq, 0, head, :, t * LANES:(t + 1) * LANES]
                acc = jnp.where(row8 == t - t0, jnp.sum(qm * keys, axis=0, keepdims=True), acc)
            groups.append(acc)
        scores = jnp.where(valid, groups[0] if len(groups) == 1 else jnp.concatenate(groups, axis=0), NEG)
        top = jnp.broadcast_to(jnp.max(scores, axis=0, keepdims=True), (SUBLANES, LANES))
        if period:
            top = class_fold(top, period, jnp.maximum)
        else:
            top = jnp.broadcast_to(jnp.max(top, axis=1, keepdims=True), (SUBLANES, LANES))
        probs = jnp.exp(scores - jnp.concatenate([top] * len(groups), axis=0))
        den = jnp.broadcast_to(jnp.sum(probs, axis=0, keepdims=True), (SUBLANES, LANES))
        num = jnp.zeros((HEAD_DIM, LANES), F32)
        for t in range(n_tiles + 1):
            vals = vnt if t == n_tiles else cache_ref[0, seq, 1, head, :, t * LANES:(t + 1) * LANES]
            num = num + vals * probs[t:t + 1, :]
        if period:
            return class_fold(num, period, jnp.add), top, class_fold(den, period, jnp.add)
        return (jnp.broadcast_to(jnp.sum(num, axis=1, keepdims=True), (HEAD_DIM, LANES)), top,
                jnp.broadcast_to(jnp.sum(den, axis=1, keepdims=True), (SUBLANES, LANES)))

    for pair in range(GROUP_WIDTH // LANES):
        mixed = [[None] * seqs_per_slab for _ in range(n_groups)]
        for e in range(seqs_per_slab):
            base = e * dec_seq
            sel_new = one_hot((sel_row == base + sel_lane) & (sel_lane < dec_seq))
            per_group = []
            for g, dil in enumerate(DILATIONS):
                off = pair * LANES
                knt = spread(kvn_ref[slab, 2 * g * GROUP_WIDTH + off:2 * g * GROUP_WIDTH + off + LANES], sel_new)
                vnt = spread(kvn_ref[slab, (2 * g + 1) * GROUP_WIDTH + off:(2 * g + 1) * GROUP_WIDTH + off + LANES], sel_new)
                q8 = q_ref[slab, g * GROUP_WIDTH + off:g * GROUP_WIDTH + off + LANES]
                n_tiles = cache_refs[g].shape[5] // LANES
                if dil == 1:
                    qms = [spread(q8, one_hot(sel_row == base + i)) for i in range(dec_seq)]
                    valids = [valid_mask(n_tiles, lambda p, i=i: p >= i, lambda p, i=i: p <= i) for i in range(dec_seq)]
                else:
                    qms = [spread(q8, one_hot((sel_row == base + sel_lane % dil) & (sel_lane % dil < dec_seq)))]
                    valids = [valid_mask(n_tiles, lambda p, dil=dil: p % dil < dec_seq, lambda p: p < dec_seq)]
                heads = []
                for hh in range(2):
                    hrows = slice(hh * HEAD_DIM, (hh + 1) * HEAD_DIM)
                    num = top = den = None
                    for i, (qm, valid) in enumerate(zip(qms, valids)):
                        n_i, t_i, d_i = attend(qm[hrows], cache_refs[g], e, 2 * pair + hh, knt[hrows], vnt[hrows],
                                               n_tiles, valid, dil if dil > 1 else 0)
                        if num is None:
                            num, top, den = n_i, t_i, d_i
                        else:
                            mine = query_class == i
                            num, top, den = jnp.where(mine, n_i, num), jnp.where(mine, t_i, top), jnp.where(mine, d_i, den)
                    heads.append([num, top, den])
                per_group.append(heads)
            for hh in range(2):
                tops = [per_group[g][hh][1] for g in range(n_groups)]
                peak = functools.reduce(jnp.maximum, tops)
                scales = [jnp.exp(t - peak) for t in tops]
                total = functools.reduce(jnp.add, [per_group[g][hh][2] * scales[g] for g in range(n_groups)])
                for g in range(n_groups):
                    per_group[g][hh] = per_group[g][hh][0] * (scales[g] / total)[0:1, :]
            for g in range(n_groups):
                mixed[g][e] = jnp.concatenate(per_group[g], axis=0)
        sel_out = one_hot(sel_lane == widest * (sel_row // dec_seq) + sel_row % dec_seq)
        for g in range(n_groups):
            both = mixed[g][0]
            for e in range(1, seqs_per_slab):
                both = jnp.where((lane >= widest * e) & (lane < widest * (e + 1)), mixed[g][e], both)
            hi = both.astype(BF16)
            lo = (both - hi.astype(F32)).astype(BF16)
            dims = (((1,), (1,)), ((), ()))
            rows = (lax.dot_general(sel_out, hi, dims, preferred_element_type=F32)
                    + lax.dot_general(sel_out, lo, dims, preferred_element_type=F32))
            att_ref[slab, g * GROUP_WIDTH + pair * LANES:g * GROUP_WIDTH + (pair + 1) * LANES] = rows[:SUBLANES]


def _sample_attention(q, kvn, caches_t, layer, dec_seq):
    rows, width = q.shape
    seqs_per_slab = SUBLANES // dec_seq
    cache_spec = lambda c: pl.BlockSpec((1, seqs_per_slab) + c.shape[2:], lambda i: (layer, i, 0, 0, 0, 0))
    return pl.pallas_call(
        functools.partial(_sample_attn_kernel, dec_seq=dec_seq),
        grid=(rows // SUBLANES,),
        in_specs=[_const_spec(q.shape), _const_spec(kvn.shape)] + [cache_spec(c) for c in caches_t],
        out_specs=pl.BlockSpec((rows, width), lambda i: (0, 0)),
        out_shape=jax.ShapeDtypeStruct((rows, width), F32),
        compiler_params=_params(1),
        name="sample_attn",
    )(q, kvn, *caches_t)


MIX_SUBTILE = 256


def _mix_kernel(*refs, tm, alpha):
    n_att = len(refs) - 9
    u_ref, gv_ref, x_ref, wsp_ref, bsp_ref, wm_ref, g_ref, b_ref, out_ref = refs[n_att:]
    t_idx = lax.broadcasted_iota(jnp.int32, (SGU_CHUNK, SGU_CHUNK), 0)
    s_idx = lax.broadcasted_iota(jnp.int32, (SGU_CHUNK, SGU_CHUNK), 1)
    w_sp = [jnp.where(s_idx <= t_idx, wsp_ref[k], 0.0).astype(BF16) for k in range(wsp_ref.shape[0])]
    sgu_width = gv_ref.shape[2]

    def gated_chunk(crow):
        slabs = []
        for pair in range(sgu_width // LANES):
            gp = gv_ref[0, crow, pair * LANES:(pair + 1) * LANES]
            mixed = jnp.zeros((SGU_CHUNK, LANES), F32)
            for hh in range(2):
                gm = jnp.where(_head_half_mask(LANES, hh), gp, 0.0).astype(BF16)
                mixed = mixed + jnp.dot(w_sp[2 * pair + hh], gm, preferred_element_type=F32)
            slabs.append(mixed)
        return (u_ref[0, crow, :] * (jnp.concatenate(slabs, axis=1) + bsp_ref[...])).astype(BF16)

    sub = min(tm, MIX_SUBTILE)
    for r0 in range(0, tm, sub):
        rws = slice(r0, r0 + sub)
        sgu = jnp.concatenate([gated_chunk(slice(c, c + SGU_CHUNK)) for c in range(r0, r0 + sub, SGU_CHUNK)], axis=0)
        lhs = jnp.concatenate([a[0, rws, :].astype(BF16) for a in refs[:n_att]] + [sgu], axis=1)
        y = alpha * x_ref[0, rws, :] + jnp.dot(lhs, wm_ref[...], preferred_element_type=F32)
        out_ref[0, rws, :] = _layer_norm(y, g_ref[...], b_ref[...])


def _mix(att_inputs, u, gv, x, w_sp, b_sp_tile, w_mix_b, g, b, tm, alpha):
    bk, tk, d = x.shape
    sgu_width = u.shape[2]
    row = lambda w: pl.BlockSpec((1, tm, w), lambda i, j: (i, j, 0))
    return pl.pallas_call(
        functools.partial(_mix_kernel, tm=tm, alpha=alpha),
        grid=(bk, tk // tm),
        in_specs=[row(a.shape[2]) for a in att_inputs] + [
            row(sgu_width), row(sgu_width), row(d),
            _const_spec(w_sp.shape), _const_spec(b_sp_tile.shape), _const_spec(w_mix_b.shape),
            _const_spec((1, d)), _const_spec((1, d))],
        out_specs=row(d),
        out_shape=jax.ShapeDtypeStruct((bk, tk, d), F32),
        compiler_params=_params(2),
        name="mix",
    )(*att_inputs, u, gv, x, w_sp, b_sp_tile, w_mix_b, g, b)


def _memkv_kernel(m_ref, w_ref, out_ref, out_b_ref):
    mkv = jnp.dot(m_ref[0].astype(BF16), w_ref[...], preferred_element_type=F32)
    out_ref[0] = mkv
    out_b_ref[0] = mkv.astype(BF16)


def _memkv(mem, w_xkv_b):
    bk, n_mem, d = mem.shape
    width = w_xkv_b.shape[1]
    out_spec = pl.BlockSpec((1, n_mem, width), lambda i: (i, 0, 0))
    return pl.pallas_call(
        _memkv_kernel,
        grid=(bk,),
        in_specs=[pl.BlockSpec((1, n_mem, d), lambda i: (i, 0, 0)), _const_spec(w_xkv_b.shape)],
        out_specs=[out_spec, out_spec],
        out_shape=[jax.ShapeDtypeStruct((bk, n_mem, width), F32), jax.ShapeDtypeStruct((bk, n_mem, width), BF16)],
        compiler_params=_params(1),
        name="memkv",
    )(mem, w_xkv_b)


def _softmax_rows(s):
    m = jnp.max(s, axis=-1, keepdims=True)
    p = jnp.exp(s - m)
    return p * (1.0 / jnp.sum(p, axis=-1, keepdims=True))


XATTN_SUBTILE = 256


def _xattn_kernel(x_ref, mkv_ref, wq_ref, wo_ref, g_ref, b_ref, out_ref, *, alpha):
    tm, d = x_ref.shape[1:]
    hd = d // MEM_HEADS
    sub = min(tm, XATTN_SUBTILE)
    for r0 in range(0, tm, sub):
        x = x_ref[0, r0:r0 + sub, :]
        qx = jnp.dot(x.astype(BF16), wq_ref[...], preferred_element_type=F32)
        heads = []
        for h in range(MEM_HEADS):
            qh = qx[:, h * hd:(h + 1) * hd].astype(BF16)
            kh = mkv_ref[0, :, h * hd:(h + 1) * hd].astype(BF16)
            vh = mkv_ref[0, :, d + h * hd:d + (h + 1) * hd].astype(BF16)
            s = lax.dot_general(qh, kh, (((1,), (1,)), ((), ())), preferred_element_type=F32)
            m = jnp.max(s, axis=-1, keepdims=True)
            p = jnp.exp(s - m)
            inv = 1.0 / jnp.sum(p, axis=-1, keepdims=True)
            heads.append((jnp.dot(p.astype(BF16), vh, preferred_element_type=F32) * inv).astype(BF16))
        y = alpha * x + jnp.dot(jnp.concatenate(heads, axis=1), wo_ref[...], preferred_element_type=F32)
        out_ref[0, r0:r0 + sub, :] = _layer_norm(y, g_ref[...], b_ref[...])


def _xattn(x, mkv, wq_b, wo_b, g, b, tm, alpha):
    bk, tk, d = x.shape
    n_mem = mkv.shape[1]
    row = pl.BlockSpec((1, tm, d), lambda i, j: (i, j, 0))
    return pl.pallas_call(
        functools.partial(_xattn_kernel, alpha=alpha),
        grid=(bk, tk // tm),
        in_specs=[row, pl.BlockSpec((1, n_mem, 2 * d), lambda i, j: (i, 0, 0)),
                  _const_spec(wq_b.shape), _const_spec(wo_b.shape), _const_spec((1, d)), _const_spec((1, d))],
        out_specs=row,
        out_shape=jax.ShapeDtypeStruct((bk, tk, d), F32),
        compiler_params=_params(2),
        name="xattn",
    )(x, mkv, wq_b, wo_b, g, b)


def _sample_xattn_kernel(x_ref, mkv_ref, wq_ref, wo_ref, g_ref, b_ref, out_ref, qx_ref, ox_ref,
                         *, dec_seq, seqs_per_step, alpha):
    step = pl.program_id(0)
    d = x_ref.shape[1]
    hd = d // MEM_HEADS
    seqs_per_slab = SUBLANES // dec_seq

    @pl.when(step == 0)
    def _():
        qx_ref[...] = jnp.dot(x_ref[...].astype(BF16), wq_ref[...], preferred_element_type=F32)

    n_chunks = hd // LANES
    rows_per_mem = 2 * n_chunks * MEM_HEADS
    n_mem = mkv_ref.shape[2] // rows_per_mem

    def head_matrix(b_local, kv, h):
        chunks = [mkv_ref[0, b_local, pl.ds((kv * n_chunks + c) * MEM_HEADS + h, n_mem, stride=rows_per_mem), :]
                  for c in range(n_chunks)]
        return jnp.concatenate(chunks, axis=1).astype(BF16)

    slab_row = lax.broadcasted_iota(jnp.int32, (SUBLANES, 1), 0)
    for j in range(seqs_per_step // seqs_per_slab):
        slab = pl.ds(pl.multiple_of((step * (seqs_per_step // seqs_per_slab) + j) * SUBLANES, SUBLANES), SUBLANES)
        for h in range(MEM_HEADS):
            qh = qx_ref[slab, h * hd:(h + 1) * hd].astype(BF16)
            o8 = jnp.zeros((SUBLANES, hd), F32)
            for e in range(seqs_per_slab):
                b_local = j * seqs_per_slab + e
                kh = head_matrix(b_local, 0, h)
                vh = head_matrix(b_local, 1, h)
                s = lax.dot_general(qh, kh, (((1,), (1,)), ((), ())), preferred_element_type=F32)
                oh = jnp.dot(_softmax_rows(s).astype(BF16), vh, preferred_element_type=F32)
                o8 = jnp.where(slab_row // dec_seq == e, oh, o8)
            ox_ref[slab, h * hd:(h + 1) * hd] = o8

    @pl.when(step == pl.num_programs(0) - 1)
    def _():
        y = alpha * x_ref[...] + jnp.dot(ox_ref[...].astype(BF16), wo_ref[...], preferred_element_type=F32)
        out_ref[...] = _layer_norm(y, g_ref[...], b_ref[...])


def _sample_xattn(x, mkv_flat, layer, wq_b, wo_b, g, b, dec_seq, seqs_per_step, alpha):
    rows, d = x.shape
    _, n_seq, flat_rows, _ = mkv_flat.shape
    return pl.pallas_call(
        functools.partial(_sample_xattn_kernel, dec_seq=dec_seq, seqs_per_step=seqs_per_step, alpha=alpha),
        grid=(n_seq // seqs_per_step,),
        in_specs=[_const_spec(x.shape),
                  pl.BlockSpec((1, seqs_per_step, flat_rows, LANES), lambda i: (layer, i, 0, 0)),
                  _const_spec(wq_b.shape), _const_spec(wo_b.shape), _const_spec((1, d)), _const_spec((1, d))],
        out_specs=pl.BlockSpec((rows, d), lambda i: (0, 0)),
        out_shape=jax.ShapeDtypeStruct((rows, d), F32),
        scratch_shapes=[pltpu.VMEM((rows, d), F32), pltpu.VMEM((rows, d), F32)],
        compiler_params=_params(1),
        name="sample_xattn",
    )(x, mkv_flat, wq_b, wo_b, g, b)


def _mlp_kernel(x_ref, wu_ref, wd_ref, g_ref, b_ref, out_ref, *, ff_chunk, alpha):
    x = x_ref[...]
    xb = x.astype(BF16)
    y = alpha * x
    for c in range(wu_ref.shape[1] // ff_chunk):
        h = jnp.dot(xb, wu_ref[:, c * ff_chunk:(c + 1) * ff_chunk], preferred_element_type=F32)
        h = jnp.square(jnp.maximum(h, 0.0)).astype(BF16)
        y = y + jnp.dot(h, wd_ref[c * ff_chunk:(c + 1) * ff_chunk, :], preferred_element_type=F32)
    out_ref[...] = _layer_norm(y, g_ref[...], b_ref[...])


def _mlp(x, wu_b, wd_b, g, b, tm, alpha, ff_chunk=1024):
    rows, d = x.shape
    row = pl.BlockSpec((tm, d), lambda i: (i, 0))
    return pl.pallas_call(
        functools.partial(_mlp_kernel, ff_chunk=ff_chunk, alpha=alpha),
        grid=(rows // tm,),
        in_specs=[row, _const_spec(wu_b.shape), _const_spec(wd_b.shape), _const_spec((1, d)), _const_spec((1, d))],
        out_specs=row,
        out_shape=jax.ShapeDtypeStruct((rows, d), F32),
        compiler_params=_params(1),
        name="mlp",
    )(x, wu_b, wd_b, g, b)


def kernel(x_prompt, x_sample, cache_kv_w128, cache_kv_w512, cache_kv_w2048, cache_mem_kv, mem_prompt,
           w_in, sgu_ln_g, sgu_ln_b, w_spatial, b_spatial, w_mix_out, ln1_g, ln1_b,
           w_xq, w_xkv, w_xo, ln2_g, ln2_b, w_up, w_down, ln3_g, ln3_b):
    depth = w_in.shape[0]
    bp, tp, d = x_prompt.shape
    bs, ts, _ = x_sample.shape
    past_len = 8192
    alpha = float((2 * depth) ** 0.25)
    att_width = len(DILATIONS) * GROUP_WIDTH
    sgu_width = sgu_ln_g.shape[1]
    sgu_groups = w_spatial.shape[1]
    n_mem = mem_prompt.shape[1]
    rows_s = bs * ts
    assert tp % (DILATIONS[-1] * WINDOW_STEPS) == 0 and SUBLANES % ts == 0 and rows_s % SGU_CHUNK == 0
    assert d // MEM_HEADS == GROUP_WIDTH and sgu_width == GROUP_WIDTH

    col_scale = jnp.concatenate([jnp.full((att_width,), HEAD_DIM ** -0.5, F32),
                                 jnp.ones((w_in.shape[2] - att_width,), F32)])
    w_in_b = (w_in * col_scale).astype(BF16)
    w_xq_b = (w_xq * (d // MEM_HEADS) ** -0.5).astype(BF16)
    w_mix_b, w_xkv_b, w_xo_b = w_mix_out.astype(BF16), w_xkv.astype(BF16), w_xo.astype(BF16)
    w_up_b, w_down_b = w_up.astype(BF16), w_down.astype(BF16)
    vec = lambda p, l: p[l][None, :]

    tables_p = _rope_tables(jnp.arange(tp, dtype=jnp.int32))
    tables_s = _rope_tables(past_len + (jnp.arange(rows_s, dtype=jnp.int32) % ts))

    eye = jnp.eye(rows_s // ts, dtype=F32)
    tri = jnp.tril(jnp.ones((ts, ts), F32))

    hp = x_prompt
    hs = x_sample.reshape(1, rows_s, d)
    out_rows_p = [[] for _ in DILATIONS]
    out_rows_s = [[] for _ in DILATIONS]
    out_mem, out_gv = [], []
    caches_t = [jnp.transpose(c, (0, 1, 3, 4, 5, 2)) for c in (cache_kv_w128, cache_kv_w512, cache_kv_w2048)]
    mem_chunks = d // MEM_HEADS // LANES
    mem_flat = (cache_mem_kv.reshape(depth, bs, n_mem, 2, MEM_HEADS, mem_chunks, LANES)
                .transpose(0, 1, 2, 3, 5, 4, 6).reshape(depth, bs, n_mem * 2 * mem_chunks * MEM_HEADS, LANES))
    for l in range(depth):
        mkv_f32, mkv = _memkv(mem_prompt, w_xkv_b[l])
        out_mem.append(mkv_f32.reshape(bp, n_mem, 2, MEM_HEADS, d // MEM_HEADS))
        q, kv0, kv1, kv2, u, gv = _inproj(hp, w_in_b[l], tables_p, vec(sgu_ln_g, l), vec(sgu_ln_b, l),
                                           tm=512, gate_dtype=BF16)
        for g, kv in enumerate((kv0, kv1, kv2)):
            keep = min(DILATIONS[g] * WINDOW_STEPS, tp)
            out_rows_p[g].append(kv[:, tp - keep:].reshape(bp, keep, 2, HEADS_PER_GROUP, HEAD_DIM))
        atts = _prompt_attention(q, [kv0, kv1, kv2])
        b_tile = jnp.repeat(b_spatial[l][:, :SGU_CHUNK].T, sgu_width // sgu_groups, axis=1)
        hp = _mix(atts, u, gv, hp, w_spatial[l], b_tile, w_mix_b[l],
                  vec(ln1_g, l), vec(ln1_b, l), tm=512, alpha=alpha)
        hp = _xattn(hp, mkv, w_xq_b[l], w_xo_b[l], vec(ln2_g, l), vec(ln2_b, l), tm=512, alpha=alpha)
        hp = _mlp(hp.reshape(bp * tp, d), w_up_b[l], w_down_b[l], vec(ln3_g, l), vec(ln3_b, l),
                  tm=512, alpha=alpha).reshape(bp, tp, d)

        q, kv0, kv1, kv2, u, gv = _inproj(hs, w_in_b[l], tables_s, vec(sgu_ln_g, l), vec(sgu_ln_b, l),
                                           tm=rows_s, gate_dtype=F32)
        out_gv.append(gv.reshape(bs, ts, sgu_width))
        for g, kv in enumerate((kv0, kv1, kv2)):
            out_rows_s[g].append(kv.reshape(bs, ts, 2, HEADS_PER_GROUP, HEAD_DIM))
        kvn = jnp.concatenate([kv0[0], kv1[0], kv2[0]], axis=1)
        att = _sample_attention(q[0], kvn, caches_t, l, dec_seq=ts)
        w_sp_s = jnp.einsum("ab,gts->gatbs", eye, w_spatial[l][:, :ts, :ts] * tri).reshape(sgu_groups, rows_s, rows_s)
        b_tile_s = jnp.repeat(jnp.tile(b_spatial[l][:, :ts].T, (rows_s // ts, 1)), sgu_width // sgu_groups, axis=1)
        hs = _mix([att[None]], u, gv, hs, w_sp_s, b_tile_s, w_mix_b[l],
                  vec(ln1_g, l), vec(ln1_b, l), tm=rows_s, alpha=alpha)
        hs2 = _sample_xattn(hs[0], mem_flat, l, w_xq_b[l], w_xo_b[l], vec(ln2_g, l), vec(ln2_b, l),
                            dec_seq=ts, seqs_per_step=4, alpha=alpha)
        hs = _mlp(hs2, w_up_b[l], w_down_b[l], vec(ln3_g, l), vec(ln3_b, l), tm=rows_s, alpha=alpha)[None]

    stack = lambda xs: jnp.stack(xs)
    return (hp, hs.reshape(bs, ts, d),
            stack(out_rows_p[0]), stack(out_rows_p[1]), stack(out_rows_p[2]), stack(out_mem),
            stack(out_rows_s[0]), stack(out_rows_s[1]), stack(out_rows_s[2]), stack(out_gv))
```

```python
import functools
import math

import jax
import jax.numpy as jnp
from jax import lax
from jax.experimental import pallas as pl
from jax.experimental.pallas import tpu as pltpu

F32 = jnp.float32
BF16 = jnp.bfloat16

HEAD_DIM = 64
HEADS_PER_GROUP = 4
GROUP_WIDTH = HEAD_DIM * HEADS_PER_GROUP
DILATIONS = (1, 4, 16)
WINDOW_STEPS = 128
ROT_DIM = 16
ROPE_THETA = 500000.0
SGU_CHUNK = 128
MEM_HEADS = 4
LN_EPS = 1e-5
NEG = -1e30
LANES = 128
SUBLANES = 8
VMEM_LIMIT = 56 * 1024 * 1024


def _params(n_grid_dims):
    return pltpu.CompilerParams(
        dimension_semantics=("arbitrary",) * n_grid_dims,
        vmem_limit_bytes=VMEM_LIMIT)


def _const_spec(shape):
    nd = len(shape)
    return pl.BlockSpec(shape, lambda *_: (0,) * nd, pipeline_mode=pl.Buffered(1))


def _layer_spec(stacked, layer):
    rest = stacked.shape[1:]
    return pl.BlockSpec((None,) + rest, lambda *_: (layer,) + (0,) * len(rest), pipeline_mode=pl.Buffered(1))


def _layer_norm(y, g, b):
    mu = jnp.mean(y, axis=-1, keepdims=True)
    yc = y - mu
    var = jnp.mean(yc * yc, axis=-1, keepdims=True)
    return yc * lax.rsqrt(var + LN_EPS) * g + b


def _gelu_tanh(x):
    return 0.5 * x * (1.0 + jnp.tanh(0.7978845608028654 * (x + 0.044715 * (x * x * x))))


def _head_half_mask(width, half):
    lane = lax.broadcasted_iota(jnp.int32, (1, width), 1)
    return (lane // HEAD_DIM) % 2 == half


def _inproj_kernel(*refs, att_width, sgu_width, windows, n_aliased):
    x_ref, w_ref, cos_ref, sin_lo_ref, sin_hi_ref, g_ref, b_ref = refs[:7]
    q_ref, kv0_ref, kv1_ref, kv2_ref, u_ref, gv_ref = refs[7 + n_aliased:13 + n_aliased]
    window_refs = refs[13 + n_aliased:]
    tm = x_ref.shape[1]
    xb = x_ref[0].astype(BF16)
    cos = cos_ref[...]
    sin_lo = sin_lo_ref[...]
    sin_hi = sin_hi_ref[...]

    def proj(c0, width):
        return jnp.dot(xb, w_ref[:, c0:c0 + width], preferred_element_type=F32)

    def rope(t):
        return t * cos + pltpu.roll(t, LANES - ROT_DIM // 2, 1) * sin_lo + pltpu.roll(t, ROT_DIM // 2, 1) * sin_hi

    def rope_group(t):
        return jnp.concatenate([rope(t[:, s:s + LANES]) for s in range(0, GROUP_WIDTH, LANES)], axis=1)

    u_ref[0] = _gelu_tanh(proj(3 * att_width, sgu_width)).astype(u_ref.dtype)
    gate = _gelu_tanh(proj(3 * att_width + sgu_width, sgu_width))
    gv_ref[0] = _layer_norm(gate, g_ref[...], b_ref[...]).astype(gv_ref.dtype)
    kv_refs = (kv0_ref, kv1_ref, kv2_ref)

    def write_window(win_ref, part, rows):
        t = rows.T
        for h in range(HEADS_PER_GROUP):
            win_ref[part, h] = t[h * HEAD_DIM:(h + 1) * HEAD_DIM, :]

    def write_window_tail(kv_ref, win_ref, n_rows):
        for part in range(2):
            write_window(win_ref, part, kv_ref[0, tm - n_rows:tm, part * GROUP_WIDTH:(part + 1) * GROUP_WIDTH])

    order = sorted(range(len(kv_refs)), key=lambda grp: not (windows and windows[grp][1]))
    for grp in order:
        c = grp * GROUP_WIDTH
        q_ref[0, :, c:c + GROUP_WIDTH] = rope_group(proj(c, GROUP_WIDTH))
        k = rope_group(proj(att_width + c, GROUP_WIDTH))
        kv_refs[grp][0, :, 0:GROUP_WIDTH] = k
        if windows and windows[grp][1]:
            write_window(window_refs[grp], 0, k)
    for grp in order:
        v = proj(2 * att_width + grp * GROUP_WIDTH, GROUP_WIDTH)
        kv_refs[grp][0, :, GROUP_WIDTH:2 * GROUP_WIDTH] = v
        if windows and windows[grp][1]:
            write_window(window_refs[grp], 1, v)
    for grp, (keep, every_tile) in enumerate(windows):
        if not every_tile:
            pl.when(pl.program_id(1) == pl.num_programs(1) - 1)(
                functools.partial(write_window_tail, kv_refs[grp], window_refs[grp], keep))


def _inproj(x, w_in_b, layer, tables, g, b, tm, gate_dtype, window_keeps=(), window_bufs=None):
    bk, tk, d = x.shape
    depth = w_in_b.shape[0]
    att_width = len(DILATIONS) * GROUP_WIDTH
    sgu_width = (w_in_b.shape[2] - 3 * att_width) // 2
    cos, sin_lo, sin_hi = tables
    row = lambda w: pl.BlockSpec((1, tm, w), lambda i, j: (i, j, 0))
    tab = pl.BlockSpec((tm, LANES), lambda i, j: (j, 0))
    out_shape = ([jax.ShapeDtypeStruct((bk, tk, w), F32) for w in (att_width,) + (2 * GROUP_WIDTH,) * len(DILATIONS)]
                 + [jax.ShapeDtypeStruct((bk, tk, sgu_width), gate_dtype)] * 2)
    out_specs = [row(s.shape[2]) for s in out_shape]
    windows = []
    for keep in window_keeps:
        every_tile = keep == tk
        assert every_tile or keep <= tm
        windows.append((keep, every_tile))
        out_shape.append(jax.ShapeDtypeStruct((depth, bk, 2, HEADS_PER_GROUP, HEAD_DIM, keep), F32))
        out_specs.append(pl.BlockSpec((None, None, 2, HEADS_PER_GROUP, HEAD_DIM, tm if every_tile else keep),
                                      (lambda i, j: (layer, i, 0, 0, 0, j)) if every_tile
                                      else (lambda i, j: (layer, i, 0, 0, 0, 0))))
    aliased = list(window_bufs or ())
    n_in = 7
    return pl.pallas_call(
        functools.partial(_inproj_kernel, att_width=att_width, sgu_width=sgu_width, windows=tuple(windows),
                          n_aliased=len(aliased)),
        grid=(bk, tk // tm),
        in_specs=[row(d), _layer_spec(w_in_b, layer), tab, tab, tab, _layer_spec(g, layer), _layer_spec(b, layer)]
        + [pl.BlockSpec(memory_space=pl.ANY)] * len(aliased),
        out_specs=out_specs,
        out_shape=out_shape,
        input_output_aliases={n_in + k: 6 + k for k in range(len(aliased))},
        compiler_params=_params(2),
        name="inproj",
    )(x, w_in_b, cos, sin_lo, sin_hi, g, b, *aliased)


def _rope_tables(pos):
    half = ROT_DIM // 2
    inv = ROPE_THETA ** (-jnp.arange(half, dtype=F32) / half)
    ang = pos.astype(F32)[:, None] * inv[None, :]
    cos, sin = jnp.cos(ang), jnp.sin(ang)
    zeros = jnp.zeros((pos.shape[0], HEAD_DIM - ROT_DIM), F32)
    zero_half = jnp.zeros_like(sin)
    cos_head = jnp.concatenate([cos, cos, zeros + 1.0], axis=1)
    lo_head = jnp.concatenate([-sin, zero_half, zeros], axis=1)
    hi_head = jnp.concatenate([zero_half, sin, zeros], axis=1)
    two = lambda t: jnp.concatenate([t, t], axis=1)
    return two(cos_head), two(lo_head), two(hi_head)


ATTN_MIX_ROWS = 256
ATTN_TILES_PER_BATCH = 8


def _attn_kernel(*refs, seq):
    n_groups = len(DILATIONS)
    q_refs, k_refs, v_refs = refs[0:n_groups], refs[n_groups:2 * n_groups], refs[2 * n_groups:3 * n_groups]
    att_refs = refs[3 * n_groups:4 * n_groups]
    o_s, lse_s = refs[4 * n_groups:]
    blk = WINDOW_STEPS
    x_idx = lax.broadcasted_iota(jnp.int32, (blk, 2 * blk), 0)
    k_idx = lax.broadcasted_iota(jnp.int32, (blk, 2 * blk), 1)
    band_mask = (k_idx >= x_idx) & (k_idx <= x_idx + blk)
    causal_mask = (lax.broadcasted_iota(jnp.int32, (blk, blk), 1)
                   <= lax.broadcasted_iota(jnp.int32, (blk, blk), 0))
    half_masks = [_head_half_mask(LANES, hh) for hh in range(2)]

    def batch(g, dil, blocks):
        def rows(r, first_block, n):
            start = r + dil * blk * first_block
            return pl.ds(start, n) if dil == 1 else pl.ds(start, n, stride=dil)

        first = blocks[0][1] == 0
        mask = causal_mask if first else band_mask
        scores, values = [], []
        for r, c in blocks:
            k_rows = rows(r, 0, blk) if first else rows(r, c - 1, 2 * blk)
            qp = q_refs[g][0, rows(r, c, blk), :]
            kp = k_refs[g][0, k_rows, :].astype(BF16)
            values.append(v_refs[g][0, k_rows, :].astype(BF16))
            for hh in range(2):
                qh = jnp.where(half_masks[hh], qp, 0.0).astype(BF16)
                s = lax.dot_general(qh, kp, (((1,), (1,)), ((), ())), preferred_element_type=F32)
                scores.append(jnp.where(mask, s, NEG))
        s_all = jnp.concatenate(scores, axis=0)
        m = jnp.max(s_all, axis=-1, keepdims=True)
        p = jnp.exp(s_all - m)
        den = jnp.sum(p, axis=-1, keepdims=True)
        p = p.astype(BF16)
        inv = 1.0 / den
        lse = m + jnp.log(den)
        for n, (r, c) in enumerate(blocks):
            piece = lambda t, hh: t[(2 * n + hh) * blk:(2 * n + hh + 1) * blk]
            outs = [jnp.dot(piece(p, hh), values[n], preferred_element_type=F32) * piece(inv, hh) for hh in range(2)]
            o_s[g, rows(r, c, blk), :] = jnp.where(half_masks[0], outs[0], outs[1])
            lse_s[g, rows(r, c, blk), :] = jnp.where(half_masks[0], piece(lse, 0), piece(lse, 1))

    for g, dil in enumerate(DILATIONS):
        n_blocks = seq // dil // blk
        first_blocks = [(r, 0) for r in range(dil)]
        later_blocks = [(r, c) for r in range(dil) for c in range(1, n_blocks)]
        per_first = ATTN_TILES_PER_BATCH // 2
        per_later = ATTN_TILES_PER_BATCH // 4
        for i in range(0, len(first_blocks), per_first):
            batch(g, dil, first_blocks[i:i + per_first])
        for i in range(0, len(later_blocks), per_later):
            batch(g, dil, later_blocks[i:i + per_later])

    for t0 in range(0, seq, ATTN_MIX_ROWS):
        rws = slice(t0, t0 + ATTN_MIX_ROWS)
        lses = [lse_s[g, rws, :] for g in range(n_groups)]
        top = functools.reduce(jnp.maximum, lses)
        es = [jnp.exp(l - top) for l in lses]
        inv = 1.0 / functools.reduce(jnp.add, es)
        for g in range(n_groups):
            att_refs[g][0, rws, :] = (o_s[g, rws, :] * (es[g] * inv)).astype(att_refs[g].dtype)


def _prompt_attention(q, kvs):
    bk, tk, _ = q.shape
    n_groups = len(DILATIONS)
    pairs = GROUP_WIDTH // LANES
    col = lambda first: pl.BlockSpec((1, tk, LANES), lambda i, j: (i, 0, first + j))
    out = jax.ShapeDtypeStruct((bk, tk, GROUP_WIDTH), BF16)
    return pl.pallas_call(
        functools.partial(_attn_kernel, seq=tk),
        grid=(bk, pairs),
        in_specs=([col(g * pairs) for g in range(n_groups)]
                  + [col(0)] * n_groups
                  + [col(pairs)] * n_groups),
        out_specs=[col(0)] * n_groups,
        out_shape=[out] * n_groups,
        scratch_shapes=[pltpu.VMEM((n_groups, tk, LANES), F32)] * 2,
        compiler_params=_params(2),
        name="attn",
    )(*([q] * n_groups), *kvs, *kvs)


def _sample_attn_kernel(q_ref, kvn_ref, c0_ref, c1_ref, c2_ref, att_ref, *, dec_seq):
    cache_refs = (c0_ref, c1_ref, c2_ref)
    seqs_per_slab = SUBLANES // dec_seq
    n_groups = len(DILATIONS)
    widest = DILATIONS[-1]
    slab = pl.ds(pl.multiple_of(pl.program_id(0) * SUBLANES, SUBLANES), SUBLANES)
    lane = lax.broadcasted_iota(jnp.int32, (1, LANES), 1)
    row8 = lax.broadcasted_iota(jnp.int32, (SUBLANES, LANES), 0)
    sel_rows = 2 * SUBLANES
    sel_row = lax.broadcasted_iota(jnp.int32, (sel_rows, LANES), 0)
    sel_lane = lax.broadcasted_iota(jnp.int32, (sel_rows, LANES), 1)
    query_class = lane % widest

    def one_hot(cond):
        return jnp.where(cond, 1.0, 0.0).astype(BF16)

    def spread(rows8, sel):
        padded = jnp.concatenate([rows8, jnp.zeros_like(rows8)], axis=0).astype(BF16)
        return lax.dot_general(padded, sel, (((0,), (0,)), ((), ())), preferred_element_type=F32)

    def class_fold(x, period, op):
        shift = LANES // 2
        while shift >= period:
            x = op(x, pltpu.roll(x, shift, 1))
            shift //= 2
        return x

    def valid_mask(n_tiles, cache_ok, new_ok):
        rows = SUBLANES * (n_tiles // SUBLANES + 1)
        r = lax.broadcasted_iota(jnp.int32, (rows, LANES), 0)
        p = lax.broadcasted_iota(jnp.int32, (rows, LANES), 1)
        return ((r < n_tiles) & cache_ok(p)) | ((r == n_tiles) & new_ok(p))

    def attend(qm, cache_ref, seq, head, knt, vnt, n_tiles, valid, period):
        groups = []
        for t0 in range(0, n_tiles + 1, SUBLANES):
            acc = jnp.full((SUBLANES, LANES), NEG, F32)
            for t in range(t0, min(t0 + SUBLANES, n_tiles + 1)):
                keys = knt if t == n_tiles else cache_ref[0, seq, 0, head, :, t * LANES:(t + 1) * LANES]
                acc = jnp.where(row8 == t - t0, jnp.sum(qm * keys, axis=0, keepdims=True), acc)
            groups.append(acc)
        scores = jnp.where(valid, groups[0] if len(groups) == 1 else jnp.concatenate(groups, axis=0), NEG)
        top = jnp.broadcast_to(jnp.max(scores, axis=0, keepdims=True), (SUBLANES, LANES))
        if period:
            top = class_fold(top, period, jnp.maximum)
        else:
            top = jnp.broadcast_to(jnp.max(top, axis=1, keepdims=True), (SUBLANES, LANES))
        probs = jnp.exp(scores - jnp.concatenate([top] * len(groups), axis=0))
        den = jnp.broadcast_to(jnp.sum(probs, axis=0, keepdims=True), (SUBLANES, LANES))
        num = jnp.zeros((HEAD_DIM, LANES), F32)
        for t in range(n_tiles + 1):
            vals = vnt if t == n_tiles else cache_ref[0, seq, 1, head, :, t * LANES:(t + 1) * LANES]
            num = num + vals * probs[t:t + 1, :]
        if period:
            return class_fold(num, period, jnp.add), top, class_fold(den, period, jnp.add)
        return (jnp.broadcast_to(jnp.sum(num, axis=1, keepdims=True), (HEAD_DIM, LANES)), top,
                jnp.broadcast_to(jnp.sum(den, axis=1, keepdims=True), (SUBLANES, LANES)))

    for pair in range(GROUP_WIDTH // LANES):
        mixed = [[None] * seqs_per_slab for _ in range(n_groups)]
        for e in range(seqs_per_slab):
            base = e * dec_seq
            sel_new = one_hot((sel_row == base + sel_lane) & (sel_lane < dec_seq))
            per_group = []
            for g, dil in enumerate(DILATIONS):
                off = pair * LANES
                knt = spread(kvn_ref[slab, 2 * g * GROUP_WIDTH + off:2 * g * GROUP_WIDTH + off + LANES], sel_new)
                vnt = spread(kvn_ref[slab, (2 * g + 1) * GROUP_WIDTH + off:(2 * g + 1) * GROUP_WIDTH + off + LANES], sel_new)
                q8 = q_ref[slab, g * GROUP_WIDTH + off:g * GROUP_WIDTH + off + LANES]
                n_tiles = cache_refs[g].shape[5] // LANES
                if dil == 1:
                    qms = [spread(q8, one_hot(sel_row == base + i)) for i in range(dec_seq)]
                    valids = [valid_mask(n_tiles, lambda p, i=i: p >= i, lambda p, i=i: p <= i) for i in range(dec_seq)]
                else:
                    qms = [spread(q8, one_hot((sel_row == base + sel_lane % dil) & (sel_lane % dil < dec_seq)))]
                    valids = [valid_mask(n_tiles, lambda p, dil=dil: p % dil < dec_seq, lambda p: p < dec_seq)]
                heads = []
                for hh in range(2):
                    hrows = slice(hh * HEAD_DIM, (hh + 1) * HEAD_DIM)
                    num = top = den = None
                    for i, (qm, valid) in enumerate(zip(qms, valids)):
                        n_i, t_i, d_i = attend(qm[hrows], cache_refs[g], e, 2 * pair + hh, knt[hrows], vnt[hrows],
                                               n_tiles, valid, dil if dil > 1 else 0)
                        if num is None:
                            num, top, den = n_i, t_i, d_i
                        else:
                            mine = query_class == i
                            num, top, den = jnp.where(mine, n_i, num), jnp.where(mine, t_i, top), jnp.where(mine, d_i, den)
                    heads.append([num, top, den])
                per_group.append(heads)
            for hh in range(2):
                tops = [per_group[g][hh][1] for g in range(n_groups)]
                peak = functools.reduce(jnp.maximum, tops)
                scales = [jnp.exp(t - peak) for t in tops]
                total = functools.reduce(jnp.add, [per_group[g][hh][2] * scales[g] for g in range(n_groups)])
                for g in range(n_groups):
                    per_group[g][hh] = per_group[g][hh][0] * (scales[g] / total)[0:1, :]
            for g in range(n_groups):
                mixed[g][e] = jnp.concatenate(per_group[g], axis=0)
        sel_out = one_hot(sel_lane == widest * (sel_row // dec_seq) + sel_row % dec_seq)
        for g in range(n_groups):
            both = mixed[g][0]
            for e in range(1, seqs_per_slab):
                both = jnp.where((lane >= widest * e) & (lane < widest * (e + 1)), mixed[g][e], both)
            hi = both.astype(BF16)
            lo = (both - hi.astype(F32)).astype(BF16)
            dims = (((1,), (1,)), ((), ()))
            rows = (lax.dot_general(sel_out, hi, dims, preferred_element_type=F32)
                    + lax.dot_general(sel_out, lo, dims, preferred_element_type=F32))
            att_ref[slab, g * GROUP_WIDTH + pair * LANES:g * GROUP_WIDTH + (pair + 1) * LANES] = rows[:SUBLANES]


def _sample_attention(q, kvn, caches_t, layer, dec_seq):
    rows, width = q.shape
    seqs_per_slab = SUBLANES // dec_seq
    cache_spec = lambda c: pl.BlockSpec((1, seqs_per_slab) + c.shape[2:], lambda i: (layer, i, 0, 0, 0, 0))
    return pl.pallas_call(
        functools.partial(_sample_attn_kernel, dec_seq=dec_seq),
        grid=(rows // SUBLANES,),
        in_specs=[_const_spec(q.shape), _const_spec(kvn.shape)] + [cache_spec(c) for c in caches_t],
        out_specs=pl.BlockSpec((rows, width), lambda i: (0, 0)),
        out_shape=jax.ShapeDtypeStruct((rows, width), F32),
        compiler_params=_params(1),
        name="sample_attn",
    )(q, kvn, *caches_t)


MIX_SUBTILE = 256


def _mix_kernel(*refs, tm, alpha):
    n_att = len(refs) - 9
    u_ref, gv_ref, x_ref, wsp_ref, bsp_ref, wm_ref, g_ref, b_ref, out_ref = refs[n_att:]
    t_idx = lax.broadcasted_iota(jnp.int32, (SGU_CHUNK, SGU_CHUNK), 0)
    s_idx = lax.broadcasted_iota(jnp.int32, (SGU_CHUNK, SGU_CHUNK), 1)
    w_sp = [jnp.where(s_idx <= t_idx, wsp_ref[k], 0.0).astype(BF16) for k in range(wsp_ref.shape[0])]
    sgu_width = gv_ref.shape[2]

    def gated_chunk(crow):
        slabs = []
        for pair in range(sgu_width // LANES):
            gp = gv_ref[0, crow, pair * LANES:(pair + 1) * LANES]
            mixed = jnp.zeros((SGU_CHUNK, LANES), F32)
            for hh in range(2):
                gm = jnp.where(_head_half_mask(LANES, hh), gp, 0.0).astype(BF16)
                mixed = mixed + jnp.dot(w_sp[2 * pair + hh], gm, preferred_element_type=F32)
            slabs.append(mixed)
        return (u_ref[0, crow, :] * (jnp.concatenate(slabs, axis=1) + bsp_ref[...])).astype(BF16)

    sub = min(tm, MIX_SUBTILE)
    for r0 in range(0, tm, sub):
        rws = slice(r0, r0 + sub)
        sgu = jnp.concatenate([gated_chunk(slice(c, c + SGU_CHUNK)) for c in range(r0, r0 + sub, SGU_CHUNK)], axis=0)
        lhs = jnp.concatenate([a[0, rws, :].astype(BF16) for a in refs[:n_att]] + [sgu], axis=1)
        y = alpha * x_ref[0, rws, :] + jnp.dot(lhs, wm_ref[...], preferred_element_type=F32)
        out_ref[0, rws, :] = _layer_norm(y, g_ref[...], b_ref[...])


def _mix(att_inputs, u, gv, x, w_sp, b_sp_tile, w_mix_b, layer, g, b, tm, alpha):
    bk, tk, d = x.shape
    sgu_width = u.shape[2]
    row = lambda w: pl.BlockSpec((1, tm, w), lambda i, j: (i, j, 0))
    return pl.pallas_call(
        functools.partial(_mix_kernel, tm=tm, alpha=alpha),
        grid=(bk, tk // tm),
        in_specs=[row(a.shape[2]) for a in att_inputs] + [
            row(sgu_width), row(sgu_width), row(d),
            _const_spec(w_sp.shape), _const_spec(b_sp_tile.shape), _layer_spec(w_mix_b, layer),
            _layer_spec(g, layer), _layer_spec(b, layer)],
        out_specs=row(d),
        out_shape=jax.ShapeDtypeStruct((bk, tk, d), F32),
        compiler_params=_params(2),
        name="mix",
    )(*att_inputs, u, gv, x, w_sp, b_sp_tile, w_mix_b, g, b)


def _mem_row_stride(d):
    return 2 * (d // MEM_HEADS // LANES) * MEM_HEADS


def _memkv_kernel(*refs):
    m_ref, w_ref = refs[:2]
    flat_ref, out_b_ref = refs[-2:]
    n_mem, width = out_b_ref.shape[1:]
    d = width // 2
    hd = d // MEM_HEADS
    n_chunks = hd // LANES
    mkv = jnp.dot(m_ref[0].astype(BF16), w_ref[...], preferred_element_type=F32)
    out_b_ref[0] = mkv.astype(BF16)
    for kv in range(2):
        for c in range(n_chunks):
            for h in range(MEM_HEADS):
                col = kv * d + h * hd + c * LANES
                flat_ref[pl.ds((kv * n_chunks + c) * MEM_HEADS + h, n_mem, stride=_mem_row_stride(d)), :] = (
                    mkv[:, col:col + LANES])


def _memkv(mem, w_xkv_b, layer, flat_buf=None):
    bk, n_mem, d = mem.shape
    depth, _, width = w_xkv_b.shape
    flat_rows = n_mem * _mem_row_stride(d)
    aliased = [] if flat_buf is None else [flat_buf]
    return pl.pallas_call(
        _memkv_kernel,
        grid=(bk,),
        in_specs=[pl.BlockSpec((1, n_mem, d), lambda i: (i, 0, 0)), _layer_spec(w_xkv_b, layer)]
        + [pl.BlockSpec(memory_space=pl.ANY)] * len(aliased),
        out_specs=[pl.BlockSpec((None, None, flat_rows, LANES), lambda i: (layer, i, 0, 0)),
                   pl.BlockSpec((1, n_mem, width), lambda i: (i, 0, 0))],
        out_shape=[jax.ShapeDtypeStruct((depth, bk, flat_rows, LANES), F32),
                   jax.ShapeDtypeStruct((bk, n_mem, width), BF16)],
        input_output_aliases={2: 0} if aliased else {},
        compiler_params=_params(1),
        name="memkv",
    )(mem, w_xkv_b, *aliased)


def _softmax_rows(s):
    m = jnp.max(s, axis=-1, keepdims=True)
    p = jnp.exp(s - m)
    return p * (1.0 / jnp.sum(p, axis=-1, keepdims=True))


XATTN_SUBTILE = 256


def _xattn_kernel(x_ref, mkv_ref, wq_ref, wo_ref, g_ref, b_ref, out_ref, *, alpha):
    tm, d = x_ref.shape[1:]
    hd = d // MEM_HEADS
    sub = min(tm, XATTN_SUBTILE)
    for r0 in range(0, tm, sub):
        x = x_ref[0, r0:r0 + sub, :]
        qx = jnp.dot(x.astype(BF16), wq_ref[...], preferred_element_type=F32)
        heads = []
        for h in range(MEM_HEADS):
            qh = qx[:, h * hd:(h + 1) * hd].astype(BF16)
            kh = mkv_ref[0, :, h * hd:(h + 1) * hd].astype(BF16)
            vh = mkv_ref[0, :, d + h * hd:d + (h + 1) * hd].astype(BF16)
            s = lax.dot_general(qh, kh, (((1,), (1,)), ((), ())), preferred_element_type=F32)
            m = jnp.max(s, axis=-1, keepdims=True)
            p = jnp.exp(s - m)
            inv = 1.0 / jnp.sum(p, axis=-1, keepdims=True)
            heads.append((jnp.dot(p.astype(BF16), vh, preferred_element_type=F32) * inv).astype(BF16))
        y = alpha * x + jnp.dot(jnp.concatenate(heads, axis=1), wo_ref[...], preferred_element_type=F32)
        out_ref[0, r0:r0 + sub, :] = _layer_norm(y, g_ref[...], b_ref[...])


def _xattn(x, mkv, wq_b, wo_b, layer, g, b, tm, alpha):
    bk, tk, d = x.shape
    n_mem = mkv.shape[1]
    row = pl.BlockSpec((1, tm, d), lambda i, j: (i, j, 0))
    return pl.pallas_call(
        functools.partial(_xattn_kernel, alpha=alpha),
        grid=(bk, tk // tm),
        in_specs=[row, pl.BlockSpec((1, n_mem, 2 * d), lambda i, j: (i, 0, 0)),
                  _layer_spec(wq_b, layer), _layer_spec(wo_b, layer), _layer_spec(g, layer), _layer_spec(b, layer)],
        out_specs=row,
        out_shape=jax.ShapeDtypeStruct((bk, tk, d), F32),
        compiler_params=_params(2),
        name="xattn",
    )(x, mkv, wq_b, wo_b, g, b)


def _sample_xattn_kernel(x_ref, mkv_ref, wq_ref, wo_ref, g_ref, b_ref, out_ref, qx_ref, ox_ref,
                         *, dec_seq, seqs_per_step, alpha):
    step = pl.program_id(0)
    d = x_ref.shape[1]
    hd = d // MEM_HEADS
    seqs_per_slab = SUBLANES // dec_seq

    @pl.when(step == 0)
    def _():
        qx_ref[...] = jnp.dot(x_ref[...].astype(BF16), wq_ref[...], preferred_element_type=F32)

    n_chunks = hd // LANES
    rows_per_mem = 2 * n_chunks * MEM_HEADS
    n_mem = mkv_ref.shape[2] // rows_per_mem

    def head_matrix(b_local, kv, h):
        chunks = [mkv_ref[0, b_local, pl.ds((kv * n_chunks + c) * MEM_HEADS + h, n_mem, stride=rows_per_mem), :]
                  for c in range(n_chunks)]
        return jnp.concatenate(chunks, axis=1).astype(BF16)

    slab_row = lax.broadcasted_iota(jnp.int32, (SUBLANES, 1), 0)
    for j in range(seqs_per_step // seqs_per_slab):
        slab = pl.ds(pl.multiple_of((step * (seqs_per_step // seqs_per_slab) + j) * SUBLANES, SUBLANES), SUBLANES)
        for h in range(MEM_HEADS):
            qh = qx_ref[slab, h * hd:(h + 1) * hd].astype(BF16)
            o8 = jnp.zeros((SUBLANES, hd), F32)
            for e in range(seqs_per_slab):
                b_local = j * seqs_per_slab + e
                kh = head_matrix(b_local, 0, h)
                vh = head_matrix(b_local, 1, h)
                s = lax.dot_general(qh, kh, (((1,), (1,)), ((), ())), preferred_element_type=F32)
                oh = jnp.dot(_softmax_rows(s).astype(BF16), vh, preferred_element_type=F32)
                o8 = jnp.where(slab_row // dec_seq == e, oh, o8)
            ox_ref[slab, h * hd:(h + 1) * hd] = o8

    @pl.when(step == pl.num_programs(0) - 1)
    def _():
        y = alpha * x_ref[...] + jnp.dot(ox_ref[...].astype(BF16), wo_ref[...], preferred_element_type=F32)
        out_ref[...] = _layer_norm(y, g_ref[...], b_ref[...])


def _sample_xattn(x, mkv_flat, layer, wq_b, wo_b, g, b, dec_seq, seqs_per_step, alpha):
    rows, d = x.shape
    _, n_seq, flat_rows, _ = mkv_flat.shape
    return pl.pallas_call(
        functools.partial(_sample_xattn_kernel, dec_seq=dec_seq, seqs_per_step=seqs_per_step, alpha=alpha),
        grid=(n_seq // seqs_per_step,),
        in_specs=[_const_spec(x.shape),
                  pl.BlockSpec((1, seqs_per_step, flat_rows, LANES), lambda i: (layer, i, 0, 0)),
                  _layer_spec(wq_b, layer), _layer_spec(wo_b, layer), _layer_spec(g, layer), _layer_spec(b, layer)],
        out_specs=pl.BlockSpec((rows, d), lambda i: (0, 0)),
        out_shape=jax.ShapeDtypeStruct((rows, d), F32),
        scratch_shapes=[pltpu.VMEM((rows, d), F32), pltpu.VMEM((rows, d), F32)],
        compiler_params=_params(1),
        name="sample_xattn",
    )(x, mkv_flat, wq_b, wo_b, g, b)


def _mlp_kernel(x_ref, wu_ref, wd_ref, g_ref, b_ref, out_ref, *, ff_chunk, alpha):
    x = x_ref[...]
    xb = x.astype(BF16)
    y = alpha * x
    for c in range(wu_ref.shape[1] // ff_chunk):
        h = jnp.dot(xb, wu_ref[:, c * ff_chunk:(c + 1) * ff_chunk], preferred_element_type=F32)
        h = jnp.square(jnp.maximum(h, 0.0)).astype(BF16)
        y = y + jnp.dot(h, wd_ref[c * ff_chunk:(c + 1) * ff_chunk, :], preferred_element_type=F32)
    out_ref[...] = _layer_norm(y, g_ref[...], b_ref[...])


def _mlp(x, wu_b, wd_b, layer, g, b, tm, alpha, ff_chunk=1024):
    rows, d = x.shape
    row = pl.BlockSpec((tm, d), lambda i: (i, 0))
    return pl.pallas_call(
        functools.partial(_mlp_kernel, ff_chunk=ff_chunk, alpha=alpha),
        grid=(rows // tm,),
        in_specs=[row, _layer_spec(wu_b, layer), _layer_spec(wd_b, layer), _layer_spec(g, layer), _layer_spec(b, layer)],
        out_specs=row,
        out_shape=jax.ShapeDtypeStruct((rows, d), F32),
        compiler_params=_params(1),
        name="mlp",
    )(x, wu_b, wd_b, g, b)


def kernel(x_prompt, x_sample, cache_kv_w128, cache_kv_w512, cache_kv_w2048, cache_mem_kv, mem_prompt,
           w_in, sgu_ln_g, sgu_ln_b, w_spatial, b_spatial, w_mix_out, ln1_g, ln1_b,
           w_xq, w_xkv, w_xo, ln2_g, ln2_b, w_up, w_down, ln3_g, ln3_b):
    depth = w_in.shape[0]
    bp, tp, d = x_prompt.shape
    bs, ts, _ = x_sample.shape
    past_len = 8192
    alpha = float((2 * depth) ** 0.25)
    att_width = len(DILATIONS) * GROUP_WIDTH
    sgu_width = sgu_ln_g.shape[1]
    sgu_groups = w_spatial.shape[1]
    n_mem = mem_prompt.shape[1]
    rows_s = bs * ts
    assert tp % (DILATIONS[-1] * WINDOW_STEPS) == 0 and SUBLANES % ts == 0 and rows_s % SGU_CHUNK == 0
    assert d // MEM_HEADS == GROUP_WIDTH and sgu_width == GROUP_WIDTH

    col_scale = jnp.concatenate([jnp.full((att_width,), HEAD_DIM ** -0.5, F32),
                                 jnp.ones((w_in.shape[2] - att_width,), F32)])
    w_in_b = (w_in * col_scale).astype(BF16)
    w_xq_b = (w_xq * (d // MEM_HEADS) ** -0.5).astype(BF16)
    w_mix_b, w_xkv_b, w_xo_b = w_mix_out.astype(BF16), w_xkv.astype(BF16), w_xo.astype(BF16)
    w_up_b, w_down_b = w_up.astype(BF16), w_down.astype(BF16)
    vec = lambda p: p[:, None, :]
    sgu_g, sgu_b, g1, b1, g2, b2, g3, b3 = map(vec, (sgu_ln_g, sgu_ln_b, ln1_g, ln1_b, ln2_g, ln2_b, ln3_g, ln3_b))

    tables_p = _rope_tables(jnp.arange(tp, dtype=jnp.int32))
    tables_s = _rope_tables(past_len + (jnp.arange(rows_s, dtype=jnp.int32) % ts))

    eye = jnp.eye(rows_s // ts, dtype=F32)
    tri = jnp.tril(jnp.ones((ts, ts), F32))

    hp = x_prompt
    hs = x_sample.reshape(1, rows_s, d)
    out_rows_s = [[] for _ in DILATIONS]
    out_gv = []
    window_keeps = tuple(min(dil * WINDOW_STEPS, tp) for dil in DILATIONS)
    windows = mem_out = None
    caches_t = [jnp.transpose(c, (0, 1, 3, 4, 5, 2)) for c in (cache_kv_w128, cache_kv_w512, cache_kv_w2048)]
    mem_chunks = d // MEM_HEADS // LANES
    mem_flat = (cache_mem_kv.reshape(depth, bs, n_mem, 2, MEM_HEADS, mem_chunks, LANES)
                .transpose(0, 1, 2, 3, 5, 4, 6).reshape(depth, bs, n_mem * 2 * mem_chunks * MEM_HEADS, LANES))
    for l in range(depth):
        mem_out, mkv = _memkv(mem_prompt, w_xkv_b, l, mem_out)
        q, kv0, kv1, kv2, u, gv, *windows = _inproj(hp, w_in_b, l, tables_p, sgu_g, sgu_b, tm=512, gate_dtype=BF16,
                                                      window_keeps=window_keeps, window_bufs=windows)
        atts = _prompt_attention(q, [kv0, kv1, kv2])
        b_tile = jnp.repeat(b_spatial[l][:, :SGU_CHUNK].T, sgu_width // sgu_groups, axis=1)
        hp = _mix(atts, u, gv, hp, w_spatial[l], b_tile, w_mix_b, l, g1, b1, tm=512, alpha=alpha)
        hp = _xattn(hp, mkv, w_xq_b, w_xo_b, l, g2, b2, tm=512, alpha=alpha)
        hp = _mlp(hp.reshape(bp * tp, d), w_up_b, w_down_b, l, g3, b3, tm=512, alpha=alpha).reshape(bp, tp, d)

        q, kv0, kv1, kv2, u, gv = _inproj(hs, w_in_b, l, tables_s, sgu_g, sgu_b, tm=rows_s, gate_dtype=F32)
        out_gv.append(gv.reshape(bs, ts, sgu_width))
        for g, kv in enumerate((kv0, kv1, kv2)):
            out_rows_s[g].append(kv.reshape(bs, ts, 2, HEADS_PER_GROUP, HEAD_DIM))
        kvn = jnp.concatenate([kv0[0], kv1[0], kv2[0]], axis=1)
        att = _sample_attention(q[0], kvn, caches_t, l, dec_seq=ts)
        w_sp_s = jnp.einsum("ab,gts->gatbs", eye, w_spatial[l][:, :ts, :ts] * tri).reshape(sgu_groups, rows_s, rows_s)
        b_tile_s = jnp.repeat(jnp.tile(b_spatial[l][:, :ts].T, (rows_s // ts, 1)), sgu_width // sgu_groups, axis=1)
        hs = _mix([att[None]], u, gv, hs, w_sp_s, b_tile_s, w_mix_b, l, g1, b1, tm=rows_s, alpha=alpha)
        hs2 = _sample_xattn(hs[0], mem_flat, l, w_xq_b, w_xo_b, g2, b2, dec_seq=ts, seqs_per_step=4, alpha=alpha)
        hs = _mlp(hs2, w_up_b, w_down_b, l, g3, b3, tm=rows_s, alpha=alpha)[None]

    rows_p = [jnp.transpose(w, (0, 1, 5, 2, 3, 4)) for w in windows]
    mem_p = (mem_out.reshape(depth, bp, n_mem, 2, mem_chunks, MEM_HEADS, LANES)
             .transpose(0, 1, 2, 3, 5, 4, 6).reshape(depth, bp, n_mem, 2, MEM_HEADS, d // MEM_HEADS))
    stack = lambda xs: jnp.stack(xs)
    return (hp, hs.reshape(bs, ts, d), rows_p[0], rows_p[1], rows_p[2], mem_p,
            stack(out_rows_s[0]), stack(out_rows_s[1]), stack(out_rows_s[2]), stack(out_gv))
```

```python
import functools
import math

import jax
import jax.numpy as jnp
from jax import lax
from jax.experimental import pallas as pl
from jax.experimental.pallas import tpu as pltpu

F32 = jnp.float32
BF16 = jnp.bfloat16

HEAD_DIM = 64
HEADS_PER_GROUP = 4
GROUP_WIDTH = HEAD_DIM * HEADS_PER_GROUP
DILATIONS = (1, 4, 16)
WINDOW_STEPS = 128
ROT_DIM = 16
ROPE_THETA = 500000.0
SGU_CHUNK = 128
MEM_HEADS = 4
LN_EPS = 1e-5
NEG = -1e30
LANES = 128
SUBLANES = 8
VMEM_LIMIT = 56 * 1024 * 1024


def _params(n_grid_dims):
    return pltpu.CompilerParams(
        dimension_semantics=("arbitrary",) * n_grid_dims,
        vmem_limit_bytes=VMEM_LIMIT)


def _const_spec(shape):
    nd = len(shape)
    return pl.BlockSpec(shape, lambda *_: (0,) * nd, pipeline_mode=pl.Buffered(1))


def _layer_spec(stacked, layer):
    rest = stacked.shape[1:]
    return pl.BlockSpec((None,) + rest, lambda *_: (layer,) + (0,) * len(rest), pipeline_mode=pl.Buffered(1))


def _layer_norm(y, g, b):
    mu = jnp.mean(y, axis=-1, keepdims=True)
    yc = y - mu
    var = jnp.mean(yc * yc, axis=-1, keepdims=True)
    return yc * lax.rsqrt(var + LN_EPS) * g + b


def _gelu_tanh(x):
    return 0.5 * x * (1.0 + jnp.tanh(0.7978845608028654 * (x + 0.044715 * (x * x * x))))


def _head_half_mask(width, half):
    lane = lax.broadcasted_iota(jnp.int32, (1, width), 1)
    return (lane // HEAD_DIM) % 2 == half


def _inproj_kernel(*refs, att_width, sgu_width, windows, n_aliased):
    x_ref, w_ref, cos_ref, sin_lo_ref, sin_hi_ref, g_ref, b_ref = refs[:7]
    q_ref, kv0_ref, kv1_ref, kv2_ref, u_ref, gv_ref = refs[7 + n_aliased:13 + n_aliased]
    window_refs = refs[13 + n_aliased:]
    tm = x_ref.shape[1]
    xb = x_ref[0].astype(BF16)
    cos = cos_ref[...]
    sin_lo = sin_lo_ref[...]
    sin_hi = sin_hi_ref[...]

    def proj(c0, width):
        return jnp.dot(xb, w_ref[:, c0:c0 + width], preferred_element_type=F32)

    def rope(t):
        return t * cos + pltpu.roll(t, LANES - ROT_DIM // 2, 1) * sin_lo + pltpu.roll(t, ROT_DIM // 2, 1) * sin_hi

    def rope_group(t):
        return jnp.concatenate([rope(t[:, s:s + LANES]) for s in range(0, GROUP_WIDTH, LANES)], axis=1)

    u_ref[0] = _gelu_tanh(proj(3 * att_width, sgu_width)).astype(u_ref.dtype)
    gate = _gelu_tanh(proj(3 * att_width + sgu_width, sgu_width))
    gv_ref[0] = _layer_norm(gate, g_ref[...], b_ref[...]).astype(gv_ref.dtype)
    kv_refs = (kv0_ref, kv1_ref, kv2_ref)

    def write_window(win_ref, part, rows):
        t = rows.T
        for h in range(HEADS_PER_GROUP):
            win_ref[part, h] = t[h * HEAD_DIM:(h + 1) * HEAD_DIM, :]

    def write_window_tail(kv_ref, win_ref, n_rows):
        for part in range(2):
            write_window(win_ref, part, kv_ref[0, tm - n_rows:tm, part * GROUP_WIDTH:(part + 1) * GROUP_WIDTH])

    order = sorted(range(len(kv_refs)), key=lambda grp: not (windows and windows[grp][1]))
    for grp in order:
        c = grp * GROUP_WIDTH
        q_ref[0, :, c:c + GROUP_WIDTH] = rope_group(proj(c, GROUP_WIDTH))
        k = rope_group(proj(att_width + c, GROUP_WIDTH))
        kv_refs[grp][0, :, 0:GROUP_WIDTH] = k
        if windows and windows[grp][1]:
            write_window(window_refs[grp], 0, k)
    for grp in order:
        v = proj(2 * att_width + grp * GROUP_WIDTH, GROUP_WIDTH)
        kv_refs[grp][0, :, GROUP_WIDTH:2 * GROUP_WIDTH] = v
        if windows and windows[grp][1]:
            write_window(window_refs[grp], 1, v)
    for grp, (keep, every_tile) in enumerate(windows):
        if not every_tile:
            pl.when(pl.program_id(1) == pl.num_programs(1) - 1)(
                functools.partial(write_window_tail, kv_refs[grp], window_refs[grp], keep))


def _inproj(x, w_in_b, layer, tables, g, b, tm, gate_dtype, window_keeps=(), window_bufs=None):
    bk, tk, d = x.shape
    depth = w_in_b.shape[0]
    att_width = len(DILATIONS) * GROUP_WIDTH
    sgu_width = (w_in_b.shape[2] - 3 * att_width) // 2
    cos, sin_lo, sin_hi = tables
    row = lambda w: pl.BlockSpec((1, tm, w), lambda i, j: (i, j, 0))
    tab = pl.BlockSpec((tm, LANES), lambda i, j: (j, 0))
    out_shape = ([jax.ShapeDtypeStruct((bk, tk, w), F32) for w in (att_width,) + (2 * GROUP_WIDTH,) * len(DILATIONS)]
                 + [jax.ShapeDtypeStruct((bk, tk, sgu_width), gate_dtype)] * 2)
    out_specs = [row(s.shape[2]) for s in out_shape]
    windows = []
    for keep in window_keeps:
        every_tile = keep == tk
        assert every_tile or keep <= tm
        windows.append((keep, every_tile))
        out_shape.append(jax.ShapeDtypeStruct((depth, bk, 2, HEADS_PER_GROUP, HEAD_DIM, keep), F32))
        out_specs.append(pl.BlockSpec((None, None, 2, HEADS_PER_GROUP, HEAD_DIM, tm if every_tile else keep),
                                      (lambda i, j: (layer, i, 0, 0, 0, j)) if every_tile
                                      else (lambda i, j: (layer, i, 0, 0, 0, 0))))
    aliased = list(window_bufs or ())
    n_in = 7
    return pl.pallas_call(
        functools.partial(_inproj_kernel, att_width=att_width, sgu_width=sgu_width, windows=tuple(windows),
                          n_aliased=len(aliased)),
        grid=(bk, tk // tm),
        in_specs=[row(d), _layer_spec(w_in_b, layer), tab, tab, tab, _layer_spec(g, layer), _layer_spec(b, layer)]
        + [pl.BlockSpec(memory_space=pl.ANY)] * len(aliased),
        out_specs=out_specs,
        out_shape=out_shape,
        input_output_aliases={n_in + k: 6 + k for k in range(len(aliased))},
        compiler_params=_params(2),
        name="inproj",
    )(x, w_in_b, cos, sin_lo, sin_hi, g, b, *aliased)


def _rope_tables(pos):
    half = ROT_DIM // 2
    inv = ROPE_THETA ** (-jnp.arange(half, dtype=F32) / half)
    ang = pos.astype(F32)[:, None] * inv[None, :]
    cos, sin = jnp.cos(ang), jnp.sin(ang)
    zeros = jnp.zeros((pos.shape[0], HEAD_DIM - ROT_DIM), F32)
    zero_half = jnp.zeros_like(sin)
    cos_head = jnp.concatenate([cos, cos, zeros + 1.0], axis=1)
    lo_head = jnp.concatenate([-sin, zero_half, zeros], axis=1)
    hi_head = jnp.concatenate([zero_half, sin, zeros], axis=1)
    two = lambda t: jnp.concatenate([t, t], axis=1)
    return two(cos_head), two(lo_head), two(hi_head)


ATTN_MIX_ROWS = 256
ATTN_TILES_PER_BATCH = 32


def _attn_kernel(*refs, seq):
    n_groups = len(DILATIONS)
    q_refs, k_refs, v_refs = refs[0:n_groups], refs[n_groups:2 * n_groups], refs[2 * n_groups:3 * n_groups]
    att_refs = refs[3 * n_groups:4 * n_groups]
    o_s, m_s, den_s = refs[4 * n_groups:]
    blk = WINDOW_STEPS
    x_idx = lax.broadcasted_iota(jnp.int32, (blk, 2 * blk), 0)
    k_idx = lax.broadcasted_iota(jnp.int32, (blk, 2 * blk), 1)
    band_mask = (k_idx >= x_idx) & (k_idx <= x_idx + blk)
    causal_mask = (lax.broadcasted_iota(jnp.int32, (blk, blk), 1)
                   <= lax.broadcasted_iota(jnp.int32, (blk, blk), 0))
    half_masks = [_head_half_mask(LANES, hh) for hh in range(2)]

    def batch(g, dil, blocks):
        def rows(r, first_block, n):
            start = r + dil * blk * first_block
            return pl.ds(start, n) if dil == 1 else pl.ds(start, n, stride=dil)

        first = blocks[0][1] == 0
        mask = causal_mask if first else band_mask
        scores, values = [], []
        for r, c in blocks:
            k_rows = rows(r, 0, blk) if first else rows(r, c - 1, 2 * blk)
            qp = q_refs[g][0, rows(r, c, blk), :]
            kp = k_refs[g][0, k_rows, :].astype(BF16)
            values.append(v_refs[g][0, k_rows, :].astype(BF16))
            for hh in range(2):
                qh = jnp.where(half_masks[hh], qp, 0.0).astype(BF16)
                s = lax.dot_general(qh, kp, (((1,), (1,)), ((), ())), preferred_element_type=F32)
                scores.append(jnp.where(mask, s, NEG))
        s_all = jnp.concatenate(scores, axis=0)
        m = jnp.max(s_all, axis=-1, keepdims=True)
        p = jnp.exp2(s_all - m)
        den = jnp.sum(p, axis=-1, keepdims=True)
        p = p.astype(BF16)
        for n, (r, c) in enumerate(blocks):
            piece = lambda t, hh: t[(2 * n + hh) * blk:(2 * n + hh + 1) * blk]
            outs = [jnp.dot(piece(p, hh), values[n], preferred_element_type=F32) for hh in range(2)]
            o_s[g, rows(r, c, blk), :] = jnp.where(half_masks[0], outs[0], outs[1])
            m_s[g, rows(r, c, blk), :] = jnp.where(half_masks[0], piece(m, 0), piece(m, 1))
            den_s[g, rows(r, c, blk), :] = jnp.where(half_masks[0], piece(den, 0), piece(den, 1))

    for g, dil in enumerate(DILATIONS):
        n_blocks = seq // dil // blk
        first_blocks = [(r, 0) for r in range(dil)]
        later_blocks = [(r, c) for r in range(dil) for c in range(1, n_blocks)]
        per_first = ATTN_TILES_PER_BATCH // 2
        per_later = ATTN_TILES_PER_BATCH // 4
        for i in range(0, len(first_blocks), per_first):
            batch(g, dil, first_blocks[i:i + per_first])
        for i in range(0, len(later_blocks), per_later):
            batch(g, dil, later_blocks[i:i + per_later])

    for t0 in range(0, seq, ATTN_MIX_ROWS):
        rws = slice(t0, t0 + ATTN_MIX_ROWS)
        tops = [m_s[g, rws, :] for g in range(n_groups)]
        peak = functools.reduce(jnp.maximum, tops)
        scales = [jnp.exp2(t - peak) for t in tops]
        inv = 1.0 / functools.reduce(jnp.add, [den_s[g, rws, :] * scales[g] for g in range(n_groups)])
        for g in range(n_groups):
            att_refs[g][0, rws, :] = (o_s[g, rws, :] * (scales[g] * inv)).astype(att_refs[g].dtype)


def _prompt_attention(q, kvs):
    bk, tk, _ = q.shape
    n_groups = len(DILATIONS)
    pairs = GROUP_WIDTH // LANES
    col = lambda first: pl.BlockSpec((1, tk, LANES), lambda i, j: (i, 0, first + j))
    out = jax.ShapeDtypeStruct((bk, tk, GROUP_WIDTH), BF16)
    return pl.pallas_call(
        functools.partial(_attn_kernel, seq=tk),
        grid=(bk, pairs),
        in_specs=([col(g * pairs) for g in range(n_groups)]
                  + [col(0)] * n_groups
                  + [col(pairs)] * n_groups),
        out_specs=[col(0)] * n_groups,
        out_shape=[out] * n_groups,
        scratch_shapes=[pltpu.VMEM((n_groups, tk, LANES), F32)] * 3,
        compiler_params=_params(2),
        name="attn",
    )(*([q] * n_groups), *kvs, *kvs)


def _sample_attn_kernel(q_ref, kvn_ref, c0_ref, c1_ref, c2_ref, att_ref, *, dec_seq):
    cache_refs = (c0_ref, c1_ref, c2_ref)
    seqs_per_slab = SUBLANES // dec_seq
    n_groups = len(DILATIONS)
    widest = DILATIONS[-1]
    slab = pl.ds(pl.multiple_of(pl.program_id(0) * SUBLANES, SUBLANES), SUBLANES)
    lane = lax.broadcasted_iota(jnp.int32, (1, LANES), 1)
    row8 = lax.broadcasted_iota(jnp.int32, (SUBLANES, LANES), 0)
    sel_rows = 2 * SUBLANES
    sel_row = lax.broadcasted_iota(jnp.int32, (sel_rows, LANES), 0)
    sel_lane = lax.broadcasted_iota(jnp.int32, (sel_rows, LANES), 1)
    query_class = lane % widest

    def one_hot(cond):
        return jnp.where(cond, 1.0, 0.0).astype(BF16)

    def spread(rows8, sel):
        padded = jnp.concatenate([rows8, jnp.zeros_like(rows8)], axis=0).astype(BF16)
        return lax.dot_general(padded, sel, (((0,), (0,)), ((), ())), preferred_element_type=F32)

    def class_fold(x, period, op):
        shift = LANES // 2
        while shift >= period:
            x = op(x, pltpu.roll(x, shift, 1))
            shift //= 2
        return x

    def valid_mask(n_tiles, cache_ok, new_ok):
        rows = SUBLANES * (n_tiles // SUBLANES + 1)
        r = lax.broadcasted_iota(jnp.int32, (rows, LANES), 0)
        p = lax.broadcasted_iota(jnp.int32, (rows, LANES), 1)
        return ((r < n_tiles) & cache_ok(p)) | ((r == n_tiles) & new_ok(p))

    def attend(qm, cache_ref, seq, head, knt, vnt, n_tiles, valid, period):
        groups = []
        for t0 in range(0, n_tiles + 1, SUBLANES):
            acc = jnp.full((SUBLANES, LANES), NEG, F32)
            for t in range(t0, min(t0 + SUBLANES, n_tiles + 1)):
                keys = knt if t == n_tiles else cache_ref[0, seq, 0, head, :, t * LANES:(t + 1) * LANES]
                acc = jnp.where(row8 == t - t0, jnp.sum(qm * keys, axis=0, keepdims=True), acc)
            groups.append(acc)
        scores = jnp.where(valid, groups[0] if len(groups) == 1 else jnp.concatenate(groups, axis=0), NEG)
        top = jnp.broadcast_to(jnp.max(scores, axis=0, keepdims=True), (SUBLANES, LANES))
        if period:
            top = class_fold(top, period, jnp.maximum)
        else:
            top = jnp.broadcast_to(jnp.max(top, axis=1, keepdims=True), (SUBLANES, LANES))
        probs = jnp.exp2(scores - jnp.concatenate([top] * len(groups), axis=0))
        den = jnp.broadcast_to(jnp.sum(probs, axis=0, keepdims=True), (SUBLANES, LANES))
        num = jnp.zeros((HEAD_DIM, LANES), F32)
        for t in range(n_tiles + 1):
            vals = vnt if t == n_tiles else cache_ref[0, seq, 1, head, :, t * LANES:(t + 1) * LANES]
            num = num + vals * probs[t:t + 1, :]
        if period:
            return class_fold(num, period, jnp.add), top, class_fold(den, period, jnp.add)
        return (jnp.broadcast_to(jnp.sum(num, axis=1, keepdims=True), (HEAD_DIM, LANES)), top,
                jnp.broadcast_to(jnp.sum(den, axis=1, keepdims=True), (SUBLANES, LANES)))

    for pair in range(GROUP_WIDTH // LANES):
        mixed = [[None] * seqs_per_slab for _ in range(n_groups)]
        for e in range(seqs_per_slab):
            base = e * dec_seq
            sel_new = one_hot((sel_row == base + sel_lane) & (sel_lane < dec_seq))
            per_group = []
            for g, dil in enumerate(DILATIONS):
                off = pair * LANES
                knt = spread(kvn_ref[slab, 2 * g * GROUP_WIDTH + off:2 * g * GROUP_WIDTH + off + LANES], sel_new)
                vnt = spread(kvn_ref[slab, (2 * g + 1) * GROUP_WIDTH + off:(2 * g + 1) * GROUP_WIDTH + off + LANES], sel_new)
                q8 = q_ref[slab, g * GROUP_WIDTH + off:g * GROUP_WIDTH + off + LANES]
                n_tiles = cache_refs[g].shape[5] // LANES
                if dil == 1:
                    qms = [spread(q8, one_hot(sel_row == base + i)) for i in range(dec_seq)]
                    valids = [valid_mask(n_tiles, lambda p, i=i: p >= i, lambda p, i=i: p <= i) for i in range(dec_seq)]
                else:
                    qms = [spread(q8, one_hot((sel_row == base + sel_lane % dil) & (sel_lane % dil < dec_seq)))]
                    valids = [valid_mask(n_tiles, lambda p, dil=dil: p % dil < dec_seq, lambda p: p < dec_seq)]
                heads = []
                for hh in range(2):
                    hrows = slice(hh * HEAD_DIM, (hh + 1) * HEAD_DIM)
                    num = top = den = None
                    for i, (qm, valid) in enumerate(zip(qms, valids)):
                        n_i, t_i, d_i = attend(qm[hrows], cache_refs[g], e, 2 * pair + hh, knt[hrows], vnt[hrows],
                                               n_tiles, valid, dil if dil > 1 else 0)
                        if num is None:
                            num, top, den = n_i, t_i, d_i
                        else:
                            mine = query_class == i
                            num, top, den = jnp.where(mine, n_i, num), jnp.where(mine, t_i, top), jnp.where(mine, d_i, den)
                    heads.append([num, top, den])
                per_group.append(heads)
            for hh in range(2):
                tops = [per_group[g][hh][1] for g in range(n_groups)]
                peak = functools.reduce(jnp.maximum, tops)
                scales = [jnp.exp2(t - peak) for t in tops]
                total = functools.reduce(jnp.add, [per_group[g][hh][2] * scales[g] for g in range(n_groups)])
                for g in range(n_groups):
                    per_group[g][hh] = per_group[g][hh][0] * (scales[g] / total)[0:1, :]
            for g in range(n_groups):
                mixed[g][e] = jnp.concatenate(per_group[g], axis=0)
        sel_out = one_hot(sel_lane == widest * (sel_row // dec_seq) + sel_row % dec_seq)
        for g in range(n_groups):
            both = mixed[g][0]
            for e in range(1, seqs_per_slab):
                both = jnp.where((lane >= widest * e) & (lane < widest * (e + 1)), mixed[g][e], both)
            hi = both.astype(BF16)
            lo = (both - hi.astype(F32)).astype(BF16)
            dims = (((1,), (1,)), ((), ()))
            rows = (lax.dot_general(sel_out, hi, dims, preferred_element_type=F32)
                    + lax.dot_general(sel_out, lo, dims, preferred_element_type=F32))
            att_ref[slab, g * GROUP_WIDTH + pair * LANES:g * GROUP_WIDTH + (pair + 1) * LANES] = rows[:SUBLANES]


def _sample_attention(q, kvn, caches_t, layer, dec_seq):
    rows, width = q.shape
    seqs_per_slab = SUBLANES // dec_seq
    cache_spec = lambda c: pl.BlockSpec((1, seqs_per_slab) + c.shape[2:], lambda i: (layer, i, 0, 0, 0, 0))
    return pl.pallas_call(
        functools.partial(_sample_attn_kernel, dec_seq=dec_seq),
        grid=(rows // SUBLANES,),
        in_specs=[_const_spec(q.shape), _const_spec(kvn.shape)] + [cache_spec(c) for c in caches_t],
        out_specs=pl.BlockSpec((rows, width), lambda i: (0, 0)),
        out_shape=jax.ShapeDtypeStruct((rows, width), F32),
        compiler_params=_params(1),
        name="sample_attn",
    )(q, kvn, *caches_t)


MIX_SUBTILE = 256


def _mix_kernel(*refs, tm, alpha):
    n_att = len(refs) - 9
    u_ref, gv_ref, x_ref, wsp_ref, bsp_ref, wm_ref, g_ref, b_ref, out_ref = refs[n_att:]
    t_idx = lax.broadcasted_iota(jnp.int32, (SGU_CHUNK, SGU_CHUNK), 0)
    s_idx = lax.broadcasted_iota(jnp.int32, (SGU_CHUNK, SGU_CHUNK), 1)
    w_sp = [jnp.where(s_idx <= t_idx, wsp_ref[k], 0.0).astype(BF16) for k in range(wsp_ref.shape[0])]
    sgu_width = gv_ref.shape[2]

    def gated_chunk(crow):
        slabs = []
        for pair in range(sgu_width // LANES):
            gp = gv_ref[0, crow, pair * LANES:(pair + 1) * LANES]
            mixed = jnp.zeros((SGU_CHUNK, LANES), F32)
            for hh in range(2):
                gm = jnp.where(_head_half_mask(LANES, hh), gp, 0.0).astype(BF16)
                mixed = mixed + jnp.dot(w_sp[2 * pair + hh], gm, preferred_element_type=F32)
            slabs.append(mixed)
        return (u_ref[0, crow, :] * (jnp.concatenate(slabs, axis=1) + bsp_ref[...])).astype(BF16)

    sub = min(tm, MIX_SUBTILE)
    for r0 in range(0, tm, sub):
        rws = slice(r0, r0 + sub)
        sgu = jnp.concatenate([gated_chunk(slice(c, c + SGU_CHUNK)) for c in range(r0, r0 + sub, SGU_CHUNK)], axis=0)
        lhs = jnp.concatenate([a[0, rws, :].astype(BF16) for a in refs[:n_att]] + [sgu], axis=1)
        y = alpha * x_ref[0, rws, :] + jnp.dot(lhs, wm_ref[...], preferred_element_type=F32)
        out_ref[0, rws, :] = _layer_norm(y, g_ref[...], b_ref[...])


def _mix(att_inputs, u, gv, x, w_sp, b_sp_tile, w_mix_b, layer, g, b, tm, alpha):
    bk, tk, d = x.shape
    sgu_width = u.shape[2]
    row = lambda w: pl.BlockSpec((1, tm, w), lambda i, j: (i, j, 0))
    return pl.pallas_call(
        functools.partial(_mix_kernel, tm=tm, alpha=alpha),
        grid=(bk, tk // tm),
        in_specs=[row(a.shape[2]) for a in att_inputs] + [
            row(sgu_width), row(sgu_width), row(d),
            _const_spec(w_sp.shape), _const_spec(b_sp_tile.shape), _layer_spec(w_mix_b, layer),
            _layer_spec(g, layer), _layer_spec(b, layer)],
        out_specs=row(d),
        out_shape=jax.ShapeDtypeStruct((bk, tk, d), F32),
        compiler_params=_params(2),
        name="mix",
    )(*att_inputs, u, gv, x, w_sp, b_sp_tile, w_mix_b, g, b)


def _mem_row_stride(d):
    return 2 * (d // MEM_HEADS // LANES) * MEM_HEADS


def _memkv_kernel(*refs):
    m_ref, w_ref = refs[:2]
    flat_ref, out_b_ref = refs[-2:]
    n_mem, width = out_b_ref.shape[1:]
    d = width // 2
    hd = d // MEM_HEADS
    n_chunks = hd // LANES
    mkv = jnp.dot(m_ref[0].astype(BF16), w_ref[...], preferred_element_type=F32)
    out_b_ref[0] = mkv.astype(BF16)
    for kv in range(2):
        for c in range(n_chunks):
            for h in range(MEM_HEADS):
                col = kv * d + h * hd + c * LANES
                flat_ref[pl.ds((kv * n_chunks + c) * MEM_HEADS + h, n_mem, stride=_mem_row_stride(d)), :] = (
                    mkv[:, col:col + LANES])


def _memkv(mem, w_xkv_b, layer, flat_buf=None):
    bk, n_mem, d = mem.shape
    depth, _, width = w_xkv_b.shape
    flat_rows = n_mem * _mem_row_stride(d)
    aliased = [] if flat_buf is None else [flat_buf]
    return pl.pallas_call(
        _memkv_kernel,
        grid=(bk,),
        in_specs=[pl.BlockSpec((1, n_mem, d), lambda i: (i, 0, 0)), _layer_spec(w_xkv_b, layer)]
        + [pl.BlockSpec(memory_space=pl.ANY)] * len(aliased),
        out_specs=[pl.BlockSpec((None, None, flat_rows, LANES), lambda i: (layer, i, 0, 0)),
                   pl.BlockSpec((1, n_mem, width), lambda i: (i, 0, 0))],
        out_shape=[jax.ShapeDtypeStruct((depth, bk, flat_rows, LANES), F32),
                   jax.ShapeDtypeStruct((bk, n_mem, width), BF16)],
        input_output_aliases={2: 0} if aliased else {},
        compiler_params=_params(1),
        name="memkv",
    )(mem, w_xkv_b, *aliased)


def _softmax_rows(s):
    m = jnp.max(s, axis=-1, keepdims=True)
    p = jnp.exp(s - m)
    return p * (1.0 / jnp.sum(p, axis=-1, keepdims=True))


XATTN_SUBTILE = 256


def _xattn_kernel(x_ref, mkv_ref, wq_ref, wo_ref, g_ref, b_ref, out_ref, *, alpha):
    tm, d = x_ref.shape[1:]
    hd = d // MEM_HEADS
    sub = min(tm, XATTN_SUBTILE)
    for r0 in range(0, tm, sub):
        x = x_ref[0, r0:r0 + sub, :]
        qx = jnp.dot(x.astype(BF16), wq_ref[...], preferred_element_type=F32)
        heads = []
        for h in range(MEM_HEADS):
            qh = qx[:, h * hd:(h + 1) * hd].astype(BF16)
            kh = mkv_ref[0, :, h * hd:(h + 1) * hd].astype(BF16)
            vh = mkv_ref[0, :, d + h * hd:d + (h + 1) * hd].astype(BF16)
            s = lax.dot_general(qh, kh, (((1,), (1,)), ((), ())), preferred_element_type=F32)
            m = jnp.max(s, axis=-1, keepdims=True)
            p = jnp.exp(s - m)
            inv = 1.0 / jnp.sum(p, axis=-1, keepdims=True)
            heads.append((jnp.dot(p.astype(BF16), vh, preferred_element_type=F32) * inv).astype(BF16))
        y = alpha * x + jnp.dot(jnp.concatenate(heads, axis=1), wo_ref[...], preferred_element_type=F32)
        out_ref[0, r0:r0 + sub, :] = _layer_norm(y, g_ref[...], b_ref[...])


def _xattn(x, mkv, wq_b, wo_b, layer, g, b, tm, alpha):
    bk, tk, d = x.shape
    n_mem = mkv.shape[1]
    row = pl.BlockSpec((1, tm, d), lambda i, j: (i, j, 0))
    return pl.pallas_call(
        functools.partial(_xattn_kernel, alpha=alpha),
        grid=(bk, tk // tm),
        in_specs=[row, pl.BlockSpec((1, n_mem, 2 * d), lambda i, j: (i, 0, 0)),
                  _layer_spec(wq_b, layer), _layer_spec(wo_b, layer), _layer_spec(g, layer), _layer_spec(b, layer)],
        out_specs=row,
        out_shape=jax.ShapeDtypeStruct((bk, tk, d), F32),
        compiler_params=_params(2),
        name="xattn",
    )(x, mkv, wq_b, wo_b, g, b)


def _sample_xattn_kernel(x_ref, mkv_ref, wq_ref, wo_ref, g_ref, b_ref, out_ref, qx_ref, ox_ref,
                         *, dec_seq, seqs_per_step, alpha):
    step = pl.program_id(0)
    d = x_ref.shape[1]
    hd = d // MEM_HEADS
    seqs_per_slab = SUBLANES // dec_seq

    @pl.when(step == 0)
    def _():
        qx_ref[...] = jnp.dot(x_ref[...].astype(BF16), wq_ref[...], preferred_element_type=F32)

    n_chunks = hd // LANES
    rows_per_mem = 2 * n_chunks * MEM_HEADS
    n_mem = mkv_ref.shape[2] // rows_per_mem

    def head_matrix(b_local, kv, h):
        chunks = [mkv_ref[0, b_local, pl.ds((kv * n_chunks + c) * MEM_HEADS + h, n_mem, stride=rows_per_mem), :]
                  for c in range(n_chunks)]
        return jnp.concatenate(chunks, axis=1).astype(BF16)

    slab_row = lax.broadcasted_iota(jnp.int32, (SUBLANES, 1), 0)
    for j in range(seqs_per_step // seqs_per_slab):
        slab = pl.ds(pl.multiple_of((step * (seqs_per_step // seqs_per_slab) + j) * SUBLANES, SUBLANES), SUBLANES)
        for h in range(MEM_HEADS):
            q8 = qx_ref[slab, h * hd:(h + 1) * hd]
            qh = jnp.concatenate([q8, jnp.zeros_like(q8)], axis=0).astype(BF16)
            o8 = jnp.zeros((SUBLANES, hd), F32)
            for e in range(seqs_per_slab):
                b_local = j * seqs_per_slab + e
                s_t = lax.dot_general(head_matrix(b_local, 0, h), qh, (((1,), (1,)), ((), ())),
                                      preferred_element_type=F32)
                p_t = jnp.exp(s_t - jnp.max(s_t, axis=0, keepdims=True))
                p_t = (p_t * (1.0 / jnp.sum(p_t, axis=0, keepdims=True))).astype(BF16)
                oh = lax.dot_general(p_t, head_matrix(b_local, 1, h), (((0,), (0,)), ((), ())),
                                     preferred_element_type=F32)
                o8 = jnp.where(slab_row // dec_seq == e, oh[:SUBLANES], o8)
            ox_ref[slab, h * hd:(h + 1) * hd] = o8

    @pl.when(step == pl.num_programs(0) - 1)
    def _():
        y = alpha * x_ref[...] + jnp.dot(ox_ref[...].astype(BF16), wo_ref[...], preferred_element_type=F32)
        out_ref[...] = _layer_norm(y, g_ref[...], b_ref[...])


def _sample_xattn(x, mkv_flat, layer, wq_b, wo_b, g, b, dec_seq, seqs_per_step, alpha):
    rows, d = x.shape
    _, n_seq, flat_rows, _ = mkv_flat.shape
    return pl.pallas_call(
        functools.partial(_sample_xattn_kernel, dec_seq=dec_seq, seqs_per_step=seqs_per_step, alpha=alpha),
        grid=(n_seq // seqs_per_step,),
        in_specs=[_const_spec(x.shape),
                  pl.BlockSpec((1, seqs_per_step, flat_rows, LANES), lambda i: (layer, i, 0, 0)),
                  _layer_spec(wq_b, layer), _layer_spec(wo_b, layer), _layer_spec(g, layer), _layer_spec(b, layer)],
        out_specs=pl.BlockSpec((rows, d), lambda i: (0, 0)),
        out_shape=jax.ShapeDtypeStruct((rows, d), F32),
        scratch_shapes=[pltpu.VMEM((rows, d), F32), pltpu.VMEM((rows, d), F32)],
        compiler_params=_params(1),
        name="sample_xattn",
    )(x, mkv_flat, wq_b, wo_b, g, b)


MLP_SUBTILE = 512


def _mlp_kernel(x_ref, wu_ref, wd_ref, g_ref, b_ref, out_ref, *, ff_chunk, alpha):
    tm = x_ref.shape[0]
    sub = min(tm, MLP_SUBTILE)
    for r0 in range(0, tm, sub):
        x = x_ref[r0:r0 + sub, :]
        xb = x.astype(BF16)
        y = alpha * x
        for c in range(wu_ref.shape[1] // ff_chunk):
            h = jnp.dot(xb, wu_ref[:, c * ff_chunk:(c + 1) * ff_chunk], preferred_element_type=F32)
            h = jnp.square(jnp.maximum(h, 0.0)).astype(BF16)
            y = y + jnp.dot(h, wd_ref[c * ff_chunk:(c + 1) * ff_chunk, :], preferred_element_type=F32)
        out_ref[r0:r0 + sub, :] = _layer_norm(y, g_ref[...], b_ref[...])


def _mlp(x, wu_b, wd_b, layer, g, b, tm, alpha, ff_chunk=1024):
    rows, d = x.shape
    row = pl.BlockSpec((tm, d), lambda i: (i, 0))
    return pl.pallas_call(
        functools.partial(_mlp_kernel, ff_chunk=ff_chunk, alpha=alpha),
        grid=(rows // tm,),
        in_specs=[row, _layer_spec(wu_b, layer), _layer_spec(wd_b, layer), _layer_spec(g, layer), _layer_spec(b, layer)],
        out_specs=row,
        out_shape=jax.ShapeDtypeStruct((rows, d), F32),
        compiler_params=_params(1),
        name="mlp",
    )(x, wu_b, wd_b, g, b)


def kernel(x_prompt, x_sample, cache_kv_w128, cache_kv_w512, cache_kv_w2048, cache_mem_kv, mem_prompt,
           w_in, sgu_ln_g, sgu_ln_b, w_spatial, b_spatial, w_mix_out, ln1_g, ln1_b,
           w_xq, w_xkv, w_xo, ln2_g, ln2_b, w_up, w_down, ln3_g, ln3_b):
    depth = w_in.shape[0]
    bp, tp, d = x_prompt.shape
    bs, ts, _ = x_sample.shape
    past_len = 8192
    alpha = float((2 * depth) ** 0.25)
    att_width = len(DILATIONS) * GROUP_WIDTH
    sgu_width = sgu_ln_g.shape[1]
    sgu_groups = w_spatial.shape[1]
    n_mem = mem_prompt.shape[1]
    rows_s = bs * ts
    assert tp % (DILATIONS[-1] * WINDOW_STEPS) == 0 and SUBLANES % ts == 0 and rows_s % SGU_CHUNK == 0
    assert d // MEM_HEADS == GROUP_WIDTH and sgu_width == GROUP_WIDTH

    col_scale = jnp.concatenate([jnp.full((att_width,), HEAD_DIM ** -0.5 * math.log2(math.e), F32),
                                 jnp.ones((w_in.shape[2] - att_width,), F32)])
    w_in_b = (w_in * col_scale).astype(BF16)
    w_xq_b = (w_xq * (d // MEM_HEADS) ** -0.5).astype(BF16)
    w_mix_b, w_xkv_b, w_xo_b = w_mix_out.astype(BF16), w_xkv.astype(BF16), w_xo.astype(BF16)
    w_up_b, w_down_b = w_up.astype(BF16), w_down.astype(BF16)
    vec = lambda p: p[:, None, :]
    sgu_g, sgu_b, g1, b1, g2, b2, g3, b3 = map(vec, (sgu_ln_g, sgu_ln_b, ln1_g, ln1_b, ln2_g, ln2_b, ln3_g, ln3_b))

    tables_p = _rope_tables(jnp.arange(tp, dtype=jnp.int32))
    tables_s = _rope_tables(past_len + (jnp.arange(rows_s, dtype=jnp.int32) % ts))

    eye = jnp.eye(rows_s // ts, dtype=F32)
    tri = jnp.tril(jnp.ones((ts, ts), F32))

    hp = x_prompt
    hs = x_sample.reshape(1, rows_s, d)
    out_rows_s = [[] for _ in DILATIONS]
    out_gv = []
    window_keeps = tuple(min(dil * WINDOW_STEPS, tp) for dil in DILATIONS)
    windows = mem_out = None
    caches_t = [jnp.transpose(c, (0, 1, 3, 4, 5, 2)) for c in (cache_kv_w128, cache_kv_w512, cache_kv_w2048)]
    mem_chunks = d // MEM_HEADS // LANES
    mem_flat = (cache_mem_kv.reshape(depth, bs, n_mem, 2, MEM_HEADS, mem_chunks, LANES)
                .transpose(0, 1, 2, 3, 5, 4, 6).reshape(depth, bs, n_mem * 2 * mem_chunks * MEM_HEADS, LANES))
    for l in range(depth):
        mem_out, mkv = _memkv(mem_prompt, w_xkv_b, l, mem_out)
        q, kv0, kv1, kv2, u, gv, *windows = _inproj(hp, w_in_b, l, tables_p, sgu_g, sgu_b, tm=512, gate_dtype=BF16,
                                                      window_keeps=window_keeps, window_bufs=windows)
        atts = _prompt_attention(q, [kv0, kv1, kv2])
        b_tile = jnp.repeat(b_spatial[l][:, :SGU_CHUNK].T, sgu_width // sgu_groups, axis=1)
        hp = _mix(atts, u, gv, hp, w_spatial[l], b_tile, w_mix_b, l, g1, b1, tm=512, alpha=alpha)
        hp = _xattn(hp, mkv, w_xq_b, w_xo_b, l, g2, b2, tm=512, alpha=alpha)
        hp = _mlp(hp.reshape(bp * tp, d), w_up_b, w_down_b, l, g3, b3, tm=1024, alpha=alpha).reshape(bp, tp, d)

        q, kv0, kv1, kv2, u, gv = _inproj(hs, w_in_b, l, tables_s, sgu_g, sgu_b, tm=rows_s, gate_dtype=F32)
        out_gv.append(gv.reshape(bs, ts, sgu_width))
        for g, kv in enumerate((kv0, kv1, kv2)):
            out_rows_s[g].append(kv.reshape(bs, ts, 2, HEADS_PER_GROUP, HEAD_DIM))
        kvn = jnp.concatenate([kv0[0], kv1[0], kv2[0]], axis=1)
        att = _sample_attention(q[0], kvn, caches_t, l, dec_seq=ts)
        w_sp_s = jnp.einsum("ab,gts->gatbs", eye, w_spatial[l][:, :ts, :ts] * tri).reshape(sgu_groups, rows_s, rows_s)
        b_tile_s = jnp.repeat(jnp.tile(b_spatial[l][:, :ts].T, (rows_s // ts, 1)), sgu_width // sgu_groups, axis=1)
        hs = _mix([att[None]], u, gv, hs, w_sp_s, b_tile_s, w_mix_b, l, g1, b1, tm=rows_s, alpha=alpha)
        hs2 = _sample_xattn(hs[0], mem_flat, l, w_xq_b, w_xo_b, g2, b2, dec_seq=ts, seqs_per_step=4, alpha=alpha)
        hs = _mlp(hs2, w_up_b, w_down_b, l, g3, b3, tm=rows_s, alpha=alpha)[None]

    rows_p = [jnp.transpose(w, (0, 1, 5, 2, 3, 4)) for w in windows]
    mem_p = (mem_out.reshape(depth, bp, n_mem, 2, mem_chunks, MEM_HEADS, LANES)
             .transpose(0, 1, 2, 3, 5, 4, 6).reshape(depth, bp, n_mem, 2, MEM_HEADS, d // MEM_HEADS))
    stack = lambda xs: jnp.stack(xs)
    return (hp, hs.reshape(bs, ts, d), rows_p[0], rows_p[1], rows_p[2], mem_p,
            stack(out_rows_s[0]), stack(out_rows_s[1]), stack(out_rows_s[2]), stack(out_gv))
```

```python
import functools
import math

import jax
import jax.numpy as jnp
from jax import lax
from jax.experimental import pallas as pl
from jax.experimental.pallas import tpu as pltpu

F32 = jnp.float32
BF16 = jnp.bfloat16

HEAD_DIM = 64
HEADS_PER_GROUP = 4
GROUP_WIDTH = HEAD_DIM * HEADS_PER_GROUP
DILATIONS = (1, 4, 16)
WINDOW_STEPS = 128
ROT_DIM = 16
ROPE_THETA = 500000.0
SGU_CHUNK = 128
MEM_HEADS = 4
LN_EPS = 1e-5
NEG = -1e30
LANES = 128
SUBLANES = 8
VMEM_LIMIT = 56 * 1024 * 1024


def _params(n_grid_dims):
    return pltpu.CompilerParams(
        dimension_semantics=("arbitrary",) * n_grid_dims,
        vmem_limit_bytes=VMEM_LIMIT)


def _const_spec(shape):
    nd = len(shape)
    return pl.BlockSpec(shape, lambda *_: (0,) * nd, pipeline_mode=pl.Buffered(1))


def _layer_spec(stacked, layer):
    rest = stacked.shape[1:]
    return pl.BlockSpec((None,) + rest, lambda *_: (layer,) + (0,) * len(rest), pipeline_mode=pl.Buffered(1))


def _layer_norm(y, g, b):
    mu = jnp.mean(y, axis=-1, keepdims=True)
    yc = y - mu
    var = jnp.mean(yc * yc, axis=-1, keepdims=True)
    return yc * lax.rsqrt(var + LN_EPS) * g + b


def _gelu_tanh(x):
    return 0.5 * x * (1.0 + jnp.tanh(0.7978845608028654 * (x + 0.044715 * (x * x * x))))


def _head_half_mask(width, half):
    lane = lax.broadcasted_iota(jnp.int32, (1, width), 1)
    return (lane // HEAD_DIM) % 2 == half


def _inproj_kernel(x_ref, w_ref, cos_ref, sin_lo_ref, sin_hi_ref, g_ref, b_ref,
                   q_ref, kv0_ref, kv1_ref, kv2_ref, u_ref, gv_ref, *, att_width, sgu_width):
    xb = x_ref[0].astype(BF16)
    cos = cos_ref[...]
    sin_lo = sin_lo_ref[...]
    sin_hi = sin_hi_ref[...]

    def proj(c0, width):
        return jnp.dot(xb, w_ref[:, c0:c0 + width], preferred_element_type=F32)

    def rope(t):
        return t * cos + pltpu.roll(t, LANES - ROT_DIM // 2, 1) * sin_lo + pltpu.roll(t, ROT_DIM // 2, 1) * sin_hi

    def rope_group(t):
        return jnp.concatenate([rope(t[:, s:s + LANES]) for s in range(0, GROUP_WIDTH, LANES)], axis=1)

    u_ref[0] = _gelu_tanh(proj(3 * att_width, sgu_width)).astype(u_ref.dtype)
    gate = _gelu_tanh(proj(3 * att_width + sgu_width, sgu_width))
    gv_ref[0] = _layer_norm(gate, g_ref[...], b_ref[...]).astype(gv_ref.dtype)
    kv_refs = (kv0_ref, kv1_ref, kv2_ref)
    for grp, kv_ref in enumerate(kv_refs):
        c = grp * GROUP_WIDTH
        q_ref[0, :, c:c + GROUP_WIDTH] = rope_group(proj(c, GROUP_WIDTH))
        kv_ref[0, :, 0:GROUP_WIDTH] = rope_group(proj(att_width + c, GROUP_WIDTH))
    for grp, kv_ref in enumerate(kv_refs):
        kv_ref[0, :, GROUP_WIDTH:2 * GROUP_WIDTH] = proj(2 * att_width + grp * GROUP_WIDTH, GROUP_WIDTH)


def _inproj(x, w_in_b, layer, tables, g, b, tm, gate_dtype):
    bk, tk, d = x.shape
    att_width = len(DILATIONS) * GROUP_WIDTH
    sgu_width = (w_in_b.shape[2] - 3 * att_width) // 2
    cos, sin_lo, sin_hi = tables
    row = lambda w: pl.BlockSpec((1, tm, w), lambda i, j: (i, j, 0))
    tab = pl.BlockSpec((tm, LANES), lambda i, j: (j, 0))
    out_shape = ([jax.ShapeDtypeStruct((bk, tk, w), F32) for w in (att_width,) + (2 * GROUP_WIDTH,) * len(DILATIONS)]
                 + [jax.ShapeDtypeStruct((bk, tk, sgu_width), gate_dtype)] * 2)
    return pl.pallas_call(
        functools.partial(_inproj_kernel, att_width=att_width, sgu_width=sgu_width),
        grid=(bk, tk // tm),
        in_specs=[row(d), _layer_spec(w_in_b, layer), tab, tab, tab, _layer_spec(g, layer), _layer_spec(b, layer)],
        out_specs=[row(s.shape[2]) for s in out_shape],
        out_shape=out_shape,
        compiler_params=_params(2),
        name="inproj",
    )(x, w_in_b, cos, sin_lo, sin_hi, g, b)


def _rope_tables(pos):
    half = ROT_DIM // 2
    inv = ROPE_THETA ** (-jnp.arange(half, dtype=F32) / half)
    ang = pos.astype(F32)[:, None] * inv[None, :]
    cos, sin = jnp.cos(ang), jnp.sin(ang)
    zeros = jnp.zeros((pos.shape[0], HEAD_DIM - ROT_DIM), F32)
    zero_half = jnp.zeros_like(sin)
    cos_head = jnp.concatenate([cos, cos, zeros + 1.0], axis=1)
    lo_head = jnp.concatenate([-sin, zero_half, zeros], axis=1)
    hi_head = jnp.concatenate([zero_half, sin, zeros], axis=1)
    two = lambda t: jnp.concatenate([t, t], axis=1)
    return two(cos_head), two(lo_head), two(hi_head)


ATTN_MIX_ROWS = 256
ATTN_TILES_PER_BATCH = 32


def _attn_kernel(*refs, seq):
    n_groups = len(DILATIONS)
    q_refs, k_refs, v_refs = refs[0:n_groups], refs[n_groups:2 * n_groups], refs[2 * n_groups:3 * n_groups]
    att_refs = refs[3 * n_groups:4 * n_groups]
    o_s, m_s, den_s = refs[4 * n_groups:]
    blk = WINDOW_STEPS
    x_idx = lax.broadcasted_iota(jnp.int32, (blk, 2 * blk), 0)
    k_idx = lax.broadcasted_iota(jnp.int32, (blk, 2 * blk), 1)
    band_mask = (k_idx >= x_idx) & (k_idx <= x_idx + blk)
    causal_mask = (lax.broadcasted_iota(jnp.int32, (blk, blk), 1)
                   <= lax.broadcasted_iota(jnp.int32, (blk, blk), 0))
    half_masks = [_head_half_mask(LANES, hh) for hh in range(2)]

    def batch(g, dil, blocks):
        def rows(r, first_block, n):
            start = r + dil * blk * first_block
            return pl.ds(start, n) if dil == 1 else pl.ds(start, n, stride=dil)

        first = blocks[0][1] == 0
        mask = causal_mask if first else band_mask
        scores, values = [], []
        for r, c in blocks:
            k_rows = rows(r, 0, blk) if first else rows(r, c - 1, 2 * blk)
            qp = q_refs[g][0, rows(r, c, blk), :]
            kp = k_refs[g][0, k_rows, :].astype(BF16)
            values.append(v_refs[g][0, k_rows, :].astype(BF16))
            for hh in range(2):
                qh = jnp.where(half_masks[hh], qp, 0.0).astype(BF16)
                s = lax.dot_general(qh, kp, (((1,), (1,)), ((), ())), preferred_element_type=F32)
                scores.append(jnp.where(mask, s, NEG))
        s_all = jnp.concatenate(scores, axis=0)
        m = jnp.max(s_all, axis=-1, keepdims=True)
        p = jnp.exp2(s_all - m)
        den = jnp.sum(p, axis=-1, keepdims=True)
        p = p.astype(BF16)
        for n, (r, c) in enumerate(blocks):
            piece = lambda t, hh: t[(2 * n + hh) * blk:(2 * n + hh + 1) * blk]
            outs = [jnp.dot(piece(p, hh), values[n], preferred_element_type=F32) for hh in range(2)]
            o_s[g, rows(r, c, blk), :] = jnp.where(half_masks[0], outs[0], outs[1])
            m_s[g, rows(r, c, blk), :] = jnp.where(half_masks[0], piece(m, 0), piece(m, 1))
            den_s[g, rows(r, c, blk), :] = jnp.where(half_masks[0], piece(den, 0), piece(den, 1))

    for g, dil in enumerate(DILATIONS):
        n_blocks = seq // dil // blk
        first_blocks = [(r, 0) for r in range(dil)]
        later_blocks = [(r, c) for r in range(dil) for c in range(1, n_blocks)]
        per_first = ATTN_TILES_PER_BATCH // 2
        per_later = ATTN_TILES_PER_BATCH // 4
        for i in range(0, len(first_blocks), per_first):
            batch(g, dil, first_blocks[i:i + per_first])
        for i in range(0, len(later_blocks), per_later):
            batch(g, dil, later_blocks[i:i + per_later])

    for t0 in range(0, seq, ATTN_MIX_ROWS):
        rws = slice(t0, t0 + ATTN_MIX_ROWS)
        tops = [m_s[g, rws, :] for g in range(n_groups)]
        peak = functools.reduce(jnp.maximum, tops)
        scales = [jnp.exp2(t - peak) for t in tops]
        inv = 1.0 / functools.reduce(jnp.add, [den_s[g, rws, :] * scales[g] for g in range(n_groups)])
        for g in range(n_groups):
            att_refs[g][0, rws, :] = (o_s[g, rws, :] * (scales[g] * inv)).astype(att_refs[g].dtype)


def _prompt_attention(q, kvs):
    bk, tk, _ = q.shape
    n_groups = len(DILATIONS)
    pairs = GROUP_WIDTH // LANES
    col = lambda first: pl.BlockSpec((1, tk, LANES), lambda i, j: (i, 0, first + j))
    out = jax.ShapeDtypeStruct((bk, tk, GROUP_WIDTH), BF16)
    return pl.pallas_call(
        functools.partial(_attn_kernel, seq=tk),
        grid=(bk, pairs),
        in_specs=([col(g * pairs) for g in range(n_groups)]
                  + [col(0)] * n_groups
                  + [col(pairs)] * n_groups),
        out_specs=[col(0)] * n_groups,
        out_shape=[out] * n_groups,
        scratch_shapes=[pltpu.VMEM((n_groups, tk, LANES), F32)] * 3,
        compiler_params=_params(2),
        name="attn",
    )(*([q] * n_groups), *kvs, *kvs)


def _sample_attn_kernel(q_ref, kvn_ref, c0_ref, c1_ref, c2_ref, att_ref, *, dec_seq):
    cache_refs = (c0_ref, c1_ref, c2_ref)
    seqs_per_slab = SUBLANES // dec_seq
    n_groups = len(DILATIONS)
    widest = DILATIONS[-1]
    slab = pl.ds(pl.multiple_of(pl.program_id(0) * SUBLANES, SUBLANES), SUBLANES)
    lane = lax.broadcasted_iota(jnp.int32, (1, LANES), 1)
    row8 = lax.broadcasted_iota(jnp.int32, (SUBLANES, LANES), 0)
    sel_rows = 2 * SUBLANES
    sel_row = lax.broadcasted_iota(jnp.int32, (sel_rows, LANES), 0)
    sel_lane = lax.broadcasted_iota(jnp.int32, (sel_rows, LANES), 1)
    query_class = lane % widest

    def one_hot(cond):
        return jnp.where(cond, 1.0, 0.0).astype(BF16)

    def spread(rows8, sel):
        padded = jnp.concatenate([rows8, jnp.zeros_like(rows8)], axis=0).astype(BF16)
        return lax.dot_general(padded, sel, (((0,), (0,)), ((), ())), preferred_element_type=F32)

    def class_fold(x, period, op):
        shift = LANES // 2
        while shift >= period:
            x = op(x, pltpu.roll(x, shift, 1))
            shift //= 2
        return x

    def valid_mask(n_tiles, cache_ok, new_ok):
        rows = SUBLANES * (n_tiles // SUBLANES + 1)
        r = lax.broadcasted_iota(jnp.int32, (rows, LANES), 0)
        p = lax.broadcasted_iota(jnp.int32, (rows, LANES), 1)
        return ((r < n_tiles) & cache_ok(p)) | ((r == n_tiles) & new_ok(p))

    def attend(qm, cache_ref, seq, head, knt, vnt, n_tiles, valid, period):
        groups = []
        for t0 in range(0, n_tiles + 1, SUBLANES):
            acc = jnp.full((SUBLANES, LANES), NEG, F32)
            for t in range(t0, min(t0 + SUBLANES, n_tiles + 1)):
                keys = knt if t == n_tiles else cache_ref[0, seq, 0, head, :, t * LANES:(t + 1) * LANES]
                acc = jnp.where(row8 == t - t0, jnp.sum(qm * keys, axis=0, keepdims=True), acc)
            groups.append(acc)
        scores = jnp.where(valid, groups[0] if len(groups) == 1 else jnp.concatenate(groups, axis=0), NEG)
        top = jnp.broadcast_to(jnp.max(scores, axis=0, keepdims=True), (SUBLANES, LANES))
        if period:
            top = class_fold(top, period, jnp.maximum)
        else:
            top = jnp.broadcast_to(jnp.max(top, axis=1, keepdims=True), (SUBLANES, LANES))
        probs = jnp.exp2(scores - jnp.concatenate([top] * len(groups), axis=0))
        den = jnp.broadcast_to(jnp.sum(probs, axis=0, keepdims=True), (SUBLANES, LANES))
        num = jnp.zeros((HEAD_DIM, LANES), F32)
        for t in range(n_tiles + 1):
            vals = vnt if t == n_tiles else cache_ref[0, seq, 1, head, :, t * LANES:(t + 1) * LANES]
            num = num + vals * probs[t:t + 1, :]
        if period:
            return class_fold(num, period, jnp.add), top, class_fold(den, period, jnp.add)
        return (jnp.broadcast_to(jnp.sum(num, axis=1, keepdims=True), (HEAD_DIM, LANES)), top,
                jnp.broadcast_to(jnp.sum(den, axis=1, keepdims=True), (SUBLANES, LANES)))

    for pair in range(GROUP_WIDTH // LANES):
        mixed = [[None] * seqs_per_slab for _ in range(n_groups)]
        for e in range(seqs_per_slab):
            base = e * dec_seq
            sel_new = one_hot((sel_row == base + sel_lane) & (sel_lane < dec_seq))
            per_group = []
            for g, dil in enumerate(DILATIONS):
                off = pair * LANES
                knt = spread(kvn_ref[slab, 2 * g * GROUP_WIDTH + off:2 * g * GROUP_WIDTH + off + LANES], sel_new)
                vnt = spread(kvn_ref[slab, (2 * g + 1) * GROUP_WIDTH + off:(2 * g + 1) * GROUP_WIDTH + off + LANES], sel_new)
                q8 = q_ref[slab, g * GROUP_WIDTH + off:g * GROUP_WIDTH + off + LANES]
                n_tiles = cache_refs[g].shape[5] // LANES
                if dil == 1:
                    qms = [spread(q8, one_hot(sel_row == base + i)) for i in range(dec_seq)]
                    valids = [valid_mask(n_tiles, lambda p, i=i: p >= i, lambda p, i=i: p <= i) for i in range(dec_seq)]
                else:
                    qms = [spread(q8, one_hot((sel_row == base + sel_lane % dil) & (sel_lane % dil < dec_seq)))]
                    valids = [valid_mask(n_tiles, lambda p, dil=dil: p % dil < dec_seq, lambda p: p < dec_seq)]
                heads = []
                for hh in range(2):
                    hrows = slice(hh * HEAD_DIM, (hh + 1) * HEAD_DIM)
                    num = top = den = None
                    for i, (qm, valid) in enumerate(zip(qms, valids)):
                        n_i, t_i, d_i = attend(qm[hrows], cache_refs[g], e, 2 * pair + hh, knt[hrows], vnt[hrows],
                                               n_tiles, valid, dil if dil > 1 else 0)
                        if num is None:
                            num, top, den = n_i, t_i, d_i
                        else:
                            mine = query_class == i
                            num, top, den = jnp.where(mine, n_i, num), jnp.where(mine, t_i, top), jnp.where(mine, d_i, den)
                    heads.append([num, top, den])
                per_group.append(heads)
            for hh in range(2):
                tops = [per_group[g][hh][1] for g in range(n_groups)]
                peak = functools.reduce(jnp.maximum, tops)
                scales = [jnp.exp2(t - peak) for t in tops]
                total = functools.reduce(jnp.add, [per_group[g][hh][2] * scales[g] for g in range(n_groups)])
                for g in range(n_groups):
                    per_group[g][hh] = per_group[g][hh][0] * (scales[g] / total)[0:1, :]
            for g in range(n_groups):
                mixed[g][e] = jnp.concatenate(per_group[g], axis=0)
        sel_out = one_hot(sel_lane == widest * (sel_row // dec_seq) + sel_row % dec_seq)
        for g in range(n_groups):
            both = mixed[g][0]
            for e in range(1, seqs_per_slab):
                both = jnp.where((lane >= widest * e) & (lane < widest * (e + 1)), mixed[g][e], both)
            hi = both.astype(BF16)
            lo = (both - hi.astype(F32)).astype(BF16)
            dims = (((1,), (1,)), ((), ()))
            rows = (lax.dot_general(sel_out, hi, dims, preferred_element_type=F32)
                    + lax.dot_general(sel_out, lo, dims, preferred_element_type=F32))
            att_ref[slab, g * GROUP_WIDTH + pair * LANES:g * GROUP_WIDTH + (pair + 1) * LANES] = rows[:SUBLANES]


def _sample_attention(q, kvn, caches_t, layer, dec_seq):
    rows, width = q.shape
    seqs_per_slab = SUBLANES // dec_seq
    cache_spec = lambda c: pl.BlockSpec((1, seqs_per_slab) + c.shape[2:], lambda i: (layer, i, 0, 0, 0, 0))
    return pl.pallas_call(
        functools.partial(_sample_attn_kernel, dec_seq=dec_seq),
        grid=(rows // SUBLANES,),
        in_specs=[_const_spec(q.shape), _const_spec(kvn.shape)] + [cache_spec(c) for c in caches_t],
        out_specs=pl.BlockSpec((rows, width), lambda i: (0, 0)),
        out_shape=jax.ShapeDtypeStruct((rows, width), F32),
        compiler_params=_params(1),
        name="sample_attn",
    )(q, kvn, *caches_t)


MIX_SUBTILE = 256


def _mix_kernel(*refs, tm, alpha):
    n_att = len(refs) - 9
    u_ref, gv_ref, x_ref, wsp_ref, bsp_ref, wm_ref, g_ref, b_ref, out_ref = refs[n_att:]
    t_idx = lax.broadcasted_iota(jnp.int32, (SGU_CHUNK, SGU_CHUNK), 0)
    s_idx = lax.broadcasted_iota(jnp.int32, (SGU_CHUNK, SGU_CHUNK), 1)
    w_sp = [jnp.where(s_idx <= t_idx, wsp_ref[k], 0.0).astype(BF16) for k in range(wsp_ref.shape[0])]
    sgu_width = gv_ref.shape[2]

    def gated_chunk(crow):
        slabs = []
        for pair in range(sgu_width // LANES):
            gp = gv_ref[0, crow, pair * LANES:(pair + 1) * LANES]
            mixed = jnp.zeros((SGU_CHUNK, LANES), F32)
            for hh in range(2):
                gm = jnp.where(_head_half_mask(LANES, hh), gp, 0.0).astype(BF16)
                mixed = mixed + jnp.dot(w_sp[2 * pair + hh], gm, preferred_element_type=F32)
            slabs.append(mixed)
        return (u_ref[0, crow, :] * (jnp.concatenate(slabs, axis=1) + bsp_ref[...])).astype(BF16)

    sub = min(tm, MIX_SUBTILE)
    for r0 in range(0, tm, sub):
        rws = slice(r0, r0 + sub)
        sgu = jnp.concatenate([gated_chunk(slice(c, c + SGU_CHUNK)) for c in range(r0, r0 + sub, SGU_CHUNK)], axis=0)
        lhs = jnp.concatenate([a[0, rws, :].astype(BF16) for a in refs[:n_att]] + [sgu], axis=1)
        y = alpha * x_ref[0, rws, :] + jnp.dot(lhs, wm_ref[...], preferred_element_type=F32)
        out_ref[0, rws, :] = _layer_norm(y, g_ref[...], b_ref[...])


def _mix(att_inputs, u, gv, x, w_sp, b_sp_tile, w_mix_b, layer, g, b, tm, alpha):
    bk, tk, d = x.shape
    sgu_width = u.shape[2]
    row = lambda w: pl.BlockSpec((1, tm, w), lambda i, j: (i, j, 0))
    return pl.pallas_call(
        functools.partial(_mix_kernel, tm=tm, alpha=alpha),
        grid=(bk, tk // tm),
        in_specs=[row(a.shape[2]) for a in att_inputs] + [
            row(sgu_width), row(sgu_width), row(d),
            _const_spec(w_sp.shape), _const_spec(b_sp_tile.shape), _layer_spec(w_mix_b, layer),
            _layer_spec(g, layer), _layer_spec(b, layer)],
        out_specs=row(d),
        out_shape=jax.ShapeDtypeStruct((bk, tk, d), F32),
        compiler_params=_params(2),
        name="mix",
    )(*att_inputs, u, gv, x, w_sp, b_sp_tile, w_mix_b, g, b)


def _mem_row_stride(d):
    return 2 * (d // MEM_HEADS // LANES) * MEM_HEADS


def _memkv_kernel(*refs, first_layer):
    m_ref, w_ref = refs[:2]
    flat_ref, out_b_ref = refs[-2:]
    if first_layer is not None:
        for other in range(flat_ref.shape[0]):
            if other != first_layer:
                flat_ref[other] = jnp.zeros(flat_ref.shape[1:], F32)
        flat_ref = flat_ref.at[first_layer]
    n_mem, width = out_b_ref.shape[1:]
    d = width // 2
    hd = d // MEM_HEADS
    n_chunks = hd // LANES
    mkv = jnp.dot(m_ref[0].astype(BF16), w_ref[...], preferred_element_type=F32)
    out_b_ref[0] = mkv.astype(BF16)
    for kv in range(2):
        for c in range(n_chunks):
            for h in range(MEM_HEADS):
                col = kv * d + h * hd + c * LANES
                flat_ref[pl.ds((kv * n_chunks + c) * MEM_HEADS + h, n_mem, stride=_mem_row_stride(d)), :] = (
                    mkv[:, col:col + LANES])


def _memkv(mem, w_xkv_b, layer, flat_buf=None):
    bk, n_mem, d = mem.shape
    depth, _, width = w_xkv_b.shape
    flat_rows = n_mem * _mem_row_stride(d)
    aliased = [] if flat_buf is None else [flat_buf]
    lead, at = (None, layer) if aliased else (depth, 0)
    return pl.pallas_call(
        functools.partial(_memkv_kernel, first_layer=None if aliased else layer),
        grid=(bk,),
        in_specs=[pl.BlockSpec((1, n_mem, d), lambda i: (i, 0, 0)), _layer_spec(w_xkv_b, layer)]
        + [pl.BlockSpec(memory_space=pl.ANY)] * len(aliased),
        out_specs=[pl.BlockSpec((lead, None, flat_rows, LANES), lambda i: (at, i, 0, 0)),
                   pl.BlockSpec((1, n_mem, width), lambda i: (i, 0, 0))],
        out_shape=[jax.ShapeDtypeStruct((depth, bk, flat_rows, LANES), F32),
                   jax.ShapeDtypeStruct((bk, n_mem, width), BF16)],
        input_output_aliases={2: 0} if aliased else {},
        compiler_params=_params(1),
        name="memkv",
    )(mem, w_xkv_b, *aliased)


def _softmax_rows(s):
    m = jnp.max(s, axis=-1, keepdims=True)
    p = jnp.exp(s - m)
    return p * (1.0 / jnp.sum(p, axis=-1, keepdims=True))


XATTN_SUBTILE = 256


def _xattn_kernel(x_ref, mkv_ref, wq_ref, wo_ref, g_ref, b_ref, out_ref, *, alpha):
    tm, d = x_ref.shape[1:]
    hd = d // MEM_HEADS
    sub = min(tm, XATTN_SUBTILE)
    for r0 in range(0, tm, sub):
        x = x_ref[0, r0:r0 + sub, :]
        qx = jnp.dot(x.astype(BF16), wq_ref[...], preferred_element_type=F32)
        heads = []
        for h in range(MEM_HEADS):
            qh = qx[:, h * hd:(h + 1) * hd].astype(BF16)
            kh = mkv_ref[0, :, h * hd:(h + 1) * hd].astype(BF16)
            vh = mkv_ref[0, :, d + h * hd:d + (h + 1) * hd].astype(BF16)
            s = lax.dot_general(qh, kh, (((1,), (1,)), ((), ())), preferred_element_type=F32)
            m = jnp.max(s, axis=-1, keepdims=True)
            p = jnp.exp(s - m)
            inv = 1.0 / jnp.sum(p, axis=-1, keepdims=True)
            heads.append((jnp.dot(p.astype(BF16), vh, preferred_element_type=F32) * inv).astype(BF16))
        y = alpha * x + jnp.dot(jnp.concatenate(heads, axis=1), wo_ref[...], preferred_element_type=F32)
        out_ref[0, r0:r0 + sub, :] = _layer_norm(y, g_ref[...], b_ref[...])


def _xattn(x, mkv, wq_b, wo_b, layer, g, b, tm, alpha):
    bk, tk, d = x.shape
    n_mem = mkv.shape[1]
    row = pl.BlockSpec((1, tm, d), lambda i, j: (i, j, 0))
    return pl.pallas_call(
        functools.partial(_xattn_kernel, alpha=alpha),
        grid=(bk, tk // tm),
        in_specs=[row, pl.BlockSpec((1, n_mem, 2 * d), lambda i, j: (i, 0, 0)),
                  _layer_spec(wq_b, layer), _layer_spec(wo_b, layer), _layer_spec(g, layer), _layer_spec(b, layer)],
        out_specs=row,
        out_shape=jax.ShapeDtypeStruct((bk, tk, d), F32),
        compiler_params=_params(2),
        name="xattn",
    )(x, mkv, wq_b, wo_b, g, b)


def _sample_xattn_kernel(x_ref, mkv_ref, wq_ref, wo_ref, g_ref, b_ref, out_ref, qx_ref, ox_ref,
                         *, dec_seq, seqs_per_step, alpha):
    step = pl.program_id(0)
    d = x_ref.shape[1]
    hd = d // MEM_HEADS
    seqs_per_slab = SUBLANES // dec_seq

    @pl.when(step == 0)
    def _():
        qx_ref[...] = jnp.dot(x_ref[...].astype(BF16), wq_ref[...], preferred_element_type=F32)

    n_chunks = hd // LANES
    rows_per_mem = 2 * n_chunks * MEM_HEADS
    n_mem = mkv_ref.shape[2] // rows_per_mem

    def head_matrix(b_local, kv, h):
        chunks = [mkv_ref[0, b_local, pl.ds((kv * n_chunks + c) * MEM_HEADS + h, n_mem, stride=rows_per_mem), :]
                  for c in range(n_chunks)]
        return jnp.concatenate(chunks, axis=1).astype(BF16)

    slab_row = lax.broadcasted_iota(jnp.int32, (SUBLANES, 1), 0)
    for j in range(seqs_per_step // seqs_per_slab):
        slab = pl.ds(pl.multiple_of((step * (seqs_per_step // seqs_per_slab) + j) * SUBLANES, SUBLANES), SUBLANES)
        for h in range(MEM_HEADS):
            q8 = qx_ref[slab, h * hd:(h + 1) * hd]
            qh = jnp.concatenate([q8, jnp.zeros_like(q8)], axis=0).astype(BF16)
            o8 = jnp.zeros((SUBLANES, hd), F32)
            for e in range(seqs_per_slab):
                b_local = j * seqs_per_slab + e
                s_t = lax.dot_general(head_matrix(b_local, 0, h), qh, (((1,), (1,)), ((), ())),
                                      preferred_element_type=F32)
                p_t = jnp.exp(s_t - jnp.max(s_t, axis=0, keepdims=True))
                p_t = (p_t * (1.0 / jnp.sum(p_t, axis=0, keepdims=True))).astype(BF16)
                oh = lax.dot_general(p_t, head_matrix(b_local, 1, h), (((0,), (0,)), ((), ())),
                                     preferred_element_type=F32)
                o8 = jnp.where(slab_row // dec_seq == e, oh[:SUBLANES], o8)
            ox_ref[slab, h * hd:(h + 1) * hd] = o8

    @pl.when(step == pl.num_programs(0) - 1)
    def _():
        y = alpha * x_ref[...] + jnp.dot(ox_ref[...].astype(BF16), wo_ref[...], preferred_element_type=F32)
        out_ref[...] = _layer_norm(y, g_ref[...], b_ref[...])


def _sample_xattn(x, mkv_flat, layer, wq_b, wo_b, g, b, dec_seq, seqs_per_step, alpha):
    rows, d = x.shape
    _, n_seq, flat_rows, _ = mkv_flat.shape
    return pl.pallas_call(
        functools.partial(_sample_xattn_kernel, dec_seq=dec_seq, seqs_per_step=seqs_per_step, alpha=alpha),
        grid=(n_seq // seqs_per_step,),
        in_specs=[_const_spec(x.shape),
                  pl.BlockSpec((1, seqs_per_step, flat_rows, LANES), lambda i: (layer, i, 0, 0)),
                  _layer_spec(wq_b, layer), _layer_spec(wo_b, layer), _layer_spec(g, layer), _layer_spec(b, layer)],
        out_specs=pl.BlockSpec((rows, d), lambda i: (0, 0)),
        out_shape=jax.ShapeDtypeStruct((rows, d), F32),
        scratch_shapes=[pltpu.VMEM((rows, d), F32), pltpu.VMEM((rows, d), F32)],
        compiler_params=_params(1),
        name="sample_xattn",
    )(x, mkv_flat, wq_b, wo_b, g, b)


MLP_SUBTILE = 512


def _mlp_kernel(*refs, ff_chunk, alpha, windows, first_layer):
    n_win = len(windows)
    x_ref, wu_ref, wd_ref, g_ref, b_ref = refs[:5]
    kv_refs = refs[5:5 + n_win]
    out_ref = refs[-1 - n_win]
    window_refs = list(refs[len(refs) - n_win:])

    def write_window(grp):
        win_ref = window_refs[grp] if first_layer is None else window_refs[grp].at[first_layer]
        for part in range(2):
            t = kv_refs[grp][0, :, part * GROUP_WIDTH:(part + 1) * GROUP_WIDTH].T
            for h in range(HEADS_PER_GROUP):
                win_ref[part, h] = t[h * HEAD_DIM:(h + 1) * HEAD_DIM, :]
        if first_layer is not None:
            for other in range(window_refs[grp].shape[0]):
                if other != first_layer:
                    window_refs[grp][other] = jnp.zeros(window_refs[grp].shape[1:], F32)

    for grp, every_tile in enumerate(windows):
        if not every_tile:
            pl.when(pl.program_id(1) == pl.num_programs(1) - 1)(functools.partial(write_window, grp))
    for grp, every_tile in enumerate(windows):
        if every_tile:
            write_window(grp)

    tm = x_ref.shape[1]
    sub = min(tm, MLP_SUBTILE)
    for r0 in range(0, tm, sub):
        x = x_ref[0, r0:r0 + sub, :]
        xb = x.astype(BF16)
        y = alpha * x
        for c in range(wu_ref.shape[1] // ff_chunk):
            h = jnp.dot(xb, wu_ref[:, c * ff_chunk:(c + 1) * ff_chunk], preferred_element_type=F32)
            h = jnp.square(jnp.maximum(h, 0.0)).astype(BF16)
            y = y + jnp.dot(h, wd_ref[c * ff_chunk:(c + 1) * ff_chunk, :], preferred_element_type=F32)
        out_ref[0, r0:r0 + sub, :] = _layer_norm(y, g_ref[...], b_ref[...])


def _mlp(x, wu_b, wd_b, layer, g, b, tm, alpha, ff_chunk=1024, kvs=(), window_keeps=(), window_bufs=None):
    bk, tk, d = x.shape
    depth = wu_b.shape[0]
    row = pl.BlockSpec((1, tm, d), lambda i, j: (i, j, 0))
    aliased = list(window_bufs or ())
    owns_all_layers = bool(window_keeps) and not aliased
    lead, at = (depth, 0) if owns_all_layers else (None, layer)
    kv_specs, out_shape, out_specs, windows = [], [jax.ShapeDtypeStruct((bk, tk, d), F32)], [row], []
    for kv, keep in zip(kvs, window_keeps):
        every_tile = keep == tk
        assert every_tile or tk % keep == 0
        windows.append(every_tile)
        n = tm if every_tile else keep
        kv_specs.append(pl.BlockSpec((1, n, kv.shape[2]),
                                     (lambda i, j: (i, j, 0)) if every_tile else (lambda i, j, last=tk // keep - 1: (i, last, 0))))
        out_shape.append(jax.ShapeDtypeStruct((depth, bk, 2, HEADS_PER_GROUP, HEAD_DIM, keep), F32))
        out_specs.append(pl.BlockSpec((lead, None, 2, HEADS_PER_GROUP, HEAD_DIM, n),
                                      (lambda i, j: (at, i, 0, 0, 0, j)) if every_tile else (lambda i, j: (at, i, 0, 0, 0, 0))))
    n_in = 5 + len(kv_specs)
    return pl.pallas_call(
        functools.partial(_mlp_kernel, ff_chunk=ff_chunk, alpha=alpha, windows=tuple(windows),
                          first_layer=layer if owns_all_layers else None),
        grid=(bk, tk // tm),
        in_specs=[row, _layer_spec(wu_b, layer), _layer_spec(wd_b, layer), _layer_spec(g, layer), _layer_spec(b, layer)]
        + kv_specs + [pl.BlockSpec(memory_space=pl.ANY)] * len(aliased),
        out_specs=out_specs,
        out_shape=out_shape,
        input_output_aliases={n_in + k: 1 + k for k in range(len(aliased))},
        compiler_params=_params(2),
        name="mlp",
    )(x, wu_b, wd_b, g, b, *kvs, *aliased)


def kernel(x_prompt, x_sample, cache_kv_w128, cache_kv_w512, cache_kv_w2048, cache_mem_kv, mem_prompt,
           w_in, sgu_ln_g, sgu_ln_b, w_spatial, b_spatial, w_mix_out, ln1_g, ln1_b,
           w_xq, w_xkv, w_xo, ln2_g, ln2_b, w_up, w_down, ln3_g, ln3_b):
    depth = w_in.shape[0]
    bp, tp, d = x_prompt.shape
    bs, ts, _ = x_sample.shape
    past_len = 8192
    alpha = float((2 * depth) ** 0.25)
    att_width = len(DILATIONS) * GROUP_WIDTH
    sgu_width = sgu_ln_g.shape[1]
    sgu_groups = w_spatial.shape[1]
    n_mem = mem_prompt.shape[1]
    rows_s = bs * ts
    assert tp % (DILATIONS[-1] * WINDOW_STEPS) == 0 and SUBLANES % ts == 0 and rows_s % SGU_CHUNK == 0
    assert d // MEM_HEADS == GROUP_WIDTH and sgu_width == GROUP_WIDTH

    col_scale = jnp.concatenate([jnp.full((att_width,), HEAD_DIM ** -0.5 * math.log2(math.e), F32),
                                 jnp.ones((w_in.shape[2] - att_width,), F32)])
    w_in_b = (w_in * col_scale).astype(BF16)
    w_xq_b = (w_xq * (d // MEM_HEADS) ** -0.5).astype(BF16)
    w_mix_b, w_xkv_b, w_xo_b = w_mix_out.astype(BF16), w_xkv.astype(BF16), w_xo.astype(BF16)
    w_up_b, w_down_b = w_up.astype(BF16), w_down.astype(BF16)
    vec = lambda p: p[:, None, :]
    sgu_g, sgu_b, g1, b1, g2, b2, g3, b3 = map(vec, (sgu_ln_g, sgu_ln_b, ln1_g, ln1_b, ln2_g, ln2_b, ln3_g, ln3_b))

    tables_p = _rope_tables(jnp.arange(tp, dtype=jnp.int32))
    tables_s = _rope_tables(past_len + (jnp.arange(rows_s, dtype=jnp.int32) % ts))

    eye = jnp.eye(rows_s // ts, dtype=F32)
    tri = jnp.tril(jnp.ones((ts, ts), F32))

    hp = x_prompt
    hs = x_sample.reshape(1, rows_s, d)
    out_rows_s = [[] for _ in DILATIONS]
    out_gv = []
    window_keeps = tuple(min(dil * WINDOW_STEPS, tp) for dil in DILATIONS)
    windows = mem_out = None
    caches_t = [jnp.transpose(c, (0, 1, 3, 4, 5, 2)) for c in (cache_kv_w128, cache_kv_w512, cache_kv_w2048)]
    mem_chunks = d // MEM_HEADS // LANES
    mem_flat = (cache_mem_kv.reshape(depth, bs, n_mem, 2, MEM_HEADS, mem_chunks, LANES)
                .transpose(0, 1, 2, 3, 5, 4, 6).reshape(depth, bs, n_mem * 2 * mem_chunks * MEM_HEADS, LANES))
    for l in range(depth):
        mem_out, mkv = _memkv(mem_prompt, w_xkv_b, l, mem_out)
        q, kv0, kv1, kv2, u, gv = _inproj(hp, w_in_b, l, tables_p, sgu_g, sgu_b, tm=512, gate_dtype=BF16)
        atts = _prompt_attention(q, [kv0, kv1, kv2])
        b_tile = jnp.repeat(b_spatial[l][:, :SGU_CHUNK].T, sgu_width // sgu_groups, axis=1)
        hp = _mix(atts, u, gv, hp, w_spatial[l], b_tile, w_mix_b, l, g1, b1, tm=512, alpha=alpha)
        hp = _xattn(hp, mkv, w_xq_b, w_xo_b, l, g2, b2, tm=512, alpha=alpha)
        hp, *windows = _mlp(hp, w_up_b, w_down_b, l, g3, b3, tm=512, alpha=alpha,
                            kvs=(kv0, kv1, kv2), window_keeps=window_keeps, window_bufs=windows)

        q, kv0, kv1, kv2, u, gv = _inproj(hs, w_in_b, l, tables_s, sgu_g, sgu_b, tm=rows_s, gate_dtype=F32)
        out_gv.append(gv.reshape(bs, ts, sgu_width))
        for g, kv in enumerate((kv0, kv1, kv2)):
            out_rows_s[g].append(kv.reshape(bs, ts, 2, HEADS_PER_GROUP, HEAD_DIM))
        kvn = jnp.concatenate([kv0[0], kv1[0], kv2[0]], axis=1)
        att = _sample_attention(q[0], kvn, caches_t, l, dec_seq=ts)
        w_sp_s = jnp.einsum("ab,gts->gatbs", eye, w_spatial[l][:, :ts, :ts] * tri).reshape(sgu_groups, rows_s, rows_s)
        b_tile_s = jnp.repeat(jnp.tile(b_spatial[l][:, :ts].T, (rows_s // ts, 1)), sgu_width // sgu_groups, axis=1)
        hs = _mix([att[None]], u, gv, hs, w_sp_s, b_tile_s, w_mix_b, l, g1, b1, tm=rows_s, alpha=alpha)
        hs2 = _sample_xattn(hs[0], mem_flat, l, w_xq_b, w_xo_b, g2, b2, dec_seq=ts, seqs_per_step=4, alpha=alpha)
        hs, = _mlp(hs2[None], w_up_b, w_down_b, l, g3, b3, tm=rows_s, alpha=alpha)

    rows_p = [jnp.transpose(w, (0, 1, 5, 2, 3, 4)) for w in windows]
    mem_p = (mem_out.reshape(depth, bp, n_mem, 2, mem_chunks, MEM_HEADS, LANES)
             .transpose(0, 1, 2, 3, 5, 4, 6).reshape(depth, bp, n_mem, 2, MEM_HEADS, d // MEM_HEADS))
    stack = lambda xs: jnp.stack(xs)
    return (hp, hs.reshape(bs, ts, d), rows_p[0], rows_p[1], rows_p[2], mem_p,
            stack(out_rows_s[0]), stack(out_rows_s[1]), stack(out_rows_s[2]), stack(out_gv))
```

```python
import functools
import math

import jax
import jax.numpy as jnp
from jax import lax
from jax.experimental import pallas as pl
from jax.experimental.pallas import tpu as pltpu

F32 = jnp.float32
BF16 = jnp.bfloat16

HEAD_DIM = 64
HEADS_PER_GROUP = 4
GROUP_WIDTH = HEAD_DIM * HEADS_PER_GROUP
DILATIONS = (1, 4, 16)
WINDOW_STEPS = 128
ROT_DIM = 16
ROPE_THETA = 500000.0
SGU_CHUNK = 128
MEM_HEADS = 4
LN_EPS = 1e-5
NEG = -1e30
LANES = 128
SUBLANES = 8
VMEM_LIMIT = 56 * 1024 * 1024


def _params(n_grid_dims):
    return pltpu.CompilerParams(
        dimension_semantics=("arbitrary",) * n_grid_dims,
        vmem_limit_bytes=VMEM_LIMIT)


def _const_spec(shape):
    nd = len(shape)
    return pl.BlockSpec(shape, lambda *_: (0,) * nd, pipeline_mode=pl.Buffered(1))


def _layer_spec(stacked, layer):
    rest = stacked.shape[1:]
    return pl.BlockSpec((None,) + rest, lambda *_: (layer,) + (0,) * len(rest), pipeline_mode=pl.Buffered(1))


def _layer_norm(y, g, b):
    mu = jnp.mean(y, axis=-1, keepdims=True)
    yc = y - mu
    var = jnp.mean(yc * yc, axis=-1, keepdims=True)
    return yc * lax.rsqrt(var + LN_EPS) * g + b


def _gelu_tanh(x):
    return 0.5 * x * (1.0 + jnp.tanh(0.7978845608028654 * (x + 0.044715 * (x * x * x))))


def _head_half_mask(width, half):
    lane = lax.broadcasted_iota(jnp.int32, (1, width), 1)
    return (lane // HEAD_DIM) % 2 == half


def _inproj_kernel(x_ref, w_ref, cos_ref, sin_lo_ref, sin_hi_ref, g_ref, b_ref,
                   q_ref, kv0_ref, kv1_ref, kv2_ref, u_ref, gv_ref, *, att_width, sgu_width):
    xb = x_ref[0].astype(BF16)
    cos = cos_ref[...]
    sin_lo = sin_lo_ref[...]
    sin_hi = sin_hi_ref[...]

    def proj(c0, width):
        return jnp.dot(xb, w_ref[:, c0:c0 + width], preferred_element_type=F32)

    def rope(t):
        return t * cos + pltpu.roll(t, LANES - ROT_DIM // 2, 1) * sin_lo + pltpu.roll(t, ROT_DIM // 2, 1) * sin_hi

    def rope_group(t):
        return jnp.concatenate([rope(t[:, s:s + LANES]) for s in range(0, GROUP_WIDTH, LANES)], axis=1)

    u_ref[0] = _gelu_tanh(proj(3 * att_width, sgu_width)).astype(u_ref.dtype)
    gate = _gelu_tanh(proj(3 * att_width + sgu_width, sgu_width))
    gv_ref[0] = _layer_norm(gate, g_ref[...], b_ref[...]).astype(gv_ref.dtype)
    kv_refs = (kv0_ref, kv1_ref, kv2_ref)
    for grp, kv_ref in enumerate(kv_refs):
        c = grp * GROUP_WIDTH
        q_ref[0, :, c:c + GROUP_WIDTH] = rope_group(proj(c, GROUP_WIDTH))
        kv_ref[0, :, 0:GROUP_WIDTH] = rope_group(proj(att_width + c, GROUP_WIDTH))
    for grp, kv_ref in enumerate(kv_refs):
        kv_ref[0, :, GROUP_WIDTH:2 * GROUP_WIDTH] = proj(2 * att_width + grp * GROUP_WIDTH, GROUP_WIDTH)


def _inproj(x, w_in_b, layer, tables, g, b, tm, gate_dtype):
    bk, tk, d = x.shape
    att_width = len(DILATIONS) * GROUP_WIDTH
    sgu_width = (w_in_b.shape[2] - 3 * att_width) // 2
    cos, sin_lo, sin_hi = tables
    row = lambda w: pl.BlockSpec((1, tm, w), lambda i, j: (i, j, 0))
    tab = pl.BlockSpec((tm, LANES), lambda i, j: (j, 0))
    out_shape = ([jax.ShapeDtypeStruct((bk, tk, w), F32) for w in (att_width,) + (2 * GROUP_WIDTH,) * len(DILATIONS)]
                 + [jax.ShapeDtypeStruct((bk, tk, sgu_width), gate_dtype)] * 2)
    return pl.pallas_call(
        functools.partial(_inproj_kernel, att_width=att_width, sgu_width=sgu_width),
        grid=(bk, tk // tm),
        in_specs=[row(d), _layer_spec(w_in_b, layer), tab, tab, tab, _layer_spec(g, layer), _layer_spec(b, layer)],
        out_specs=[row(s.shape[2]) for s in out_shape],
        out_shape=out_shape,
        compiler_params=_params(2),
        name="inproj",
    )(x, w_in_b, cos, sin_lo, sin_hi, g, b)


def _rope_tables(pos):
    half = ROT_DIM // 2
    inv = ROPE_THETA ** (-jnp.arange(half, dtype=F32) / half)
    ang = pos.astype(F32)[:, None] * inv[None, :]
    cos, sin = jnp.cos(ang), jnp.sin(ang)
    zeros = jnp.zeros((pos.shape[0], HEAD_DIM - ROT_DIM), F32)
    zero_half = jnp.zeros_like(sin)
    cos_head = jnp.concatenate([cos, cos, zeros + 1.0], axis=1)
    lo_head = jnp.concatenate([-sin, zero_half, zeros], axis=1)
    hi_head = jnp.concatenate([zero_half, sin, zeros], axis=1)
    two = lambda t: jnp.concatenate([t, t], axis=1)
    return two(cos_head), two(lo_head), two(hi_head)


ATTN_MIX_ROWS = 256
ATTN_TILES_PER_BATCH = 32


def _attn_kernel(*refs, seq):
    n_groups = len(DILATIONS)
    q_refs, k_refs, v_refs = refs[0:n_groups], refs[n_groups:2 * n_groups], refs[2 * n_groups:3 * n_groups]
    att_refs = refs[3 * n_groups:4 * n_groups]
    o_s, m_s, den_s = refs[4 * n_groups:]
    blk = WINDOW_STEPS
    x_idx = lax.broadcasted_iota(jnp.int32, (blk, 2 * blk), 0)
    k_idx = lax.broadcasted_iota(jnp.int32, (blk, 2 * blk), 1)
    band_mask = (k_idx >= x_idx) & (k_idx <= x_idx + blk)
    causal_mask = (lax.broadcasted_iota(jnp.int32, (blk, blk), 1)
                   <= lax.broadcasted_iota(jnp.int32, (blk, blk), 0))
    half_masks = [_head_half_mask(LANES, hh) for hh in range(2)]

    def batch(g, dil, blocks):
        def rows(r, first_block, n):
            start = r + dil * blk * first_block
            return pl.ds(start, n) if dil == 1 else pl.ds(start, n, stride=dil)

        first = blocks[0][1] == 0
        mask = causal_mask if first else band_mask
        scores, values = [], []
        for r, c in blocks:
            k_rows = rows(r, 0, blk) if first else rows(r, c - 1, 2 * blk)
            qp = q_refs[g][0, rows(r, c, blk), :]
            kp = k_refs[g][0, k_rows, :].astype(BF16)
            values.append(v_refs[g][0, k_rows, :].astype(BF16))
            for hh in range(2):
                qh = jnp.where(half_masks[hh], qp, 0.0).astype(BF16)
                s = lax.dot_general(qh, kp, (((1,), (1,)), ((), ())), preferred_element_type=F32)
                scores.append(jnp.where(mask, s, NEG))
        s_all = jnp.concatenate(scores, axis=0)
        m = jnp.max(s_all, axis=-1, keepdims=True)
        p = jnp.exp2(s_all - m)
        den = jnp.sum(p, axis=-1, keepdims=True)
        p = p.astype(BF16)
        for n, (r, c) in enumerate(blocks):
            piece = lambda t, hh: t[(2 * n + hh) * blk:(2 * n + hh + 1) * blk]
            outs = [jnp.dot(piece(p, hh), values[n], preferred_element_type=F32) for hh in range(2)]
            o_s[g, rows(r, c, blk), :] = jnp.where(half_masks[0], outs[0], outs[1])
            m_s[g, rows(r, c, blk), :] = jnp.where(half_masks[0], piece(m, 0), piece(m, 1))
            den_s[g, rows(r, c, blk), :] = jnp.where(half_masks[0], piece(den, 0), piece(den, 1))

    for g, dil in enumerate(DILATIONS):
        n_blocks = seq // dil // blk
        first_blocks = [(r, 0) for r in range(dil)]
        later_blocks = [(r, c) for r in range(dil) for c in range(1, n_blocks)]
        per_first = ATTN_TILES_PER_BATCH // 2
        per_later = ATTN_TILES_PER_BATCH // 4
        for i in range(0, len(first_blocks), per_first):
            batch(g, dil, first_blocks[i:i + per_first])
        for i in range(0, len(later_blocks), per_later):
            batch(g, dil, later_blocks[i:i + per_later])

    for t0 in range(0, seq, ATTN_MIX_ROWS):
        rws = slice(t0, t0 + ATTN_MIX_ROWS)
        tops = [m_s[g, rws, :] for g in range(n_groups)]
        peak = functools.reduce(jnp.maximum, tops)
        scales = [jnp.exp2(t - peak) for t in tops]
        inv = 1.0 / functools.reduce(jnp.add, [den_s[g, rws, :] * scales[g] for g in range(n_groups)])
        for g in range(n_groups):
            att_refs[g][0, rws, :] = (o_s[g, rws, :] * (scales[g] * inv)).astype(att_refs[g].dtype)


def _prompt_attention(q, kvs):
    bk, tk, _ = q.shape
    n_groups = len(DILATIONS)
    pairs = GROUP_WIDTH // LANES
    col = lambda first: pl.BlockSpec((1, tk, LANES), lambda i, j: (i, 0, first + j))
    out = jax.ShapeDtypeStruct((bk, tk, GROUP_WIDTH), BF16)
    return pl.pallas_call(
        functools.partial(_attn_kernel, seq=tk),
        grid=(bk, pairs),
        in_specs=([col(g * pairs) for g in range(n_groups)]
                  + [col(0)] * n_groups
                  + [col(pairs)] * n_groups),
        out_specs=[col(0)] * n_groups,
        out_shape=[out] * n_groups,
        scratch_shapes=[pltpu.VMEM((n_groups, tk, LANES), F32)] * 3,
        compiler_params=_params(2),
        name="attn",
    )(*([q] * n_groups), *kvs, *kvs)


def _sample_attn_kernel(q_ref, kvn_ref, c0_ref, c1_ref, c2_ref, att_ref, *, dec_seq):
    cache_refs = (c0_ref, c1_ref, c2_ref)
    seqs_per_slab = SUBLANES // dec_seq
    n_groups = len(DILATIONS)
    widest = DILATIONS[-1]
    slab = pl.ds(pl.multiple_of(pl.program_id(0) * SUBLANES, SUBLANES), SUBLANES)
    lane = lax.broadcasted_iota(jnp.int32, (1, LANES), 1)
    row8 = lax.broadcasted_iota(jnp.int32, (SUBLANES, LANES), 0)
    sel_rows = 2 * SUBLANES
    sel_row = lax.broadcasted_iota(jnp.int32, (sel_rows, LANES), 0)
    sel_lane = lax.broadcasted_iota(jnp.int32, (sel_rows, LANES), 1)
    query_class = lane % widest

    def one_hot(cond):
        return jnp.where(cond, 1.0, 0.0).astype(BF16)

    def spread(rows8, sel):
        padded = jnp.concatenate([rows8, jnp.zeros_like(rows8)], axis=0).astype(BF16)
        return lax.dot_general(padded, sel, (((0,), (0,)), ((), ())), preferred_element_type=F32)

    def class_fold(x, period, op):
        shift = LANES // 2
        while shift >= period:
            x = op(x, pltpu.roll(x, shift, 1))
            shift //= 2
        return x

    def valid_mask(n_tiles, cache_ok, new_ok):
        rows = SUBLANES * (n_tiles // SUBLANES + 1)
        r = lax.broadcasted_iota(jnp.int32, (rows, LANES), 0)
        p = lax.broadcasted_iota(jnp.int32, (rows, LANES), 1)
        return ((r < n_tiles) & cache_ok(p)) | ((r == n_tiles) & new_ok(p))

    def attend(qm, cache_ref, seq, head, knt, vnt, n_tiles, valid, period):
        groups = []
        for t0 in range(0, n_tiles + 1, SUBLANES):
            acc = jnp.full((SUBLANES, LANES), NEG, F32)
            for t in range(t0, min(t0 + SUBLANES, n_tiles + 1)):
                keys = knt if t == n_tiles else cache_ref[0, seq, 0, head, :, t * LANES:(t + 1) * LANES]
                acc = jnp.where(row8 == t - t0, jnp.sum(qm * keys, axis=0, keepdims=True), acc)
            groups.append(acc)
        scores = jnp.where(valid, groups[0] if len(groups) == 1 else jnp.concatenate(groups, axis=0), NEG)
        top = jnp.broadcast_to(jnp.max(scores, axis=0, keepdims=True), (SUBLANES, LANES))
        if period:
            top = class_fold(top, period, jnp.maximum)
        else:
            top = jnp.broadcast_to(jnp.max(top, axis=1, keepdims=True), (SUBLANES, LANES))
        probs = jnp.exp2(scores - jnp.concatenate([top] * len(groups), axis=0))
        den = jnp.broadcast_to(jnp.sum(probs, axis=0, keepdims=True), (SUBLANES, LANES))
        num = jnp.zeros((HEAD_DIM, LANES), F32)
        for t in range(n_tiles + 1):
            vals = vnt if t == n_tiles else cache_ref[0, seq, 1, head, :, t * LANES:(t + 1) * LANES]
            num = num + vals * probs[t:t + 1, :]
        if period:
            return class_fold(num, period, jnp.add), top, class_fold(den, period, jnp.add)
        return (jnp.broadcast_to(jnp.sum(num, axis=1, keepdims=True), (HEAD_DIM, LANES)), top,
                jnp.broadcast_to(jnp.sum(den, axis=1, keepdims=True), (SUBLANES, LANES)))

    for pair in range(GROUP_WIDTH // LANES):
        mixed = [[None] * seqs_per_slab for _ in range(n_groups)]
        for e in range(seqs_per_slab):
            base = e * dec_seq
            sel_new = one_hot((sel_row == base + sel_lane) & (sel_lane < dec_seq))
            per_group = []
            for g, dil in enumerate(DILATIONS):
                off = pair * LANES
                knt = spread(kvn_ref[slab, 2 * g * GROUP_WIDTH + off:2 * g * GROUP_WIDTH + off + LANES], sel_new)
                vnt = spread(kvn_ref[slab, (2 * g + 1) * GROUP_WIDTH + off:(2 * g + 1) * GROUP_WIDTH + off + LANES], sel_new)
                q8 = q_ref[slab, g * GROUP_WIDTH + off:g * GROUP_WIDTH + off + LANES]
                n_tiles = cache_refs[g].shape[5] // LANES
                if dil == 1:
                    qms = [spread(q8, one_hot(sel_row == base + i)) for i in range(dec_seq)]
                    valids = [valid_mask(n_tiles, lambda p, i=i: p >= i, lambda p, i=i: p <= i) for i in range(dec_seq)]
                else:
                    qms = [spread(q8, one_hot((sel_row == base + sel_lane % dil) & (sel_lane % dil < dec_seq)))]
                    valids = [valid_mask(n_tiles, lambda p, dil=dil: p % dil < dec_seq, lambda p: p < dec_seq)]
                heads = []
                for hh in range(2):
                    hrows = slice(hh * HEAD_DIM, (hh + 1) * HEAD_DIM)
                    num = top = den = None
                    for i, (qm, valid) in enumerate(zip(qms, valids)):
                        n_i, t_i, d_i = attend(qm[hrows], cache_refs[g], e, 2 * pair + hh, knt[hrows], vnt[hrows],
                                               n_tiles, valid, dil if dil > 1 else 0)
                        if num is None:
                            num, top, den = n_i, t_i, d_i
                        else:
                            mine = query_class == i
                            num, top, den = jnp.where(mine, n_i, num), jnp.where(mine, t_i, top), jnp.where(mine, d_i, den)
                    heads.append([num, top, den])
                per_group.append(heads)
            for hh in range(2):
                tops = [per_group[g][hh][1] for g in range(n_groups)]
                peak = functools.reduce(jnp.maximum, tops)
                scales = [jnp.exp2(t - peak) for t in tops]
                total = functools.reduce(jnp.add, [per_group[g][hh][2] * scales[g] for g in range(n_groups)])
                for g in range(n_groups):
                    per_group[g][hh] = per_group[g][hh][0] * (scales[g] / total)[0:1, :]
            for g in range(n_groups):
                mixed[g][e] = jnp.concatenate(per_group[g], axis=0)
        sel_out = one_hot(sel_lane == widest * (sel_row // dec_seq) + sel_row % dec_seq)
        for g in range(n_groups):
            both = mixed[g][0]
            for e in range(1, seqs_per_slab):
                both = jnp.where((lane >= widest * e) & (lane < widest * (e + 1)), mixed[g][e], both)
            hi = both.astype(BF16)
            lo = (both - hi.astype(F32)).astype(BF16)
            dims = (((1,), (1,)), ((), ()))
            rows = (lax.dot_general(sel_out, hi, dims, preferred_element_type=F32)
                    + lax.dot_general(sel_out, lo, dims, preferred_element_type=F32))
            att_ref[slab, g * GROUP_WIDTH + pair * LANES:g * GROUP_WIDTH + (pair + 1) * LANES] = rows[:SUBLANES]


def _sample_attention(q, kvn, caches_t, layer, dec_seq):
    rows, width = q.shape
    seqs_per_slab = SUBLANES // dec_seq
    cache_spec = lambda c: pl.BlockSpec((1, seqs_per_slab) + c.shape[2:], lambda i: (layer, i, 0, 0, 0, 0))
    return pl.pallas_call(
        functools.partial(_sample_attn_kernel, dec_seq=dec_seq),
        grid=(rows // SUBLANES,),
        in_specs=[_const_spec(q.shape), _const_spec(kvn.shape)] + [cache_spec(c) for c in caches_t],
        out_specs=pl.BlockSpec((rows, width), lambda i: (0, 0)),
        out_shape=jax.ShapeDtypeStruct((rows, width), F32),
        compiler_params=_params(1),
        name="sample_attn",
    )(q, kvn, *caches_t)


MIX_SUBTILE = 256


def _mix_kernel(*refs, tm, alpha, n_att, cross_attend):
    u_ref, gv_ref, x_ref, wsp_ref, bsp_ref, wm_ref, g_ref, b_ref = refs[n_att:n_att + 8]
    xattn_refs = refs[n_att + 8:-1]
    out_ref = refs[-1]
    t_idx = lax.broadcasted_iota(jnp.int32, (SGU_CHUNK, SGU_CHUNK), 0)
    s_idx = lax.broadcasted_iota(jnp.int32, (SGU_CHUNK, SGU_CHUNK), 1)
    w_sp = [jnp.where(s_idx <= t_idx, wsp_ref[k], 0.0).astype(BF16) for k in range(wsp_ref.shape[0])]
    sgu_width = gv_ref.shape[2]

    def gated_chunk(crow):
        slabs = []
        for pair in range(sgu_width // LANES):
            gp = gv_ref[0, crow, pair * LANES:(pair + 1) * LANES]
            mixed = jnp.zeros((SGU_CHUNK, LANES), F32)
            for hh in range(2):
                gm = jnp.where(_head_half_mask(LANES, hh), gp, 0.0).astype(BF16)
                mixed = mixed + jnp.dot(w_sp[2 * pair + hh], gm, preferred_element_type=F32)
            slabs.append(mixed)
        return (u_ref[0, crow, :] * (jnp.concatenate(slabs, axis=1) + bsp_ref[...])).astype(BF16)

    sub = min(tm, MIX_SUBTILE)
    for r0 in range(0, tm, sub):
        rws = slice(r0, r0 + sub)
        sgu = jnp.concatenate([gated_chunk(slice(c, c + SGU_CHUNK)) for c in range(r0, r0 + sub, SGU_CHUNK)], axis=0)
        lhs = jnp.concatenate([a[0, rws, :].astype(BF16) for a in refs[:n_att]] + [sgu], axis=1)
        y = alpha * x_ref[0, rws, :] + jnp.dot(lhs, wm_ref[...], preferred_element_type=F32)
        x1 = _layer_norm(y, g_ref[...], b_ref[...])
        out_ref[0, rws, :] = _xattn_rows(x1, *xattn_refs, alpha) if cross_attend else x1


def _mix(att_inputs, u, gv, x, w_sp, b_sp_tile, w_mix_b, layer, g, b, tm, alpha, cross=None):
    bk, tk, d = x.shape
    sgu_width = u.shape[2]
    row = lambda w: pl.BlockSpec((1, tm, w), lambda i, j: (i, j, 0))
    cross_specs = []
    if cross:
        mkv = cross[0]
        cross_specs = ([pl.BlockSpec((None, 1) + mkv.shape[2:], lambda i, j: (layer, i, 0, 0))]
                       + [_layer_spec(p, layer) for p in cross[1:]])
    return pl.pallas_call(
        functools.partial(_mix_kernel, tm=tm, alpha=alpha, n_att=len(att_inputs), cross_attend=bool(cross)),
        grid=(bk, tk // tm),
        in_specs=[row(a.shape[2]) for a in att_inputs] + [
            row(sgu_width), row(sgu_width), row(d),
            _const_spec(w_sp.shape), _const_spec(b_sp_tile.shape), _layer_spec(w_mix_b, layer),
            _layer_spec(g, layer), _layer_spec(b, layer)] + cross_specs,
        out_specs=row(d),
        out_shape=jax.ShapeDtypeStruct((bk, tk, d), F32),
        compiler_params=_params(2),
        name="mix_xattn" if cross else "mix",
    )(*att_inputs, u, gv, x, w_sp, b_sp_tile, w_mix_b, g, b, *(cross or ()))


def _mem_row_stride(d):
    return 2 * (d // MEM_HEADS // LANES) * MEM_HEADS


def _memkv_kernel(m_ref, w_ref, flat_ref, out_b_ref):
    depth, _, n_mem, width = out_b_ref.shape
    d = width // 2
    hd = d // MEM_HEADS
    n_chunks = hd // LANES
    mb = m_ref[0].astype(BF16)
    for layer in range(depth):
        mkv = jnp.dot(mb, w_ref[layer], preferred_element_type=F32)
        out_b_ref[layer, 0] = mkv.astype(BF16)
        for kv in range(2):
            for c in range(n_chunks):
                for h in range(MEM_HEADS):
                    col = kv * d + h * hd + c * LANES
                    rows = pl.ds((kv * n_chunks + c) * MEM_HEADS + h, n_mem, stride=_mem_row_stride(d))
                    flat_ref[layer, rows, :] = mkv[:, col:col + LANES]


def _memkv(mem, w_xkv_b):
    bk, n_mem, d = mem.shape
    depth, _, width = w_xkv_b.shape
    flat_rows = n_mem * _mem_row_stride(d)
    return pl.pallas_call(
        _memkv_kernel,
        grid=(bk,),
        in_specs=[pl.BlockSpec((1, n_mem, d), lambda i: (i, 0, 0)), _const_spec(w_xkv_b.shape)],
        out_specs=[pl.BlockSpec((depth, None, flat_rows, LANES), lambda i: (0, i, 0, 0)),
                   pl.BlockSpec((depth, 1, n_mem, width), lambda i: (0, i, 0, 0))],
        out_shape=[jax.ShapeDtypeStruct((depth, bk, flat_rows, LANES), F32),
                   jax.ShapeDtypeStruct((depth, bk, n_mem, width), BF16)],
        compiler_params=_params(1),
        name="memkv",
    )(mem, w_xkv_b)


def _softmax_rows(s):
    m = jnp.max(s, axis=-1, keepdims=True)
    p = jnp.exp(s - m)
    return p * (1.0 / jnp.sum(p, axis=-1, keepdims=True))


def _xattn_rows(x, mkv_ref, wq_ref, wo_ref, g_ref, b_ref, alpha):
    d = x.shape[1]
    hd = d // MEM_HEADS
    qx = jnp.dot(x.astype(BF16), wq_ref[...], preferred_element_type=F32)
    heads = []
    for h in range(MEM_HEADS):
        qh = qx[:, h * hd:(h + 1) * hd].astype(BF16)
        kh = mkv_ref[0, :, h * hd:(h + 1) * hd]
        vh = mkv_ref[0, :, d + h * hd:d + (h + 1) * hd]
        s = lax.dot_general(qh, kh, (((1,), (1,)), ((), ())), preferred_element_type=F32)
        m = jnp.max(s, axis=-1, keepdims=True)
        p = jnp.exp(s - m)
        inv = 1.0 / jnp.sum(p, axis=-1, keepdims=True)
        heads.append((jnp.dot(p.astype(BF16), vh, preferred_element_type=F32) * inv).astype(BF16))
    y = alpha * x + jnp.dot(jnp.concatenate(heads, axis=1), wo_ref[...], preferred_element_type=F32)
    return _layer_norm(y, g_ref[...], b_ref[...])


def _sample_xattn_kernel(x_ref, mkv_ref, wq_ref, wo_ref, g_ref, b_ref, out_ref, qx_ref, ox_ref,
                         *, dec_seq, seqs_per_step, alpha):
    step = pl.program_id(0)
    d = x_ref.shape[1]
    hd = d // MEM_HEADS
    seqs_per_slab = SUBLANES // dec_seq

    @pl.when(step == 0)
    def _():
        qx_ref[...] = jnp.dot(x_ref[...].astype(BF16), wq_ref[...], preferred_element_type=F32)

    n_chunks = hd // LANES
    rows_per_mem = 2 * n_chunks * MEM_HEADS
    n_mem = mkv_ref.shape[2] // rows_per_mem

    def head_matrix(b_local, kv, h):
        chunks = [mkv_ref[0, b_local, pl.ds((kv * n_chunks + c) * MEM_HEADS + h, n_mem, stride=rows_per_mem), :]
                  for c in range(n_chunks)]
        return jnp.concatenate(chunks, axis=1).astype(BF16)

    slab_row = lax.broadcasted_iota(jnp.int32, (SUBLANES, 1), 0)
    for j in range(seqs_per_step // seqs_per_slab):
        slab = pl.ds(pl.multiple_of((step * (seqs_per_step // seqs_per_slab) + j) * SUBLANES, SUBLANES), SUBLANES)
        for h in range(MEM_HEADS):
            q8 = qx_ref[slab, h * hd:(h + 1) * hd]
            qh = jnp.concatenate([q8, jnp.zeros_like(q8)], axis=0).astype(BF16)
            o8 = jnp.zeros((SUBLANES, hd), F32)
            for e in range(seqs_per_slab):
                b_local = j * seqs_per_slab + e
                s_t = lax.dot_general(head_matrix(b_local, 0, h), qh, (((1,), (1,)), ((), ())),
                                      preferred_element_type=F32)
                p_t = jnp.exp(s_t - jnp.max(s_t, axis=0, keepdims=True))
                p_t = (p_t * (1.0 / jnp.sum(p_t, axis=0, keepdims=True))).astype(BF16)
                oh = lax.dot_general(p_t, head_matrix(b_local, 1, h), (((0,), (0,)), ((), ())),
                                     preferred_element_type=F32)
                o8 = jnp.where(slab_row // dec_seq == e, oh[:SUBLANES], o8)
            ox_ref[slab, h * hd:(h + 1) * hd] = o8

    @pl.when(step == pl.num_programs(0) - 1)
    def _():
        y = alpha * x_ref[...] + jnp.dot(ox_ref[...].astype(BF16), wo_ref[...], preferred_element_type=F32)
        out_ref[...] = _layer_norm(y, g_ref[...], b_ref[...])


def _sample_xattn(x, mkv_flat, layer, wq_b, wo_b, g, b, dec_seq, seqs_per_step, alpha):
    rows, d = x.shape
    _, n_seq, flat_rows, _ = mkv_flat.shape
    return pl.pallas_call(
        functools.partial(_sample_xattn_kernel, dec_seq=dec_seq, seqs_per_step=seqs_per_step, alpha=alpha),
        grid=(n_seq // seqs_per_step,),
        in_specs=[_const_spec(x.shape),
                  pl.BlockSpec((1, seqs_per_step, flat_rows, LANES), lambda i: (layer, i, 0, 0)),
                  _layer_spec(wq_b, layer), _layer_spec(wo_b, layer), _layer_spec(g, layer), _layer_spec(b, layer)],
        out_specs=pl.BlockSpec((rows, d), lambda i: (0, 0)),
        out_shape=jax.ShapeDtypeStruct((rows, d), F32),
        scratch_shapes=[pltpu.VMEM((rows, d), F32), pltpu.VMEM((rows, d), F32)],
        compiler_params=_params(1),
        name="sample_xattn",
    )(x, mkv_flat, wq_b, wo_b, g, b)


MLP_SUBTILE = 512


def _mlp_kernel(*refs, ff_chunk, alpha, windows, first_layer):
    n_win = len(windows)
    x_ref, wu_ref, wd_ref, g_ref, b_ref = refs[:5]
    kv_refs = refs[5:5 + n_win]
    out_ref = refs[-1 - n_win]
    window_refs = list(refs[len(refs) - n_win:])

    def write_window(grp):
        win_ref = window_refs[grp] if first_layer is None else window_refs[grp].at[first_layer]
        for part in range(2):
            t = kv_refs[grp][0, :, part * GROUP_WIDTH:(part + 1) * GROUP_WIDTH].T
            for h in range(HEADS_PER_GROUP):
                win_ref[part, h] = t[h * HEAD_DIM:(h + 1) * HEAD_DIM, :]
        if first_layer is not None:
            for other in range(window_refs[grp].shape[0]):
                if other != first_layer:
                    window_refs[grp][other] = jnp.zeros(window_refs[grp].shape[1:], F32)

    for grp, every_tile in enumerate(windows):
        if not every_tile:
            pl.when(pl.program_id(1) == pl.num_programs(1) - 1)(functools.partial(write_window, grp))
    for grp, every_tile in enumerate(windows):
        if every_tile:
            write_window(grp)

    tm = x_ref.shape[1]
    sub = min(tm, MLP_SUBTILE)
    for r0 in range(0, tm, sub):
        x = x_ref[0, r0:r0 + sub, :]
        xb = x.astype(BF16)
        y = alpha * x
        for c in range(wu_ref.shape[1] // ff_chunk):
            h = jnp.dot(xb, wu_ref[:, c * ff_chunk:(c + 1) * ff_chunk], preferred_element_type=F32)
            h = jnp.square(jnp.maximum(h, 0.0)).astype(BF16)
            y = y + jnp.dot(h, wd_ref[c * ff_chunk:(c + 1) * ff_chunk, :], preferred_element_type=F32)
        out_ref[0, r0:r0 + sub, :] = _layer_norm(y, g_ref[...], b_ref[...])


def _mlp(x, wu_b, wd_b, layer, g, b, tm, alpha, ff_chunk=1024, kvs=(), window_keeps=(), window_bufs=None):
    bk, tk, d = x.shape
    depth = wu_b.shape[0]
    row = pl.BlockSpec((1, tm, d), lambda i, j: (i, j, 0))
    aliased = list(window_bufs or ())
    owns_all_layers = bool(window_keeps) and not aliased
    lead, at = (depth, 0) if owns_all_layers else (None, layer)
    kv_specs, out_shape, out_specs, windows = [], [jax.ShapeDtypeStruct((bk, tk, d), F32)], [row], []
    for kv, keep in zip(kvs, window_keeps):
        every_tile = keep == tk
        assert every_tile or tk % keep == 0
        windows.append(every_tile)
        n = tm if every_tile else keep
        kv_specs.append(pl.BlockSpec((1, n, kv.shape[2]),
                                     (lambda i, j: (i, j, 0)) if every_tile else (lambda i, j, last=tk // keep - 1: (i, last, 0))))
        out_shape.append(jax.ShapeDtypeStruct((depth, bk, 2, HEADS_PER_GROUP, HEAD_DIM, keep), F32))
        out_specs.append(pl.BlockSpec((lead, None, 2, HEADS_PER_GROUP, HEAD_DIM, n),
                                      (lambda i, j: (at, i, 0, 0, 0, j)) if every_tile else (lambda i, j: (at, i, 0, 0, 0, 0))))
    n_in = 5 + len(kv_specs)
    return pl.pallas_call(
        functools.partial(_mlp_kernel, ff_chunk=ff_chunk, alpha=alpha, windows=tuple(windows),
                          first_layer=layer if owns_all_layers else None),
        grid=(bk, tk // tm),
        in_specs=[row, _layer_spec(wu_b, layer), _layer_spec(wd_b, layer), _layer_spec(g, layer), _layer_spec(b, layer)]
        + kv_specs + [pl.BlockSpec(memory_space=pl.ANY)] * len(aliased),
        out_specs=out_specs,
        out_shape=out_shape,
        input_output_aliases={n_in + k: 1 + k for k in range(len(aliased))},
        compiler_params=_params(2),
        name="mlp",
    )(x, wu_b, wd_b, g, b, *kvs, *aliased)


def kernel(x_prompt, x_sample, cache_kv_w128, cache_kv_w512, cache_kv_w2048, cache_mem_kv, mem_prompt,
           w_in, sgu_ln_g, sgu_ln_b, w_spatial, b_spatial, w_mix_out, ln1_g, ln1_b,
           w_xq, w_xkv, w_xo, ln2_g, ln2_b, w_up, w_down, ln3_g, ln3_b):
    depth = w_in.shape[0]
    bp, tp, d = x_prompt.shape
    bs, ts, _ = x_sample.shape
    past_len = 8192
    alpha = float((2 * depth) ** 0.25)
    att_width = len(DILATIONS) * GROUP_WIDTH
    sgu_width = sgu_ln_g.shape[1]
    sgu_groups = w_spatial.shape[1]
    n_mem = mem_prompt.shape[1]
    rows_s = bs * ts
    assert tp % (DILATIONS[-1] * WINDOW_STEPS) == 0 and SUBLANES % ts == 0 and rows_s % SGU_CHUNK == 0
    assert d // MEM_HEADS == GROUP_WIDTH and sgu_width == GROUP_WIDTH

    col_scale = jnp.concatenate([jnp.full((att_width,), HEAD_DIM ** -0.5 * math.log2(math.e), F32),
                                 jnp.ones((w_in.shape[2] - att_width,), F32)])
    w_in_b = (w_in * col_scale).astype(BF16)
    w_xq_b = (w_xq * (d // MEM_HEADS) ** -0.5).astype(BF16)
    w_mix_b, w_xkv_b, w_xo_b = w_mix_out.astype(BF16), w_xkv.astype(BF16), w_xo.astype(BF16)
    w_up_b, w_down_b = w_up.astype(BF16), w_down.astype(BF16)
    vec = lambda p: p[:, None, :]
    sgu_g, sgu_b, g1, b1, g2, b2, g3, b3 = map(vec, (sgu_ln_g, sgu_ln_b, ln1_g, ln1_b, ln2_g, ln2_b, ln3_g, ln3_b))

    tables_p = _rope_tables(jnp.arange(tp, dtype=jnp.int32))
    tables_s = _rope_tables(past_len + (jnp.arange(rows_s, dtype=jnp.int32) % ts))

    eye = jnp.eye(rows_s // ts, dtype=F32)
    tri = jnp.tril(jnp.ones((ts, ts), F32))

    hp = x_prompt
    hs = x_sample.reshape(1, rows_s, d)
    out_rows_s = [[] for _ in DILATIONS]
    out_gv = []
    window_keeps = tuple(min(dil * WINDOW_STEPS, tp) for dil in DILATIONS)
    windows = None
    mem_out, mkv = _memkv(mem_prompt, w_xkv_b)
    caches_t = [jnp.transpose(c, (0, 1, 3, 4, 5, 2)) for c in (cache_kv_w128, cache_kv_w512, cache_kv_w2048)]
    mem_chunks = d // MEM_HEADS // LANES
    mem_flat = (cache_mem_kv.reshape(depth, bs, n_mem, 2, MEM_HEADS, mem_chunks, LANES)
                .transpose(0, 1, 2, 3, 5, 4, 6).reshape(depth, bs, n_mem * 2 * mem_chunks * MEM_HEADS, LANES))
    for l in range(depth):
        q, kv0, kv1, kv2, u, gv = _inproj(hp, w_in_b, l, tables_p, sgu_g, sgu_b, tm=512, gate_dtype=BF16)
        atts = _prompt_attention(q, [kv0, kv1, kv2])
        b_tile = jnp.repeat(b_spatial[l][:, :SGU_CHUNK].T, sgu_width // sgu_groups, axis=1)
        hp = _mix(atts, u, gv, hp, w_spatial[l], b_tile, w_mix_b, l, g1, b1, tm=512, alpha=alpha,
                  cross=(mkv, w_xq_b, w_xo_b, g2, b2))
        hp, *windows = _mlp(hp, w_up_b, w_down_b, l, g3, b3, tm=512, alpha=alpha,
                            kvs=(kv0, kv1, kv2), window_keeps=window_keeps, window_bufs=windows)

        q, kv0, kv1, kv2, u, gv = _inproj(hs, w_in_b, l, tables_s, sgu_g, sgu_b, tm=rows_s, gate_dtype=F32)
        out_gv.append(gv.reshape(bs, ts, sgu_width))
        for g, kv in enumerate((kv0, kv1, kv2)):
            out_rows_s[g].append(kv.reshape(bs, ts, 2, HEADS_PER_GROUP, HEAD_DIM))
        kvn = jnp.concatenate([kv0[0], kv1[0], kv2[0]], axis=1)
        att = _sample_attention(q[0], kvn, caches_t, l, dec_seq=ts)
        w_sp_s = jnp.einsum("ab,gts->gatbs", eye, w_spatial[l][:, :ts, :ts] * tri).reshape(sgu_groups, rows_s, rows_s)
        b_tile_s = jnp.repeat(jnp.tile(b_spatial[l][:, :ts].T, (rows_s // ts, 1)), sgu_width // sgu_groups, axis=1)
        hs = _mix([att[None]], u, gv, hs, w_sp_s, b_tile_s, w_mix_b, l, g1, b1, tm=rows_s, alpha=alpha)
        hs2 = _sample_xattn(hs[0], mem_flat, l, w_xq_b, w_xo_b, g2, b2, dec_seq=ts, seqs_per_step=4, alpha=alpha)
        hs, = _mlp(hs2[None], w_up_b, w_down_b, l, g3, b3, tm=rows_s, alpha=alpha)

    rows_p = [jnp.transpose(w, (0, 1, 5, 2, 3, 4)) for w in windows]
    mem_p = (mem_out.reshape(depth, bp, n_mem, 2, mem_chunks, MEM_HEADS, LANES)
             .transpose(0, 1, 2, 3, 5, 4, 6).reshape(depth, bp, n_mem, 2, MEM_HEADS, d // MEM_HEADS))
    stack = lambda xs: jnp.stack(xs)
    return (hp, hs.reshape(bs, ts, d), rows_p[0], rows_p[1], rows_p[2], mem_p,
            stack(out_rows_s[0]), stack(out_rows_s[1]), stack(out_rows_s[2]), stack(out_gv))
```

```python
import functools
import math

import jax
import jax.numpy as jnp
from jax import lax
from jax.experimental import pallas as pl
from jax.experimental.pallas import tpu as pltpu

F32 = jnp.float32
BF16 = jnp.bfloat16

HEAD_DIM = 64
HEADS_PER_GROUP = 4
GROUP_WIDTH = HEAD_DIM * HEADS_PER_GROUP
DILATIONS = (1, 4, 16)
WINDOW_STEPS = 128
ROT_DIM = 16
ROPE_THETA = 500000.0
SGU_CHUNK = 128
MEM_HEADS = 4
LN_EPS = 1e-5
NEG = -1e30
LANES = 128
SUBLANES = 8
VMEM_LIMIT = 56 * 1024 * 1024


def _params(n_grid_dims):
    return pltpu.CompilerParams(
        dimension_semantics=("arbitrary",) * n_grid_dims,
        vmem_limit_bytes=VMEM_LIMIT)


def _const_spec(shape):
    nd = len(shape)
    return pl.BlockSpec(shape, lambda *_: (0,) * nd, pipeline_mode=pl.Buffered(1))


def _layer_spec(stacked, layer):
    rest = stacked.shape[1:]
    return pl.BlockSpec((None,) + rest, lambda *_: (layer,) + (0,) * len(rest), pipeline_mode=pl.Buffered(1))


def _layer_norm(y, g, b):
    mu = jnp.mean(y, axis=-1, keepdims=True)
    yc = y - mu
    var = jnp.mean(yc * yc, axis=-1, keepdims=True)
    return yc * lax.rsqrt(var + LN_EPS) * g + b


def _gelu_tanh(x):
    return 0.5 * x * (1.0 + jnp.tanh(0.7978845608028654 * (x + 0.044715 * (x * x * x))))


def _head_half_mask(width, half):
    lane = lax.broadcasted_iota(jnp.int32, (1, width), 1)
    return (lane // HEAD_DIM) % 2 == half


def _inproj_kernel(x_ref, w_ref, cos_ref, sin_lo_ref, sin_hi_ref, g_ref, b_ref,
                   q_ref, kv0_ref, kv1_ref, kv2_ref, u_ref, gv_ref, *, att_width, sgu_width):
    xb = x_ref[0].astype(BF16)
    cos = cos_ref[...]
    sin_lo = sin_lo_ref[...]
    sin_hi = sin_hi_ref[...]

    def proj(c0, width):
        return jnp.dot(xb, w_ref[:, c0:c0 + width], preferred_element_type=F32)

    def rope(t):
        return t * cos + pltpu.roll(t, LANES - ROT_DIM // 2, 1) * sin_lo + pltpu.roll(t, ROT_DIM // 2, 1) * sin_hi

    def rope_group(t):
        return jnp.concatenate([rope(t[:, s:s + LANES]) for s in range(0, GROUP_WIDTH, LANES)], axis=1)

    u_ref[0] = _gelu_tanh(proj(3 * att_width, sgu_width)).astype(u_ref.dtype)
    gate = _gelu_tanh(proj(3 * att_width + sgu_width, sgu_width))
    gv_ref[0] = _layer_norm(gate, g_ref[...], b_ref[...]).astype(gv_ref.dtype)
    kv_refs = (kv0_ref, kv1_ref, kv2_ref)
    for grp, kv_ref in enumerate(kv_refs):
        c = grp * GROUP_WIDTH
        q_ref[0, :, c:c + GROUP_WIDTH] = rope_group(proj(c, GROUP_WIDTH))
        kv_ref[0, :, 0:GROUP_WIDTH] = rope_group(proj(att_width + c, GROUP_WIDTH))
    for grp, kv_ref in enumerate(kv_refs):
        kv_ref[0, :, GROUP_WIDTH:2 * GROUP_WIDTH] = proj(2 * att_width + grp * GROUP_WIDTH, GROUP_WIDTH)


def _inproj(x, w_in_b, layer, tables, g, b, tm, gate_dtype):
    bk, tk, d = x.shape
    att_width = len(DILATIONS) * GROUP_WIDTH
    sgu_width = (w_in_b.shape[2] - 3 * att_width) // 2
    cos, sin_lo, sin_hi = tables
    row = lambda w: pl.BlockSpec((1, tm, w), lambda i, j: (i, j, 0))
    tab = pl.BlockSpec((tm, LANES), lambda i, j: (j, 0))
    out_shape = ([jax.ShapeDtypeStruct((bk, tk, w), F32) for w in (att_width,) + (2 * GROUP_WIDTH,) * len(DILATIONS)]
                 + [jax.ShapeDtypeStruct((bk, tk, sgu_width), gate_dtype)] * 2)
    return pl.pallas_call(
        functools.partial(_inproj_kernel, att_width=att_width, sgu_width=sgu_width),
        grid=(bk, tk // tm),
        in_specs=[row(d), _layer_spec(w_in_b, layer), tab, tab, tab, _layer_spec(g, layer), _layer_spec(b, layer)],
        out_specs=[row(s.shape[2]) for s in out_shape],
        out_shape=out_shape,
        compiler_params=_params(2),
        name="inproj",
    )(x, w_in_b, cos, sin_lo, sin_hi, g, b)


def _rope_tables(pos):
    half = ROT_DIM // 2
    inv = ROPE_THETA ** (-jnp.arange(half, dtype=F32) / half)
    ang = pos.astype(F32)[:, None] * inv[None, :]
    cos, sin = jnp.cos(ang), jnp.sin(ang)
    zeros = jnp.zeros((pos.shape[0], HEAD_DIM - ROT_DIM), F32)
    zero_half = jnp.zeros_like(sin)
    cos_head = jnp.concatenate([cos, cos, zeros + 1.0], axis=1)
    lo_head = jnp.concatenate([-sin, zero_half, zeros], axis=1)
    hi_head = jnp.concatenate([zero_half, sin, zeros], axis=1)
    two = lambda t: jnp.concatenate([t, t], axis=1)
    return two(cos_head), two(lo_head), two(hi_head)


ATTN_MIX_ROWS = 256
ATTN_TILES_PER_BATCH = 32


def _attn_kernel(*refs, seq):
    n_groups = len(DILATIONS)
    q_refs, k_refs, v_refs = refs[0:n_groups], refs[n_groups:2 * n_groups], refs[2 * n_groups:3 * n_groups]
    att_refs = refs[3 * n_groups:4 * n_groups]
    o_s, m_s, den_s = refs[4 * n_groups:]
    blk = WINDOW_STEPS
    x_idx = lax.broadcasted_iota(jnp.int32, (blk, 2 * blk), 0)
    k_idx = lax.broadcasted_iota(jnp.int32, (blk, 2 * blk), 1)
    band_mask = (k_idx >= x_idx) & (k_idx <= x_idx + blk)
    causal_mask = (lax.broadcasted_iota(jnp.int32, (blk, blk), 1)
                   <= lax.broadcasted_iota(jnp.int32, (blk, blk), 0))
    half_masks = [_head_half_mask(LANES, hh) for hh in range(2)]

    def batch(g, dil, blocks):
        def rows(r, first_block, n):
            start = r + dil * blk * first_block
            return pl.ds(start, n) if dil == 1 else pl.ds(start, n, stride=dil)

        first = blocks[0][1] == 0
        mask = causal_mask if first else band_mask
        scores, values = [], []
        for r, c in blocks:
            k_rows = rows(r, 0, blk) if first else rows(r, c - 1, 2 * blk)
            qp = q_refs[g][0, rows(r, c, blk), :]
            kp = k_refs[g][0, k_rows, :].astype(BF16)
            values.append(v_refs[g][0, k_rows, :].astype(BF16))
            for hh in range(2):
                qh = jnp.where(half_masks[hh], qp, 0.0).astype(BF16)
                s = lax.dot_general(qh, kp, (((1,), (1,)), ((), ())), preferred_element_type=F32)
                scores.append(jnp.where(mask, s, NEG))
        s_all = jnp.concatenate(scores, axis=0)
        m = jnp.max(s_all, axis=-1, keepdims=True)
        p = jnp.exp2(s_all - m)
        den = jnp.sum(p, axis=-1, keepdims=True)
        p = p.astype(BF16)
        for n, (r, c) in enumerate(blocks):
            piece = lambda t, hh: t[(2 * n + hh) * blk:(2 * n + hh + 1) * blk]
            outs = [jnp.dot(piece(p, hh), values[n], preferred_element_type=F32) for hh in range(2)]
            o_s[g, rows(r, c, blk), :] = jnp.where(half_masks[0], outs[0], outs[1])
            m_s[g, rows(r, c, blk), :] = jnp.where(half_masks[0], piece(m, 0), piece(m, 1))
            den_s[g, rows(r, c, blk), :] = jnp.where(half_masks[0], piece(den, 0), piece(den, 1))

    for g, dil in enumerate(DILATIONS):
        n_blocks = seq // dil // blk
        first_blocks = [(r, 0) for r in range(dil)]
        later_blocks = [(r, c) for r in range(dil) for c in range(1, n_blocks)]
        per_first = ATTN_TILES_PER_BATCH // 2
        per_later = ATTN_TILES_PER_BATCH // 4
        for i in range(0, len(first_blocks), per_first):
            batch(g, dil, first_blocks[i:i + per_first])
        for i in range(0, len(later_blocks), per_later):
            batch(g, dil, later_blocks[i:i + per_later])

    for t0 in range(0, seq, ATTN_MIX_ROWS):
        rws = slice(t0, t0 + ATTN_MIX_ROWS)
        tops = [m_s[g, rws, :] for g in range(n_groups)]
        peak = functools.reduce(jnp.maximum, tops)
        scales = [jnp.exp2(t - peak) for t in tops]
        inv = 1.0 / functools.reduce(jnp.add, [den_s[g, rws, :] * scales[g] for g in range(n_groups)])
        for g in range(n_groups):
            att_refs[g][0, rws, :] = (o_s[g, rws, :] * (scales[g] * inv)).astype(att_refs[g].dtype)


def _prompt_attention(q, kvs):
    bk, tk, _ = q.shape
    n_groups = len(DILATIONS)
    pairs = GROUP_WIDTH // LANES
    col = lambda first: pl.BlockSpec((1, tk, LANES), lambda i, j: (i, 0, first + j))
    out = jax.ShapeDtypeStruct((bk, tk, GROUP_WIDTH), BF16)
    return pl.pallas_call(
        functools.partial(_attn_kernel, seq=tk),
        grid=(bk, pairs),
        in_specs=([col(g * pairs) for g in range(n_groups)]
                  + [col(0)] * n_groups
                  + [col(pairs)] * n_groups),
        out_specs=[col(0)] * n_groups,
        out_shape=[out] * n_groups,
        scratch_shapes=[pltpu.VMEM((n_groups, tk, LANES), F32)] * 3,
        compiler_params=_params(2),
        name="attn",
    )(*([q] * n_groups), *kvs, *kvs)


def _sample_attn_kernel(q_ref, kvn0_ref, kvn1_ref, kvn2_ref, c0_ref, c1_ref, c2_ref, att_ref, *, dec_seq):
    kvn_refs = (kvn0_ref, kvn1_ref, kvn2_ref)
    cache_refs = (c0_ref, c1_ref, c2_ref)
    seqs_per_slab = SUBLANES // dec_seq
    n_groups = len(DILATIONS)
    widest = DILATIONS[-1]
    slab = pl.ds(pl.multiple_of(pl.program_id(0) * SUBLANES, SUBLANES), SUBLANES)
    lane = lax.broadcasted_iota(jnp.int32, (1, LANES), 1)
    row8 = lax.broadcasted_iota(jnp.int32, (SUBLANES, LANES), 0)
    sel_rows = 2 * SUBLANES
    sel_row = lax.broadcasted_iota(jnp.int32, (sel_rows, LANES), 0)
    sel_lane = lax.broadcasted_iota(jnp.int32, (sel_rows, LANES), 1)
    query_class = lane % widest

    def one_hot(cond):
        return jnp.where(cond, 1.0, 0.0).astype(BF16)

    def spread(rows8, sel):
        padded = jnp.concatenate([rows8, jnp.zeros_like(rows8)], axis=0).astype(BF16)
        return lax.dot_general(padded, sel, (((0,), (0,)), ((), ())), preferred_element_type=F32)

    def class_fold(x, period, op):
        shift = LANES // 2
        while shift >= period:
            x = op(x, pltpu.roll(x, shift, 1))
            shift //= 2
        return x

    def valid_mask(n_tiles, cache_ok, new_ok):
        rows = SUBLANES * (n_tiles // SUBLANES + 1)
        r = lax.broadcasted_iota(jnp.int32, (rows, LANES), 0)
        p = lax.broadcasted_iota(jnp.int32, (rows, LANES), 1)
        return ((r < n_tiles) & cache_ok(p)) | ((r == n_tiles) & new_ok(p))

    def attend(qm, cache_ref, seq, head, knt, vnt, n_tiles, valid, period):
        groups = []
        for t0 in range(0, n_tiles + 1, SUBLANES):
            acc = jnp.full((SUBLANES, LANES), NEG, F32)
            for t in range(t0, min(t0 + SUBLANES, n_tiles + 1)):
                keys = knt if t == n_tiles else cache_ref[0, seq, 0, head, :, t * LANES:(t + 1) * LANES]
                acc = jnp.where(row8 == t - t0, jnp.sum(qm * keys, axis=0, keepdims=True), acc)
            groups.append(acc)
        scores = jnp.where(valid, groups[0] if len(groups) == 1 else jnp.concatenate(groups, axis=0), NEG)
        top = jnp.broadcast_to(jnp.max(scores, axis=0, keepdims=True), (SUBLANES, LANES))
        if period:
            top = class_fold(top, period, jnp.maximum)
        else:
            top = jnp.broadcast_to(jnp.max(top, axis=1, keepdims=True), (SUBLANES, LANES))
        probs = jnp.exp2(scores - jnp.concatenate([top] * len(groups), axis=0))
        den = jnp.broadcast_to(jnp.sum(probs, axis=0, keepdims=True), (SUBLANES, LANES))
        num = jnp.zeros((HEAD_DIM, LANES), F32)
        for t in range(n_tiles + 1):
            vals = vnt if t == n_tiles else cache_ref[0, seq, 1, head, :, t * LANES:(t + 1) * LANES]
            num = num + vals * probs[t:t + 1, :]
        if period:
            return class_fold(num, period, jnp.add), top, class_fold(den, period, jnp.add)
        return (jnp.broadcast_to(jnp.sum(num, axis=1, keepdims=True), (HEAD_DIM, LANES)), top,
                jnp.broadcast_to(jnp.sum(den, axis=1, keepdims=True), (SUBLANES, LANES)))

    for pair in range(GROUP_WIDTH // LANES):
        mixed = [[None] * seqs_per_slab for _ in range(n_groups)]
        for e in range(seqs_per_slab):
            base = e * dec_seq
            sel_new = one_hot((sel_row == base + sel_lane) & (sel_lane < dec_seq))
            per_group = []
            for g, dil in enumerate(DILATIONS):
                off = pair * LANES
                knt = spread(kvn_refs[g][0, slab, off:off + LANES], sel_new)
                vnt = spread(kvn_refs[g][0, slab, GROUP_WIDTH + off:GROUP_WIDTH + off + LANES], sel_new)
                q8 = q_ref[slab, g * GROUP_WIDTH + off:g * GROUP_WIDTH + off + LANES]
                n_tiles = cache_refs[g].shape[5] // LANES
                if dil == 1:
                    qms = [spread(q8, one_hot(sel_row == base + i)) for i in range(dec_seq)]
                    valids = [valid_mask(n_tiles, lambda p, i=i: p >= i, lambda p, i=i: p <= i) for i in range(dec_seq)]
                else:
                    qms = [spread(q8, one_hot((sel_row == base + sel_lane % dil) & (sel_lane % dil < dec_seq)))]
                    valids = [valid_mask(n_tiles, lambda p, dil=dil: p % dil < dec_seq, lambda p: p < dec_seq)]
                heads = []
                for hh in range(2):
                    hrows = slice(hh * HEAD_DIM, (hh + 1) * HEAD_DIM)
                    num = top = den = None
                    for i, (qm, valid) in enumerate(zip(qms, valids)):
                        n_i, t_i, d_i = attend(qm[hrows], cache_refs[g], e, 2 * pair + hh, knt[hrows], vnt[hrows],
                                               n_tiles, valid, dil if dil > 1 else 0)
                        if num is None:
                            num, top, den = n_i, t_i, d_i
                        else:
                            mine = query_class == i
                            num, top, den = jnp.where(mine, n_i, num), jnp.where(mine, t_i, top), jnp.where(mine, d_i, den)
                    heads.append([num, top, den])
                per_group.append(heads)
            for hh in range(2):
                tops = [per_group[g][hh][1] for g in range(n_groups)]
                peak = functools.reduce(jnp.maximum, tops)
                scales = [jnp.exp2(t - peak) for t in tops]
                total = functools.reduce(jnp.add, [per_group[g][hh][2] * scales[g] for g in range(n_groups)])
                for g in range(n_groups):
                    per_group[g][hh] = per_group[g][hh][0] * (scales[g] / total)[0:1, :]
            for g in range(n_groups):
                mixed[g][e] = jnp.concatenate(per_group[g], axis=0)
        sel_out = one_hot(sel_lane == widest * (sel_row // dec_seq) + sel_row % dec_seq)
        for g in range(n_groups):
            both = mixed[g][0]
            for e in range(1, seqs_per_slab):
                both = jnp.where((lane >= widest * e) & (lane < widest * (e + 1)), mixed[g][e], both)
            hi = both.astype(BF16)
            lo = (both - hi.astype(F32)).astype(BF16)
            dims = (((1,), (1,)), ((), ()))
            rows = (lax.dot_general(sel_out, hi, dims, preferred_element_type=F32)
                    + lax.dot_general(sel_out, lo, dims, preferred_element_type=F32))
            att_ref[slab, g * GROUP_WIDTH + pair * LANES:g * GROUP_WIDTH + (pair + 1) * LANES] = rows[:SUBLANES]


def _sample_attention(q, kvns, caches_t, layer, dec_seq):
    rows, width = q.shape
    seqs_per_slab = SUBLANES // dec_seq
    cache_spec = lambda c: pl.BlockSpec((1, seqs_per_slab) + c.shape[2:], lambda i: (layer, i, 0, 0, 0, 0))
    return pl.pallas_call(
        functools.partial(_sample_attn_kernel, dec_seq=dec_seq),
        grid=(rows // SUBLANES,),
        in_specs=[_const_spec(q.shape)] + [_const_spec(k.shape) for k in kvns] + [cache_spec(c) for c in caches_t],
        out_specs=pl.BlockSpec((rows, width), lambda i: (0, 0)),
        out_shape=jax.ShapeDtypeStruct((rows, width), F32),
        compiler_params=_params(1),
        name="sample_attn",
    )(q, *kvns, *caches_t)


MIX_SUBTILE = 256


def _mix_kernel(*refs, tm, alpha, n_att, cross_attend):
    u_ref, gv_ref, x_ref, wsp_ref, bsp_ref, wm_ref, g_ref, b_ref = refs[n_att:n_att + 8]
    xattn_refs = refs[n_att + 8:-1]
    out_ref = refs[-1]
    t_idx = lax.broadcasted_iota(jnp.int32, (SGU_CHUNK, SGU_CHUNK), 0)
    s_idx = lax.broadcasted_iota(jnp.int32, (SGU_CHUNK, SGU_CHUNK), 1)
    w_sp = [jnp.where(s_idx <= t_idx, wsp_ref[k], 0.0).astype(BF16) for k in range(wsp_ref.shape[0])]
    sgu_width = gv_ref.shape[2]

    def gated_chunk(crow):
        slabs = []
        for pair in range(sgu_width // LANES):
            gp = gv_ref[0, crow, pair * LANES:(pair + 1) * LANES]
            mixed = jnp.zeros((SGU_CHUNK, LANES), F32)
            for hh in range(2):
                gm = jnp.where(_head_half_mask(LANES, hh), gp, 0.0).astype(BF16)
                mixed = mixed + jnp.dot(w_sp[2 * pair + hh], gm, preferred_element_type=F32)
            slabs.append(mixed)
        return (u_ref[0, crow, :] * (jnp.concatenate(slabs, axis=1) + bsp_ref[...])).astype(BF16)

    sub = min(tm, MIX_SUBTILE)
    for r0 in range(0, tm, sub):
        rws = slice(r0, r0 + sub)
        sgu = jnp.concatenate([gated_chunk(slice(c, c + SGU_CHUNK)) for c in range(r0, r0 + sub, SGU_CHUNK)], axis=0)
        lhs = jnp.concatenate([a[0, rws, :].astype(BF16) for a in refs[:n_att]] + [sgu], axis=1)
        y = alpha * x_ref[0, rws, :] + jnp.dot(lhs, wm_ref[...], preferred_element_type=F32)
        x1 = _layer_norm(y, g_ref[...], b_ref[...])
        out_ref[0, rws, :] = _xattn_rows(x1, *xattn_refs, alpha) if cross_attend else x1


def _mix(att_inputs, u, gv, x, w_sp, b_sp_tile, w_mix_b, layer, g, b, tm, alpha, cross=None):
    bk, tk, d = x.shape
    sgu_width = u.shape[2]
    row = lambda w: pl.BlockSpec((1, tm, w), lambda i, j: (i, j, 0))
    cross_specs = []
    if cross:
        mkv = cross[0]
        cross_specs = ([pl.BlockSpec((None, 1) + mkv.shape[2:], lambda i, j: (layer, i, 0, 0))]
                       + [_layer_spec(p, layer) for p in cross[1:]])
    return pl.pallas_call(
        functools.partial(_mix_kernel, tm=tm, alpha=alpha, n_att=len(att_inputs), cross_attend=bool(cross)),
        grid=(bk, tk // tm),
        in_specs=[row(a.shape[2]) for a in att_inputs] + [
            row(sgu_width), row(sgu_width), row(d),
            _const_spec(w_sp.shape), _const_spec(b_sp_tile.shape), _layer_spec(w_mix_b, layer),
            _layer_spec(g, layer), _layer_spec(b, layer)] + cross_specs,
        out_specs=row(d),
        out_shape=jax.ShapeDtypeStruct((bk, tk, d), F32),
        compiler_params=_params(2),
        name="mix_xattn" if cross else "mix",
    )(*att_inputs, u, gv, x, w_sp, b_sp_tile, w_mix_b, g, b, *(cross or ()))


def _mem_row_stride(d):
    return 2 * (d // MEM_HEADS // LANES) * MEM_HEADS


def _memkv_kernel(m_ref, w_ref, flat_ref, out_b_ref):
    depth, _, n_mem, width = out_b_ref.shape
    d = width // 2
    hd = d // MEM_HEADS
    n_chunks = hd // LANES
    mb = m_ref[0].astype(BF16)
    for layer in range(depth):
        mkv = jnp.dot(mb, w_ref[layer], preferred_element_type=F32)
        out_b_ref[layer, 0] = mkv.astype(BF16)
        for kv in range(2):
            for c in range(n_chunks):
                for h in range(MEM_HEADS):
                    col = kv * d + h * hd + c * LANES
                    rows = pl.ds((kv * n_chunks + c) * MEM_HEADS + h, n_mem, stride=_mem_row_stride(d))
                    flat_ref[layer, rows, :] = mkv[:, col:col + LANES]


def _memkv(mem, w_xkv_b):
    bk, n_mem, d = mem.shape
    depth, _, width = w_xkv_b.shape
    flat_rows = n_mem * _mem_row_stride(d)
    return pl.pallas_call(
        _memkv_kernel,
        grid=(bk,),
        in_specs=[pl.BlockSpec((1, n_mem, d), lambda i: (i, 0, 0)), _const_spec(w_xkv_b.shape)],
        out_specs=[pl.BlockSpec((depth, None, flat_rows, LANES), lambda i: (0, i, 0, 0)),
                   pl.BlockSpec((depth, 1, n_mem, width), lambda i: (0, i, 0, 0))],
        out_shape=[jax.ShapeDtypeStruct((depth, bk, flat_rows, LANES), F32),
                   jax.ShapeDtypeStruct((depth, bk, n_mem, width), BF16)],
        compiler_params=_params(1),
        name="memkv",
    )(mem, w_xkv_b)


def _softmax_rows(s):
    m = jnp.max(s, axis=-1, keepdims=True)
    p = jnp.exp(s - m)
    return p * (1.0 / jnp.sum(p, axis=-1, keepdims=True))


def _xattn_rows(x, mkv_ref, wq_ref, wo_ref, g_ref, b_ref, alpha):
    d = x.shape[1]
    hd = d // MEM_HEADS
    qx = jnp.dot(x.astype(BF16), wq_ref[...], preferred_element_type=F32)
    heads = []
    for h in range(MEM_HEADS):
        qh = qx[:, h * hd:(h + 1) * hd].astype(BF16)
        kh = mkv_ref[0, :, h * hd:(h + 1) * hd]
        vh = mkv_ref[0, :, d + h * hd:d + (h + 1) * hd]
        s = lax.dot_general(qh, kh, (((1,), (1,)), ((), ())), preferred_element_type=F32)
        m = jnp.max(s, axis=-1, keepdims=True)
        p = jnp.exp(s - m)
        inv = 1.0 / jnp.sum(p, axis=-1, keepdims=True)
        heads.append((jnp.dot(p.astype(BF16), vh, preferred_element_type=F32) * inv).astype(BF16))
    y = alpha * x + jnp.dot(jnp.concatenate(heads, axis=1), wo_ref[...], preferred_element_type=F32)
    return _layer_norm(y, g_ref[...], b_ref[...])


def _sample_xattn_kernel(x_ref, mkv_ref, wq_ref, wo_ref, g_ref, b_ref, out_ref, qx_ref, ox_ref,
                         *, dec_seq, seqs_per_step, alpha):
    step = pl.program_id(0)
    d = x_ref.shape[1]
    hd = d // MEM_HEADS
    seqs_per_slab = SUBLANES // dec_seq

    @pl.when(step == 0)
    def _():
        qx_ref[...] = jnp.dot(x_ref[...].astype(BF16), wq_ref[...], preferred_element_type=F32)

    n_chunks = hd // LANES
    rows_per_mem = 2 * n_chunks * MEM_HEADS
    n_mem = mkv_ref.shape[2] // rows_per_mem

    def head_matrix(b_local, kv, h):
        chunks = [mkv_ref[0, b_local, pl.ds((kv * n_chunks + c) * MEM_HEADS + h, n_mem, stride=rows_per_mem), :]
                  for c in range(n_chunks)]
        return jnp.concatenate(chunks, axis=1).astype(BF16)

    slab_row = lax.broadcasted_iota(jnp.int32, (SUBLANES, 1), 0)
    slabs = [pl.ds(pl.multiple_of((step * (seqs_per_step // seqs_per_slab) + j) * SUBLANES, SUBLANES), SUBLANES)
             for j in range(seqs_per_step // seqs_per_slab)]
    units = [(j, h, e) for j in range(len(slabs)) for h in range(MEM_HEADS) for e in range(seqs_per_slab)]
    scores = []
    for j, h, e in units:
        if e == 0:
            q8 = qx_ref[slabs[j], h * hd:(h + 1) * hd]
            qh = jnp.concatenate([q8, jnp.zeros_like(q8)], axis=0).astype(BF16)
        scores.append(lax.dot_general(head_matrix(j * seqs_per_slab + e, 0, h), qh, (((1,), (1,)), ((), ())),
                                      preferred_element_type=F32))
    s_all = jnp.stack(scores)
    p_all = jnp.exp(s_all - jnp.max(s_all, axis=1, keepdims=True))
    p_all = (p_all * (1.0 / jnp.sum(p_all, axis=1, keepdims=True))).astype(BF16)
    o8 = None
    for n, (j, h, e) in enumerate(units):
        oh = lax.dot_general(p_all[n], head_matrix(j * seqs_per_slab + e, 1, h), (((0,), (0,)), ((), ())),
                             preferred_element_type=F32)
        o8 = oh[:SUBLANES] if e == 0 else jnp.where(slab_row // dec_seq == e, oh[:SUBLANES], o8)
        if e == seqs_per_slab - 1:
            ox_ref[slabs[j], h * hd:(h + 1) * hd] = o8

    @pl.when(step == pl.num_programs(0) - 1)
    def _():
        y = alpha * x_ref[...] + jnp.dot(ox_ref[...].astype(BF16), wo_ref[...], preferred_element_type=F32)
        out_ref[...] = _layer_norm(y, g_ref[...], b_ref[...])


def _sample_xattn(x, mkv_flat, layer, wq_b, wo_b, g, b, dec_seq, seqs_per_step, alpha):
    rows, d = x.shape
    _, n_seq, flat_rows, _ = mkv_flat.shape
    return pl.pallas_call(
        functools.partial(_sample_xattn_kernel, dec_seq=dec_seq, seqs_per_step=seqs_per_step, alpha=alpha),
        grid=(n_seq // seqs_per_step,),
        in_specs=[_const_spec(x.shape),
                  pl.BlockSpec((1, seqs_per_step, flat_rows, LANES), lambda i: (layer, i, 0, 0)),
                  _layer_spec(wq_b, layer), _layer_spec(wo_b, layer), _layer_spec(g, layer), _layer_spec(b, layer)],
        out_specs=pl.BlockSpec((rows, d), lambda i: (0, 0)),
        out_shape=jax.ShapeDtypeStruct((rows, d), F32),
        scratch_shapes=[pltpu.VMEM((rows, d), F32), pltpu.VMEM((rows, d), F32)],
        compiler_params=_params(1),
        name="sample_xattn",
    )(x, mkv_flat, wq_b, wo_b, g, b)


MLP_SUBTILE = 512


def _mlp_kernel(*refs, ff_chunk, alpha, windows, first_layer):
    n_win = len(windows)
    x_ref, wu_ref, wd_ref, g_ref, b_ref = refs[:5]
    kv_refs = refs[5:5 + n_win]
    out_ref = refs[-1 - n_win]
    window_refs = list(refs[len(refs) - n_win:])

    def write_window(grp):
        win_ref = window_refs[grp] if first_layer is None else window_refs[grp].at[first_layer]
        for part in range(2):
            t = kv_refs[grp][0, :, part * GROUP_WIDTH:(part + 1) * GROUP_WIDTH].T
            for h in range(HEADS_PER_GROUP):
                win_ref[part, h] = t[h * HEAD_DIM:(h + 1) * HEAD_DIM, :]
        if first_layer is not None:
            for other in range(window_refs[grp].shape[0]):
                if other != first_layer:
                    window_refs[grp][other] = jnp.zeros(window_refs[grp].shape[1:], F32)

    for grp, every_tile in enumerate(windows):
        if not every_tile:
            pl.when(pl.program_id(1) == pl.num_programs(1) - 1)(functools.partial(write_window, grp))
    for grp, every_tile in enumerate(windows):
        if every_tile:
            write_window(grp)

    tm = x_ref.shape[1]
    sub = min(tm, MLP_SUBTILE)
    for r0 in range(0, tm, sub):
        x = x_ref[0, r0:r0 + sub, :]
        xb = x.astype(BF16)
        y = alpha * x
        for c in range(wu_ref.shape[1] // ff_chunk):
            h = jnp.dot(xb, wu_ref[:, c * ff_chunk:(c + 1) * ff_chunk], preferred_element_type=F32)
            h = jnp.square(jnp.maximum(h, 0.0)).astype(BF16)
            y = y + jnp.dot(h, wd_ref[c * ff_chunk:(c + 1) * ff_chunk, :], preferred_element_type=F32)
        out_ref[0, r0:r0 + sub, :] = _layer_norm(y, g_ref[...], b_ref[...])


def _mlp(x, wu_b, wd_b, layer, g, b, tm, alpha, ff_chunk=1024, kvs=(), window_keeps=(), window_bufs=None):
    bk, tk, d = x.shape
    depth = wu_b.shape[0]
    row = pl.BlockSpec((1, tm, d), lambda i, j: (i, j, 0))
    aliased = list(window_bufs or ())
    owns_all_layers = bool(window_keeps) and not aliased
    lead, at = (depth, 0) if owns_all_layers else (None, layer)
    kv_specs, out_shape, out_specs, windows = [], [jax.ShapeDtypeStruct((bk, tk, d), F32)], [row], []
    for kv, keep in zip(kvs, window_keeps):
        every_tile = keep == tk
        assert every_tile or tk % keep == 0
        windows.append(every_tile)
        n = tm if every_tile else keep
        kv_specs.append(pl.BlockSpec((1, n, kv.shape[2]),
                                     (lambda i, j: (i, j, 0)) if every_tile else (lambda i, j, last=tk // keep - 1: (i, last, 0))))
        out_shape.append(jax.ShapeDtypeStruct((depth, bk, 2, HEADS_PER_GROUP, HEAD_DIM, keep), F32))
        out_specs.append(pl.BlockSpec((lead, None, 2, HEADS_PER_GROUP, HEAD_DIM, n),
                                      (lambda i, j: (at, i, 0, 0, 0, j)) if every_tile else (lambda i, j: (at, i, 0, 0, 0, 0))))
    n_in = 5 + len(kv_specs)
    return pl.pallas_call(
        functools.partial(_mlp_kernel, ff_chunk=ff_chunk, alpha=alpha, windows=tuple(windows),
                          first_layer=layer if owns_all_layers else None),
        grid=(bk, tk // tm),
        in_specs=[row, _layer_spec(wu_b, layer), _layer_spec(wd_b, layer), _layer_spec(g, layer), _layer_spec(b, layer)]
        + kv_specs + [pl.BlockSpec(memory_space=pl.ANY)] * len(aliased),
        out_specs=out_specs,
        out_shape=out_shape,
        input_output_aliases={n_in + k: 1 + k for k in range(len(aliased))},
        compiler_params=_params(2),
        name="mlp",
    )(x, wu_b, wd_b, g, b, *kvs, *aliased)


def kernel(x_prompt, x_sample, cache_kv_w128, cache_kv_w512, cache_kv_w2048, cache_mem_kv, mem_prompt,
           w_in, sgu_ln_g, sgu_ln_b, w_spatial, b_spatial, w_mix_out, ln1_g, ln1_b,
           w_xq, w_xkv, w_xo, ln2_g, ln2_b, w_up, w_down, ln3_g, ln3_b):
    depth = w_in.shape[0]
    bp, tp, d = x_prompt.shape
    bs, ts, _ = x_sample.shape
    past_len = 8192
    alpha = float((2 * depth) ** 0.25)
    att_width = len(DILATIONS) * GROUP_WIDTH
    sgu_width = sgu_ln_g.shape[1]
    sgu_groups = w_spatial.shape[1]
    n_mem = mem_prompt.shape[1]
    rows_s = bs * ts
    assert tp % (DILATIONS[-1] * WINDOW_STEPS) == 0 and SUBLANES % ts == 0 and rows_s % SGU_CHUNK == 0
    assert d // MEM_HEADS == GROUP_WIDTH and sgu_width == GROUP_WIDTH

    col_scale = jnp.concatenate([jnp.full((att_width,), HEAD_DIM ** -0.5 * math.log2(math.e), F32),
                                 jnp.ones((w_in.shape[2] - att_width,), F32)])
    w_in_b = (w_in * col_scale).astype(BF16)
    w_xq_b = (w_xq * (d // MEM_HEADS) ** -0.5).astype(BF16)
    w_mix_b, w_xkv_b, w_xo_b = w_mix_out.astype(BF16), w_xkv.astype(BF16), w_xo.astype(BF16)
    w_up_b, w_down_b = w_up.astype(BF16), w_down.astype(BF16)
    vec = lambda p: p[:, None, :]
    sgu_g, sgu_b, g1, b1, g2, b2, g3, b3 = map(vec, (sgu_ln_g, sgu_ln_b, ln1_g, ln1_b, ln2_g, ln2_b, ln3_g, ln3_b))

    tables_p = _rope_tables(jnp.arange(tp, dtype=jnp.int32))
    tables_s = _rope_tables(past_len + (jnp.arange(rows_s, dtype=jnp.int32) % ts))

    pos_s = jnp.arange(rows_s, dtype=jnp.int32) % ts
    seq_s = jnp.arange(rows_s, dtype=jnp.int32) // ts
    same_seq = (seq_s[:, None] == seq_s[None, :]).astype(F32)
    w_sp_sample = w_spatial[:, :, :ts, :ts][:, :, pos_s[:, None], pos_s[None, :]] * same_seq

    hp = x_prompt
    hs = x_sample.reshape(1, rows_s, d)
    out_rows_s = [[] for _ in DILATIONS]
    out_gv = []
    window_keeps = tuple(min(dil * WINDOW_STEPS, tp) for dil in DILATIONS)
    windows = None
    mem_out, mkv = _memkv(mem_prompt, w_xkv_b)
    caches_t = [jnp.transpose(c, (0, 1, 3, 4, 5, 2)) for c in (cache_kv_w128, cache_kv_w512, cache_kv_w2048)]
    mem_chunks = d // MEM_HEADS // LANES
    mem_flat = (cache_mem_kv.reshape(depth, bs, n_mem, 2, MEM_HEADS, mem_chunks, LANES)
                .transpose(0, 1, 2, 3, 5, 4, 6).reshape(depth, bs, n_mem * 2 * mem_chunks * MEM_HEADS, LANES))
    for l in range(depth):
        q, kv0, kv1, kv2, u, gv = _inproj(hp, w_in_b, l, tables_p, sgu_g, sgu_b, tm=512, gate_dtype=BF16)
        atts = _prompt_attention(q, [kv0, kv1, kv2])
        b_tile = jnp.repeat(b_spatial[l][:, :SGU_CHUNK].T, sgu_width // sgu_groups, axis=1)
        hp = _mix(atts, u, gv, hp, w_spatial[l], b_tile, w_mix_b, l, g1, b1, tm=512, alpha=alpha,
                  cross=(mkv, w_xq_b, w_xo_b, g2, b2))
        hp, *windows = _mlp(hp, w_up_b, w_down_b, l, g3, b3, tm=512, alpha=alpha,
                            kvs=(kv0, kv1, kv2), window_keeps=window_keeps, window_bufs=windows)

        q, kv0, kv1, kv2, u, gv = _inproj(hs, w_in_b, l, tables_s, sgu_g, sgu_b, tm=rows_s, gate_dtype=F32)
        out_gv.append(gv.reshape(bs, ts, sgu_width))
        for g, kv in enumerate((kv0, kv1, kv2)):
            out_rows_s[g].append(kv.reshape(bs, ts, 2, HEADS_PER_GROUP, HEAD_DIM))
        att = _sample_attention(q[0], (kv0, kv1, kv2), caches_t, l, dec_seq=ts)
        b_tile_s = jnp.repeat(jnp.tile(b_spatial[l][:, :ts].T, (rows_s // ts, 1)), sgu_width // sgu_groups, axis=1)
        hs = _mix([att[None]], u, gv, hs, w_sp_sample[l], b_tile_s, w_mix_b, l, g1, b1, tm=rows_s, alpha=alpha)
        hs2 = _sample_xattn(hs[0], mem_flat, l, w_xq_b, w_xo_b, g2, b2, dec_seq=ts, seqs_per_step=4, alpha=alpha)
        hs, = _mlp(hs2[None], w_up_b, w_down_b, l, g3, b3, tm=rows_s, alpha=alpha)

    rows_p = [jnp.transpose(w, (0, 1, 5, 2, 3, 4)) for w in windows]
    mem_p = (mem_out.reshape(depth, bp, n_mem, 2, mem_chunks, MEM_HEADS, LANES)
             .transpose(0, 1, 2, 3, 5, 4, 6).reshape(depth, bp, n_mem, 2, MEM_HEADS, d // MEM_HEADS))
    stack = lambda xs: jnp.stack(xs)
    return (hp, hs.reshape(bs, ts, d), rows_p[0], rows_p[1], rows_p[2], mem_p,
            stack(out_rows_s[0]), stack(out_rows_s[1]), stack(out_rows_s[2]), stack(out_gv))
```

```python
import functools
import math

import jax
import jax.numpy as jnp
from jax import lax
from jax.experimental import pallas as pl
from jax.experimental.pallas import tpu as pltpu

F32 = jnp.float32
BF16 = jnp.bfloat16

HEAD_DIM = 64
HEADS_PER_GROUP = 4
GROUP_WIDTH = HEAD_DIM * HEADS_PER_GROUP
DILATIONS = (1, 4, 16)
WINDOW_STEPS = 128
ROT_DIM = 16
ROPE_THETA = 500000.0
SGU_CHUNK = 128
MEM_HEADS = 4
LN_EPS = 1e-5
NEG = -1e30
LANES = 128
SUBLANES = 8
VMEM_LIMIT = 56 * 1024 * 1024


def _params(n_grid_dims):
    return pltpu.CompilerParams(
        dimension_semantics=("arbitrary",) * n_grid_dims,
        vmem_limit_bytes=VMEM_LIMIT)


def _const_spec(shape):
    nd = len(shape)
    return pl.BlockSpec(shape, lambda *_: (0,) * nd, pipeline_mode=pl.Buffered(1))


def _layer_spec(stacked, layer):
    rest = stacked.shape[1:]
    return pl.BlockSpec((None,) + rest, lambda *_: (layer,) + (0,) * len(rest), pipeline_mode=pl.Buffered(1))


def _layer_norm(y, g, b):
    mu = jnp.mean(y, axis=-1, keepdims=True)
    yc = y - mu
    var = jnp.mean(yc * yc, axis=-1, keepdims=True)
    return yc * lax.rsqrt(var + LN_EPS) * g + b


def _gelu_tanh(x):
    return 0.5 * x * (1.0 + jnp.tanh(0.7978845608028654 * (x + 0.044715 * (x * x * x))))


def _head_half_mask(width, half):
    lane = lax.broadcasted_iota(jnp.int32, (1, width), 1)
    return (lane // HEAD_DIM) % 2 == half


def _inproj_kernel(x_ref, w_ref, cos_ref, sin_lo_ref, sin_hi_ref, g_ref, b_ref,
                   q_ref, kv0_ref, kv1_ref, kv2_ref, u_ref, gv_ref, *, att_width, sgu_width):
    xb = x_ref[0].astype(BF16)
    cos = cos_ref[...]
    sin_lo = sin_lo_ref[...]
    sin_hi = sin_hi_ref[...]

    def proj(c0, width):
        return jnp.dot(xb, w_ref[:, c0:c0 + width], preferred_element_type=F32)

    def rope(t):
        return t * cos + pltpu.roll(t, LANES - ROT_DIM // 2, 1) * sin_lo + pltpu.roll(t, ROT_DIM // 2, 1) * sin_hi

    def rope_group(t):
        return jnp.concatenate([rope(t[:, s:s + LANES]) for s in range(0, GROUP_WIDTH, LANES)], axis=1)

    u_ref[0] = _gelu_tanh(proj(3 * att_width, sgu_width)).astype(u_ref.dtype)
    gate = _gelu_tanh(proj(3 * att_width + sgu_width, sgu_width))
    gv_ref[0] = _layer_norm(gate, g_ref[...], b_ref[...]).astype(gv_ref.dtype)
    kv_refs = (kv0_ref, kv1_ref, kv2_ref)
    for grp, kv_ref in enumerate(kv_refs):
        c = grp * GROUP_WIDTH
        q_ref[0, :, c:c + GROUP_WIDTH] = rope_group(proj(c, GROUP_WIDTH))
        kv_ref[0, :, 0:GROUP_WIDTH] = rope_group(proj(att_width + c, GROUP_WIDTH))
    for grp, kv_ref in enumerate(kv_refs):
        kv_ref[0, :, GROUP_WIDTH:2 * GROUP_WIDTH] = proj(2 * att_width + grp * GROUP_WIDTH, GROUP_WIDTH)


def _inproj(x, w_in_b, layer, tables, g, b, tm, gate_dtype):
    bk, tk, d = x.shape
    att_width = len(DILATIONS) * GROUP_WIDTH
    sgu_width = (w_in_b.shape[2] - 3 * att_width) // 2
    cos, sin_lo, sin_hi = tables
    row = lambda w: pl.BlockSpec((1, tm, w), lambda i, j: (i, j, 0))
    tab = pl.BlockSpec((tm, LANES), lambda i, j: (j, 0))
    out_shape = ([jax.ShapeDtypeStruct((bk, tk, w), F32) for w in (att_width,) + (2 * GROUP_WIDTH,) * len(DILATIONS)]
                 + [jax.ShapeDtypeStruct((bk, tk, sgu_width), gate_dtype)] * 2)
    return pl.pallas_call(
        functools.partial(_inproj_kernel, att_width=att_width, sgu_width=sgu_width),
        grid=(bk, tk // tm),
        in_specs=[row(d), _layer_spec(w_in_b, layer), tab, tab, tab, _layer_spec(g, layer), _layer_spec(b, layer)],
        out_specs=[row(s.shape[2]) for s in out_shape],
        out_shape=out_shape,
        compiler_params=_params(2),
        name="inproj",
    )(x, w_in_b, cos, sin_lo, sin_hi, g, b)


def _rope_tables(pos):
    half = ROT_DIM // 2
    inv = ROPE_THETA ** (-jnp.arange(half, dtype=F32) / half)
    ang = pos.astype(F32)[:, None] * inv[None, :]
    cos, sin = jnp.cos(ang), jnp.sin(ang)
    zeros = jnp.zeros((pos.shape[0], HEAD_DIM - ROT_DIM), F32)
    zero_half = jnp.zeros_like(sin)
    cos_head = jnp.concatenate([cos, cos, zeros + 1.0], axis=1)
    lo_head = jnp.concatenate([-sin, zero_half, zeros], axis=1)
    hi_head = jnp.concatenate([zero_half, sin, zeros], axis=1)
    two = lambda t: jnp.concatenate([t, t], axis=1)
    return two(cos_head), two(lo_head), two(hi_head)


ATTN_MIX_ROWS = 256
ATTN_TILES_PER_BATCH = 32


def _attn_kernel(*refs, seq):
    n_groups = len(DILATIONS)
    q_refs, k_refs, v_refs = refs[0:n_groups], refs[n_groups:2 * n_groups], refs[2 * n_groups:3 * n_groups]
    att_refs = refs[3 * n_groups:4 * n_groups]
    o_s, m_s, den_s = refs[4 * n_groups:]
    blk = WINDOW_STEPS
    x_idx = lax.broadcasted_iota(jnp.int32, (blk, 2 * blk), 0)
    k_idx = lax.broadcasted_iota(jnp.int32, (blk, 2 * blk), 1)
    band_mask = (k_idx >= x_idx) & (k_idx <= x_idx + blk)
    causal_mask = (lax.broadcasted_iota(jnp.int32, (blk, blk), 1)
                   <= lax.broadcasted_iota(jnp.int32, (blk, blk), 0))
    half_masks = [_head_half_mask(LANES, hh) for hh in range(2)]

    def batch(g, dil, blocks):
        def rows(r, first_block, n):
            start = r + dil * blk * first_block
            return pl.ds(start, n) if dil == 1 else pl.ds(start, n, stride=dil)

        first = blocks[0][1] == 0
        mask = causal_mask if first else band_mask
        scores, values = [], []
        for r, c in blocks:
            k_rows = rows(r, 0, blk) if first else rows(r, c - 1, 2 * blk)
            qp = q_refs[g][0, rows(r, c, blk), :]
            kp = k_refs[g][0, k_rows, :].astype(BF16)
            values.append(v_refs[g][0, k_rows, :].astype(BF16))
            for hh in range(2):
                qh = jnp.where(half_masks[hh], qp, 0.0).astype(BF16)
                s = lax.dot_general(qh, kp, (((1,), (1,)), ((), ())), preferred_element_type=F32)
                scores.append(jnp.where(mask, s, NEG))
        s_all = jnp.concatenate(scores, axis=0)
        m = jnp.max(s_all, axis=-1, keepdims=True)
        p = jnp.exp2(s_all - m)
        den = jnp.sum(p, axis=-1, keepdims=True)
        p = p.astype(BF16)
        for n, (r, c) in enumerate(blocks):
            piece = lambda t, hh: t[(2 * n + hh) * blk:(2 * n + hh + 1) * blk]
            outs = [jnp.dot(piece(p, hh), values[n], preferred_element_type=F32) for hh in range(2)]
            o_s[g, rows(r, c, blk), :] = jnp.where(half_masks[0], outs[0], outs[1])
            m_s[g, rows(r, c, blk), :] = jnp.where(half_masks[0], piece(m, 0), piece(m, 1))
            den_s[g, rows(r, c, blk), :] = jnp.where(half_masks[0], piece(den, 0), piece(den, 1))

    for g, dil in enumerate(DILATIONS):
        n_blocks = seq // dil // blk
        first_blocks = [(r, 0) for r in range(dil)]
        later_blocks = [(r, c) for r in range(dil) for c in range(1, n_blocks)]
        per_first = ATTN_TILES_PER_BATCH // 2
        per_later = ATTN_TILES_PER_BATCH // 4
        for i in range(0, len(first_blocks), per_first):
            batch(g, dil, first_blocks[i:i + per_first])
        for i in range(0, len(later_blocks), per_later):
            batch(g, dil, later_blocks[i:i + per_later])

    for t0 in range(0, seq, ATTN_MIX_ROWS):
        rws = slice(t0, t0 + ATTN_MIX_ROWS)
        tops = [m_s[g, rws, :] for g in range(n_groups)]
        peak = functools.reduce(jnp.maximum, tops)
        scales = [jnp.exp2(t - peak) for t in tops]
        inv = 1.0 / functools.reduce(jnp.add, [den_s[g, rws, :] * scales[g] for g in range(n_groups)])
        for g in range(n_groups):
            att_refs[g][0, rws, :] = (o_s[g, rws, :] * (scales[g] * inv)).astype(att_refs[g].dtype)


def _prompt_attention(q, kvs):
    bk, tk, _ = q.shape
    n_groups = len(DILATIONS)
    pairs = GROUP_WIDTH // LANES
    col = lambda first: pl.BlockSpec((1, tk, LANES), lambda i, j: (i, 0, first + j))
    out = jax.ShapeDtypeStruct((bk, tk, GROUP_WIDTH), BF16)
    return pl.pallas_call(
        functools.partial(_attn_kernel, seq=tk),
        grid=(bk, pairs),
        in_specs=([col(g * pairs) for g in range(n_groups)]
                  + [col(0)] * n_groups
                  + [col(pairs)] * n_groups),
        out_specs=[col(0)] * n_groups,
        out_shape=[out] * n_groups,
        scratch_shapes=[pltpu.VMEM((n_groups, tk, LANES), F32)] * 3,
        compiler_params=_params(2),
        name="attn",
    )(*([q] * n_groups), *kvs, *kvs)


def _sample_attn_kernel(q_ref, kvn0_ref, kvn1_ref, kvn2_ref, c0_ref, c1_ref, c2_ref, att_ref, *, dec_seq):
    kvn_refs = (kvn0_ref, kvn1_ref, kvn2_ref)
    cache_refs = (c0_ref, c1_ref, c2_ref)
    seqs_per_slab = SUBLANES // dec_seq
    n_groups = len(DILATIONS)
    widest = DILATIONS[-1]
    slab = pl.ds(pl.multiple_of(pl.program_id(0) * SUBLANES, SUBLANES), SUBLANES)
    lane = lax.broadcasted_iota(jnp.int32, (1, LANES), 1)
    row8 = lax.broadcasted_iota(jnp.int32, (SUBLANES, LANES), 0)
    sel_rows = 2 * SUBLANES
    sel_row = lax.broadcasted_iota(jnp.int32, (sel_rows, LANES), 0)
    sel_lane = lax.broadcasted_iota(jnp.int32, (sel_rows, LANES), 1)
    query_class = lane % widest

    def one_hot(cond):
        return jnp.where(cond, 1.0, 0.0).astype(BF16)

    def spread(rows8, sel):
        padded = jnp.concatenate([rows8, jnp.zeros_like(rows8)], axis=0).astype(BF16)
        return lax.dot_general(padded, sel, (((0,), (0,)), ((), ())), preferred_element_type=F32)

    def class_fold(x, period, op):
        shift = LANES // 2
        while shift >= period:
            x = op(x, pltpu.roll(x, shift, 1))
            shift //= 2
        return x

    def valid_mask(n_tiles, cache_ok, new_ok):
        rows = SUBLANES * (n_tiles // SUBLANES + 1)
        r = lax.broadcasted_iota(jnp.int32, (rows, LANES), 0)
        p = lax.broadcasted_iota(jnp.int32, (rows, LANES), 1)
        return ((r < n_tiles) & cache_ok(p)) | ((r == n_tiles) & new_ok(p))

    def attend(qm, cache_ref, seq, head, knt, vnt, n_tiles, valid, period):
        groups = []
        for t0 in range(0, n_tiles + 1, SUBLANES):
            acc = jnp.full((SUBLANES, LANES), NEG, F32)
            for t in range(t0, min(t0 + SUBLANES, n_tiles + 1)):
                keys = knt if t == n_tiles else cache_ref[0, seq, 0, head, :, t * LANES:(t + 1) * LANES]
                acc = jnp.where(row8 == t - t0, jnp.sum(qm * keys, axis=0, keepdims=True), acc)
            groups.append(acc)
        scores = jnp.where(valid, groups[0] if len(groups) == 1 else jnp.concatenate(groups, axis=0), NEG)
        top = jnp.broadcast_to(jnp.max(scores, axis=0, keepdims=True), (SUBLANES, LANES))
        if period:
            top = class_fold(top, period, jnp.maximum)
        else:
            top = jnp.broadcast_to(jnp.max(top, axis=1, keepdims=True), (SUBLANES, LANES))
        probs = jnp.exp2(scores - jnp.concatenate([top] * len(groups), axis=0))
        den = jnp.broadcast_to(jnp.sum(probs, axis=0, keepdims=True), (SUBLANES, LANES))
        num = jnp.zeros((HEAD_DIM, LANES), F32)
        for t in range(n_tiles + 1):
            vals = vnt if t == n_tiles else cache_ref[0, seq, 1, head, :, t * LANES:(t + 1) * LANES]
            num = num + vals * probs[t:t + 1, :]
        if period:
            return class_fold(num, period, jnp.add), top, class_fold(den, period, jnp.add)
        return (jnp.broadcast_to(jnp.sum(num, axis=1, keepdims=True), (HEAD_DIM, LANES)), top,
                jnp.broadcast_to(jnp.sum(den, axis=1, keepdims=True), (SUBLANES, LANES)))

    for pair in range(GROUP_WIDTH // LANES):
        mixed = [[None] * seqs_per_slab for _ in range(n_groups)]
        for e in range(seqs_per_slab):
            base = e * dec_seq
            sel_new = one_hot((sel_row == base + sel_lane) & (sel_lane < dec_seq))
            per_group = []
            for g, dil in enumerate(DILATIONS):
                off = pair * LANES
                knt = spread(kvn_refs[g][0, slab, off:off + LANES], sel_new)
                vnt = spread(kvn_refs[g][0, slab, GROUP_WIDTH + off:GROUP_WIDTH + off + LANES], sel_new)
                q8 = q_ref[slab, g * GROUP_WIDTH + off:g * GROUP_WIDTH + off + LANES]
                n_tiles = cache_refs[g].shape[5] // LANES
                if dil == 1:
                    qms = [spread(q8, one_hot(sel_row == base + i)) for i in range(dec_seq)]
                    valids = [valid_mask(n_tiles, lambda p, i=i: p >= i, lambda p, i=i: p <= i) for i in range(dec_seq)]
                else:
                    qms = [spread(q8, one_hot((sel_row == base + sel_lane % dil) & (sel_lane % dil < dec_seq)))]
                    valids = [valid_mask(n_tiles, lambda p, dil=dil: p % dil < dec_seq, lambda p: p < dec_seq)]
                heads = []
                for hh in range(2):
                    hrows = slice(hh * HEAD_DIM, (hh + 1) * HEAD_DIM)
                    num = top = den = None
                    for i, (qm, valid) in enumerate(zip(qms, valids)):
                        n_i, t_i, d_i = attend(qm[hrows], cache_refs[g], e, 2 * pair + hh, knt[hrows], vnt[hrows],
                                               n_tiles, valid, dil if dil > 1 else 0)
                        if num is None:
                            num, top, den = n_i, t_i, d_i
                        else:
                            mine = query_class == i
                            num, top, den = jnp.where(mine, n_i, num), jnp.where(mine, t_i, top), jnp.where(mine, d_i, den)
                    heads.append([num, top, den])
                per_group.append(heads)
            for hh in range(2):
                tops = [per_group[g][hh][1] for g in range(n_groups)]
                peak = functools.reduce(jnp.maximum, tops)
                scales = [jnp.exp2(t - peak) for t in tops]
                total = functools.reduce(jnp.add, [per_group[g][hh][2] * scales[g] for g in range(n_groups)])
                for g in range(n_groups):
                    per_group[g][hh] = per_group[g][hh][0] * (scales[g] / total)[0:1, :]
            for g in range(n_groups):
                mixed[g][e] = jnp.concatenate(per_group[g], axis=0)
        sel_out = one_hot(sel_lane == widest * (sel_row // dec_seq) + sel_row % dec_seq)
        for g in range(n_groups):
            both = mixed[g][0]
            for e in range(1, seqs_per_slab):
                both = jnp.where((lane >= widest * e) & (lane < widest * (e + 1)), mixed[g][e], both)
            hi = both.astype(BF16)
            lo = (both - hi.astype(F32)).astype(BF16)
            dims = (((1,), (1,)), ((), ()))
            rows = (lax.dot_general(sel_out, hi, dims, preferred_element_type=F32)
                    + lax.dot_general(sel_out, lo, dims, preferred_element_type=F32))
            att_ref[slab, g * GROUP_WIDTH + pair * LANES:g * GROUP_WIDTH + (pair + 1) * LANES] = rows[:SUBLANES]


def _sample_attention(q, kvns, caches_t, layer, dec_seq):
    rows, width = q.shape
    seqs_per_slab = SUBLANES // dec_seq
    cache_spec = lambda c: pl.BlockSpec((1, seqs_per_slab) + c.shape[2:], lambda i: (layer, i, 0, 0, 0, 0))
    return pl.pallas_call(
        functools.partial(_sample_attn_kernel, dec_seq=dec_seq),
        grid=(rows // SUBLANES,),
        in_specs=[_const_spec(q.shape)] + [_const_spec(k.shape) for k in kvns] + [cache_spec(c) for c in caches_t],
        out_specs=pl.BlockSpec((rows, width), lambda i: (0, 0)),
        out_shape=jax.ShapeDtypeStruct((rows, width), F32),
        compiler_params=_params(1),
        name="sample_attn",
    )(q, *kvns, *caches_t)


MIX_SUBTILE = 512


def _mix_kernel(*refs, tm, alpha, n_att, cross_attend):
    u_ref, gv_ref, x_ref, wsp_ref, bsp_ref, wm_ref, g_ref, b_ref = refs[n_att:n_att + 8]
    xattn_refs = refs[n_att + 8:-1]
    out_ref = refs[-1]
    t_idx = lax.broadcasted_iota(jnp.int32, (SGU_CHUNK, SGU_CHUNK), 0)
    s_idx = lax.broadcasted_iota(jnp.int32, (SGU_CHUNK, SGU_CHUNK), 1)
    w_sp = [jnp.where(s_idx <= t_idx, wsp_ref[k], 0.0).astype(BF16) for k in range(wsp_ref.shape[0])]
    sgu_width = gv_ref.shape[2]

    def gated_chunk(crow):
        slabs = []
        for pair in range(sgu_width // LANES):
            gp = gv_ref[0, crow, pair * LANES:(pair + 1) * LANES]
            mixed = jnp.zeros((SGU_CHUNK, LANES), F32)
            for hh in range(2):
                gm = jnp.where(_head_half_mask(LANES, hh), gp, 0.0).astype(BF16)
                mixed = mixed + jnp.dot(w_sp[2 * pair + hh], gm, preferred_element_type=F32)
            slabs.append(mixed)
        return (u_ref[0, crow, :] * (jnp.concatenate(slabs, axis=1) + bsp_ref[...])).astype(BF16)

    sub = min(tm, MIX_SUBTILE)
    for r0 in range(0, tm, sub):
        rws = slice(r0, r0 + sub)
        sgu = jnp.concatenate([gated_chunk(slice(c, c + SGU_CHUNK)) for c in range(r0, r0 + sub, SGU_CHUNK)], axis=0)
        lhs = jnp.concatenate([a[0, rws, :].astype(BF16) for a in refs[:n_att]] + [sgu], axis=1)
        y = alpha * x_ref[0, rws, :] + jnp.dot(lhs, wm_ref[...], preferred_element_type=F32)
        x1 = _layer_norm(y, g_ref[...], b_ref[...])
        out_ref[0, rws, :] = _xattn_rows(x1, *xattn_refs, alpha) if cross_attend else x1


def _mix(att_inputs, u, gv, x, w_sp, b_sp_tile, w_mix_b, layer, g, b, tm, alpha, cross=None):
    bk, tk, d = x.shape
    sgu_width = u.shape[2]
    row = lambda w: pl.BlockSpec((1, tm, w), lambda i, j: (i, j, 0))
    cross_specs = []
    if cross:
        mkv = cross[0]
        cross_specs = ([pl.BlockSpec((None, 1) + mkv.shape[2:], lambda i, j: (layer, i, 0, 0))]
                       + [_layer_spec(p, layer) for p in cross[1:]])
    return pl.pallas_call(
        functools.partial(_mix_kernel, tm=tm, alpha=alpha, n_att=len(att_inputs), cross_attend=bool(cross)),
        grid=(bk, tk // tm),
        in_specs=[row(a.shape[2]) for a in att_inputs] + [
            row(sgu_width), row(sgu_width), row(d),
            _const_spec(w_sp.shape), _const_spec(b_sp_tile.shape), _layer_spec(w_mix_b, layer),
            _layer_spec(g, layer), _layer_spec(b, layer)] + cross_specs,
        out_specs=row(d),
        out_shape=jax.ShapeDtypeStruct((bk, tk, d), F32),
        compiler_params=_params(2),
        name="mix_xattn" if cross else "mix",
    )(*att_inputs, u, gv, x, w_sp, b_sp_tile, w_mix_b, g, b, *(cross or ()))


def _mem_row_stride(d):
    return 2 * (d // MEM_HEADS // LANES) * MEM_HEADS


def _memkv_kernel(m_ref, w_ref, flat_ref, out_b_ref):
    depth, _, n_mem, width = out_b_ref.shape
    d = width // 2
    hd = d // MEM_HEADS
    n_chunks = hd // LANES
    mb = m_ref[0].astype(BF16)
    for layer in range(depth):
        mkv = jnp.dot(mb, w_ref[layer], preferred_element_type=F32)
        out_b_ref[layer, 0] = mkv.astype(BF16)
        for kv in range(2):
            for c in range(n_chunks):
                for h in range(MEM_HEADS):
                    col = kv * d + h * hd + c * LANES
                    rows = pl.ds((kv * n_chunks + c) * MEM_HEADS + h, n_mem, stride=_mem_row_stride(d))
                    flat_ref[layer, rows, :] = mkv[:, col:col + LANES]


def _memkv(mem, w_xkv_b):
    bk, n_mem, d = mem.shape
    depth, _, width = w_xkv_b.shape
    flat_rows = n_mem * _mem_row_stride(d)
    return pl.pallas_call(
        _memkv_kernel,
        grid=(bk,),
        in_specs=[pl.BlockSpec((1, n_mem, d), lambda i: (i, 0, 0)), _const_spec(w_xkv_b.shape)],
        out_specs=[pl.BlockSpec((depth, None, flat_rows, LANES), lambda i: (0, i, 0, 0)),
                   pl.BlockSpec((depth, 1, n_mem, width), lambda i: (0, i, 0, 0))],
        out_shape=[jax.ShapeDtypeStruct((depth, bk, flat_rows, LANES), F32),
                   jax.ShapeDtypeStruct((depth, bk, n_mem, width), BF16)],
        compiler_params=_params(1),
        name="memkv",
    )(mem, w_xkv_b)


def _softmax_rows(s):
    m = jnp.max(s, axis=-1, keepdims=True)
    p = jnp.exp(s - m)
    return p * (1.0 / jnp.sum(p, axis=-1, keepdims=True))


def _xattn_rows(x, mkv_ref, wq_ref, wo_ref, g_ref, b_ref, alpha):
    d = x.shape[1]
    hd = d // MEM_HEADS
    qx = jnp.dot(x.astype(BF16), wq_ref[...], preferred_element_type=F32)
    heads = []
    for h in range(MEM_HEADS):
        qh = qx[:, h * hd:(h + 1) * hd].astype(BF16)
        kh = mkv_ref[0, :, h * hd:(h + 1) * hd]
        vh = mkv_ref[0, :, d + h * hd:d + (h + 1) * hd]
        s = lax.dot_general(qh, kh, (((1,), (1,)), ((), ())), preferred_element_type=F32)
        m = jnp.max(s, axis=-1, keepdims=True)
        p = jnp.exp(s - m)
        inv = 1.0 / jnp.sum(p, axis=-1, keepdims=True)
        heads.append((jnp.dot(p.astype(BF16), vh, preferred_element_type=F32) * inv).astype(BF16))
    y = alpha * x + jnp.dot(jnp.concatenate(heads, axis=1), wo_ref[...], preferred_element_type=F32)
    return _layer_norm(y, g_ref[...], b_ref[...])


def _sample_xattn_kernel(x_ref, mkv_ref, wq_ref, wo_ref, g_ref, b_ref, out_ref, qx_ref, ox_ref,
                         *, dec_seq, seqs_per_step, alpha):
    step = pl.program_id(0)
    d = x_ref.shape[1]
    hd = d // MEM_HEADS
    seqs_per_slab = SUBLANES // dec_seq

    @pl.when(step == 0)
    def _():
        qx_ref[...] = jnp.dot(x_ref[...].astype(BF16), wq_ref[...], preferred_element_type=F32)

    n_chunks = hd // LANES
    rows_per_mem = 2 * n_chunks * MEM_HEADS
    n_mem = mkv_ref.shape[2] // rows_per_mem

    def head_matrix(b_local, kv, h):
        chunks = [mkv_ref[0, b_local, pl.ds((kv * n_chunks + c) * MEM_HEADS + h, n_mem, stride=rows_per_mem), :]
                  for c in range(n_chunks)]
        return jnp.concatenate(chunks, axis=1).astype(BF16)

    slab_row = lax.broadcasted_iota(jnp.int32, (SUBLANES, 1), 0)
    slabs = [pl.ds(pl.multiple_of((step * (seqs_per_step // seqs_per_slab) + j) * SUBLANES, SUBLANES), SUBLANES)
             for j in range(seqs_per_step // seqs_per_slab)]
    units = [(j, h, e) for j in range(len(slabs)) for h in range(MEM_HEADS) for e in range(seqs_per_slab)]
    scores = []
    for j, h, e in units:
        if e == 0:
            q8 = qx_ref[slabs[j], h * hd:(h + 1) * hd]
            qh = jnp.concatenate([q8, jnp.zeros_like(q8)], axis=0).astype(BF16)
        scores.append(lax.dot_general(head_matrix(j * seqs_per_slab + e, 0, h), qh, (((1,), (1,)), ((), ())),
                                      preferred_element_type=F32))
    s_all = jnp.stack(scores)
    p_all = jnp.exp(s_all - jnp.max(s_all, axis=1, keepdims=True))
    p_all = (p_all * (1.0 / jnp.sum(p_all, axis=1, keepdims=True))).astype(BF16)
    o8 = None
    for n, (j, h, e) in enumerate(units):
        oh = lax.dot_general(p_all[n], head_matrix(j * seqs_per_slab + e, 1, h), (((0,), (0,)), ((), ())),
                             preferred_element_type=F32)
        o8 = oh[:SUBLANES] if e == 0 else jnp.where(slab_row // dec_seq == e, oh[:SUBLANES], o8)
        if e == seqs_per_slab - 1:
            ox_ref[slabs[j], h * hd:(h + 1) * hd] = o8

    @pl.when(step == pl.num_programs(0) - 1)
    def _():
        y = alpha * x_ref[...] + jnp.dot(ox_ref[...].astype(BF16), wo_ref[...], preferred_element_type=F32)
        out_ref[...] = _layer_norm(y, g_ref[...], b_ref[...])


def _sample_xattn(x, mkv_flat, layer, wq_b, wo_b, g, b, dec_seq, seqs_per_step, alpha):
    rows, d = x.shape
    _, n_seq, flat_rows, _ = mkv_flat.shape
    return pl.pallas_call(
        functools.partial(_sample_xattn_kernel, dec_seq=dec_seq, seqs_per_step=seqs_per_step, alpha=alpha),
        grid=(n_seq // seqs_per_step,),
        in_specs=[_const_spec(x.shape),
                  pl.BlockSpec((1, seqs_per_step, flat_rows, LANES), lambda i: (layer, i, 0, 0)),
                  _layer_spec(wq_b, layer), _layer_spec(wo_b, layer), _layer_spec(g, layer), _layer_spec(b, layer)],
        out_specs=pl.BlockSpec((rows, d), lambda i: (0, 0)),
        out_shape=jax.ShapeDtypeStruct((rows, d), F32),
        scratch_shapes=[pltpu.VMEM((rows, d), F32), pltpu.VMEM((rows, d), F32)],
        compiler_params=_params(1),
        name="sample_xattn",
    )(x, mkv_flat, wq_b, wo_b, g, b)


MLP_SUBTILE = 512


def _mlp_kernel(*refs, ff_chunk, alpha, windows, first_layer):
    n_win = len(windows)
    x_ref, wu_ref, wd_ref, g_ref, b_ref = refs[:5]
    kv_refs = refs[5:5 + n_win]
    out_ref = refs[-1 - n_win]
    window_refs = list(refs[len(refs) - n_win:])

    def write_window(grp):
        win_ref = window_refs[grp] if first_layer is None else window_refs[grp].at[first_layer]
        for part in range(2):
            t = kv_refs[grp][0, :, part * GROUP_WIDTH:(part + 1) * GROUP_WIDTH].T
            for h in range(HEADS_PER_GROUP):
                win_ref[part, h] = t[h * HEAD_DIM:(h + 1) * HEAD_DIM, :]
        if first_layer is not None:
            for other in range(window_refs[grp].shape[0]):
                if other != first_layer:
                    window_refs[grp][other] = jnp.zeros(window_refs[grp].shape[1:], F32)

    for grp, every_tile in enumerate(windows):
        if not every_tile:
            pl.when(pl.program_id(1) == pl.num_programs(1) - 1)(functools.partial(write_window, grp))
    for grp, every_tile in enumerate(windows):
        if every_tile:
            write_window(grp)

    tm = x_ref.shape[1]
    sub = min(tm, MLP_SUBTILE)
    for r0 in range(0, tm, sub):
        x = x_ref[0, r0:r0 + sub, :]
        xb = x.astype(BF16)
        y = alpha * x
        for c in range(wu_ref.shape[1] // ff_chunk):
            h = jnp.dot(xb, wu_ref[:, c * ff_chunk:(c + 1) * ff_chunk], preferred_element_type=F32)
            h = jnp.square(jnp.maximum(h, 0.0)).astype(BF16)
            y = y + jnp.dot(h, wd_ref[c * ff_chunk:(c + 1) * ff_chunk, :], preferred_element_type=F32)
        out_ref[0, r0:r0 + sub, :] = _layer_norm(y, g_ref[...], b_ref[...])


def _mlp(x, wu_b, wd_b, layer, g, b, tm, alpha, ff_chunk=1024, kvs=(), window_keeps=(), window_bufs=None):
    bk, tk, d = x.shape
    depth = wu_b.shape[0]
    row = pl.BlockSpec((1, tm, d), lambda i, j: (i, j, 0))
    aliased = list(window_bufs or ())
    owns_all_layers = bool(window_keeps) and not aliased
    lead, at = (depth, 0) if owns_all_layers else (None, layer)
    kv_specs, out_shape, out_specs, windows = [], [jax.ShapeDtypeStruct((bk, tk, d), F32)], [row], []
    for kv, keep in zip(kvs, window_keeps):
        every_tile = keep == tk
        assert every_tile or tk % keep == 0
        windows.append(every_tile)
        n = tm if every_tile else keep
        kv_specs.append(pl.BlockSpec((1, n, kv.shape[2]),
                                     (lambda i, j: (i, j, 0)) if every_tile else (lambda i, j, last=tk // keep - 1: (i, last, 0))))
        out_shape.append(jax.ShapeDtypeStruct((depth, bk, 2, HEADS_PER_GROUP, HEAD_DIM, keep), F32))
        out_specs.append(pl.BlockSpec((lead, None, 2, HEADS_PER_GROUP, HEAD_DIM, n),
                                      (lambda i, j: (at, i, 0, 0, 0, j)) if every_tile else (lambda i, j: (at, i, 0, 0, 0, 0))))
    n_in = 5 + len(kv_specs)
    return pl.pallas_call(
        functools.partial(_mlp_kernel, ff_chunk=ff_chunk, alpha=alpha, windows=tuple(windows),
                          first_layer=layer if owns_all_layers else None),
        grid=(bk, tk // tm),
        in_specs=[row, _layer_spec(wu_b, layer), _layer_spec(wd_b, layer), _layer_spec(g, layer), _layer_spec(b, layer)]
        + kv_specs + [pl.BlockSpec(memory_space=pl.ANY)] * len(aliased),
        out_specs=out_specs,
        out_shape=out_shape,
        input_output_aliases={n_in + k: 1 + k for k in range(len(aliased))},
        compiler_params=_params(2),
        name="mlp",
    )(x, wu_b, wd_b, g, b, *kvs, *aliased)


def kernel(x_prompt, x_sample, cache_kv_w128, cache_kv_w512, cache_kv_w2048, cache_mem_kv, mem_prompt,
           w_in, sgu_ln_g, sgu_ln_b, w_spatial, b_spatial, w_mix_out, ln1_g, ln1_b,
           w_xq, w_xkv, w_xo, ln2_g, ln2_b, w_up, w_down, ln3_g, ln3_b):
    depth = w_in.shape[0]
    bp, tp, d = x_prompt.shape
    bs, ts, _ = x_sample.shape
    past_len = 8192
    alpha = float((2 * depth) ** 0.25)
    att_width = len(DILATIONS) * GROUP_WIDTH
    sgu_width = sgu_ln_g.shape[1]
    sgu_groups = w_spatial.shape[1]
    n_mem = mem_prompt.shape[1]
    rows_s = bs * ts
    assert tp % (DILATIONS[-1] * WINDOW_STEPS) == 0 and SUBLANES % ts == 0 and rows_s % SGU_CHUNK == 0
    assert d // MEM_HEADS == GROUP_WIDTH and sgu_width == GROUP_WIDTH

    col_scale = jnp.concatenate([jnp.full((att_width,), HEAD_DIM ** -0.5 * math.log2(math.e), F32),
                                 jnp.ones((w_in.shape[2] - att_width,), F32)])
    w_in_b = (w_in * col_scale).astype(BF16)
    w_xq_b = (w_xq * (d // MEM_HEADS) ** -0.5).astype(BF16)
    w_mix_b, w_xkv_b, w_xo_b = w_mix_out.astype(BF16), w_xkv.astype(BF16), w_xo.astype(BF16)
    w_up_b, w_down_b = w_up.astype(BF16), w_down.astype(BF16)
    vec = lambda p: p[:, None, :]
    sgu_g, sgu_b, g1, b1, g2, b2, g3, b3 = map(vec, (sgu_ln_g, sgu_ln_b, ln1_g, ln1_b, ln2_g, ln2_b, ln3_g, ln3_b))

    tables_p = _rope_tables(jnp.arange(tp, dtype=jnp.int32))
    tables_s = _rope_tables(past_len + (jnp.arange(rows_s, dtype=jnp.int32) % ts))

    row_s = jnp.arange(rows_s, dtype=jnp.int32)
    pick = (row_s[:, None] % ts == jnp.arange(ts)[None, :]).astype(F32)
    same_seq = (row_s[:, None] // ts == row_s[None, :] // ts).astype(F32)
    w_sp_sample = jnp.einsum("ra,lgab,cb->lgrc", pick, w_spatial[:, :, :ts, :ts], pick,
                             precision=lax.Precision.HIGHEST) * same_seq

    hp = x_prompt
    hs = x_sample.reshape(1, rows_s, d)
    out_rows_s = [[] for _ in DILATIONS]
    out_gv = []
    window_keeps = tuple(min(dil * WINDOW_STEPS, tp) for dil in DILATIONS)
    windows = None
    mem_out, mkv = _memkv(mem_prompt, w_xkv_b)
    caches_t = [jnp.transpose(c, (0, 1, 3, 4, 5, 2)) for c in (cache_kv_w128, cache_kv_w512, cache_kv_w2048)]
    mem_chunks = d // MEM_HEADS // LANES
    mem_flat = (cache_mem_kv.reshape(depth, bs, n_mem, 2, MEM_HEADS, mem_chunks, LANES)
                .transpose(0, 1, 2, 3, 5, 4, 6).reshape(depth, bs, n_mem * 2 * mem_chunks * MEM_HEADS, LANES))
    for l in range(depth):
        q, kv0, kv1, kv2, u, gv = _inproj(hp, w_in_b, l, tables_p, sgu_g, sgu_b, tm=512, gate_dtype=BF16)
        atts = _prompt_attention(q, [kv0, kv1, kv2])
        b_tile = jnp.repeat(b_spatial[l][:, :SGU_CHUNK].T, sgu_width // sgu_groups, axis=1)
        hp = _mix(atts, u, gv, hp, w_spatial[l], b_tile, w_mix_b, l, g1, b1, tm=1024, alpha=alpha,
                  cross=(mkv, w_xq_b, w_xo_b, g2, b2))
        hp, *windows = _mlp(hp, w_up_b, w_down_b, l, g3, b3, tm=512, alpha=alpha,
                            kvs=(kv0, kv1, kv2), window_keeps=window_keeps, window_bufs=windows)

        q, kv0, kv1, kv2, u, gv = _inproj(hs, w_in_b, l, tables_s, sgu_g, sgu_b, tm=rows_s, gate_dtype=F32)
        out_gv.append(gv.reshape(bs, ts, sgu_width))
        for g, kv in enumerate((kv0, kv1, kv2)):
            out_rows_s[g].append(kv.reshape(bs, ts, 2, HEADS_PER_GROUP, HEAD_DIM))
        att = _sample_attention(q[0], (kv0, kv1, kv2), caches_t, l, dec_seq=ts)
        b_tile_s = jnp.repeat(jnp.tile(b_spatial[l][:, :ts].T, (rows_s // ts, 1)), sgu_width // sgu_groups, axis=1)
        hs = _mix([att[None]], u, gv, hs, w_sp_sample[l], b_tile_s, w_mix_b, l, g1, b1, tm=rows_s, alpha=alpha)
        hs2 = _sample_xattn(hs[0], mem_flat, l, w_xq_b, w_xo_b, g2, b2, dec_seq=ts, seqs_per_step=4, alpha=alpha)
        hs, = _mlp(hs2[None], w_up_b, w_down_b, l, g3, b3, tm=rows_s, alpha=alpha)

    rows_p = [jnp.transpose(w, (0, 1, 5, 2, 3, 4)) for w in windows]
    mem_p = (mem_out.reshape(depth, bp, n_mem, 2, mem_chunks, MEM_HEADS, LANES)
             .transpose(0, 1, 2, 3, 5, 4, 6).reshape(depth, bp, n_mem, 2, MEM_HEADS, d // MEM_HEADS))
    stack = lambda xs: jnp.stack(xs)
    return (hp, hs.reshape(bs, ts, d), rows_p[0], rows_p[1], rows_p[2], mem_p,
            stack(out_rows_s[0]), stack(out_rows_s[1]), stack(out_rows_s[2]), stack(out_gv))
```

```python
import functools
import math

import jax
import jax.numpy as jnp
from jax import lax
from jax.experimental import pallas as pl
from jax.experimental.pallas import tpu as pltpu

F32 = jnp.float32
BF16 = jnp.bfloat16

HEAD_DIM = 64
HEADS_PER_GROUP = 4
GROUP_WIDTH = HEAD_DIM * HEADS_PER_GROUP
DILATIONS = (1, 4, 16)
WINDOW_STEPS = 128
ROT_DIM = 16
ROPE_THETA = 500000.0
SGU_CHUNK = 128
MEM_HEADS = 4
LN_EPS = 1e-5
NEG = -1e30
LANES = 128
SUBLANES = 8
VMEM_LIMIT = 56 * 1024 * 1024


def _params(n_grid_dims):
    return pltpu.CompilerParams(
        dimension_semantics=("arbitrary",) * n_grid_dims,
        vmem_limit_bytes=VMEM_LIMIT)


def _const_spec(shape):
    nd = len(shape)
    return pl.BlockSpec(shape, lambda *_: (0,) * nd, pipeline_mode=pl.Buffered(1))


def _layer_spec(stacked, layer):
    rest = stacked.shape[1:]
    return pl.BlockSpec((None,) + rest, lambda *_: (layer,) + (0,) * len(rest), pipeline_mode=pl.Buffered(1))


def _side_cast_specs(casts, grid):
    n_steps = grid[0] * grid[1]
    specs, shapes = [], []
    for w, _ in casts:
        depth, k, n = w.shape
        specs.append(pl.BlockSpec((depth, k // n_steps, n), lambda i, j: (0, i * grid[1] + j, 0)))
        shapes.append(jax.ShapeDtypeStruct(w.shape, BF16))
    return specs, shapes


def _side_cast(in_refs, out_refs, scales):
    for r_in, r_out, scale in zip(in_refs, out_refs, scales):
        w = r_in[...]
        r_out[...] = (w if scale == 1.0 else w * scale).astype(r_out.dtype)


def _layer_norm(y, g, b):
    mu = jnp.mean(y, axis=-1, keepdims=True)
    yc = y - mu
    var = jnp.mean(yc * yc, axis=-1, keepdims=True)
    return yc * lax.rsqrt(var + LN_EPS) * g + b


def _gelu_tanh(x):
    return 0.5 * x * (1.0 + jnp.tanh(0.7978845608028654 * (x + 0.044715 * (x * x * x))))


def _head_half_mask(width, half):
    lane = lax.broadcasted_iota(jnp.int32, (1, width), 1)
    return (lane // HEAD_DIM) % 2 == half


def _inproj_kernel(*refs, att_width, sgu_width, cast_scales):
    n_cast = len(cast_scales)
    x_ref, w_ref, cos_ref, sin_lo_ref, sin_hi_ref, g_ref, b_ref = refs[:7]
    q_ref, kv0_ref, kv1_ref, kv2_ref, u_ref, gv_ref = refs[7 + n_cast:13 + n_cast]
    _side_cast(refs[7:7 + n_cast], refs[13 + n_cast:], cast_scales)
    xb = x_ref[0].astype(BF16)
    cos = cos_ref[...]
    sin_lo = sin_lo_ref[...]
    sin_hi = sin_hi_ref[...]

    def proj(c0, width):
        return jnp.dot(xb, w_ref[:, c0:c0 + width], preferred_element_type=F32)

    def rope(t):
        return t * cos + pltpu.roll(t, LANES - ROT_DIM // 2, 1) * sin_lo + pltpu.roll(t, ROT_DIM // 2, 1) * sin_hi

    def rope_group(t):
        return jnp.concatenate([rope(t[:, s:s + LANES]) for s in range(0, GROUP_WIDTH, LANES)], axis=1)

    u_ref[0] = _gelu_tanh(proj(3 * att_width, sgu_width)).astype(u_ref.dtype)
    gate = _gelu_tanh(proj(3 * att_width + sgu_width, sgu_width))
    gv_ref[0] = _layer_norm(gate, g_ref[...], b_ref[...]).astype(gv_ref.dtype)
    kv_refs = (kv0_ref, kv1_ref, kv2_ref)
    for grp, kv_ref in enumerate(kv_refs):
        c = grp * GROUP_WIDTH
        q_ref[0, :, c:c + GROUP_WIDTH] = rope_group(proj(c, GROUP_WIDTH))
        kv_ref[0, :, 0:GROUP_WIDTH] = rope_group(proj(att_width + c, GROUP_WIDTH))
    for grp, kv_ref in enumerate(kv_refs):
        kv_ref[0, :, GROUP_WIDTH:2 * GROUP_WIDTH] = proj(2 * att_width + grp * GROUP_WIDTH, GROUP_WIDTH)


def _inproj(x, w_in_b, layer, tables, g, b, tm, gate_dtype, casts=()):
    bk, tk, d = x.shape
    att_width = len(DILATIONS) * GROUP_WIDTH
    sgu_width = (w_in_b.shape[2] - 3 * att_width) // 2
    cos, sin_lo, sin_hi = tables
    grid = (bk, tk // tm)
    row = lambda w: pl.BlockSpec((1, tm, w), lambda i, j: (i, j, 0))
    tab = pl.BlockSpec((tm, LANES), lambda i, j: (j, 0))
    out_shape = ([jax.ShapeDtypeStruct((bk, tk, w), F32) for w in (att_width,) + (2 * GROUP_WIDTH,) * len(DILATIONS)]
                 + [jax.ShapeDtypeStruct((bk, tk, sgu_width), gate_dtype)] * 2)
    cast_specs, cast_shapes = _side_cast_specs(casts, grid)
    return pl.pallas_call(
        functools.partial(_inproj_kernel, att_width=att_width, sgu_width=sgu_width,
                          cast_scales=tuple(s for _, s in casts)),
        grid=grid,
        in_specs=[row(d), _layer_spec(w_in_b, layer), tab, tab, tab, _layer_spec(g, layer), _layer_spec(b, layer)]
        + cast_specs,
        out_specs=[row(s.shape[2]) for s in out_shape] + cast_specs,
        out_shape=out_shape + cast_shapes,
        compiler_params=_params(2),
        name="inproj",
    )(x, w_in_b, cos, sin_lo, sin_hi, g, b, *(w for w, _ in casts))


def _rope_tables(pos):
    half = ROT_DIM // 2
    inv = ROPE_THETA ** (-jnp.arange(half, dtype=F32) / half)
    ang = pos.astype(F32)[:, None] * inv[None, :]
    cos, sin = jnp.cos(ang), jnp.sin(ang)
    zeros = jnp.zeros((pos.shape[0], HEAD_DIM - ROT_DIM), F32)
    zero_half = jnp.zeros_like(sin)
    cos_head = jnp.concatenate([cos, cos, zeros + 1.0], axis=1)
    lo_head = jnp.concatenate([-sin, zero_half, zeros], axis=1)
    hi_head = jnp.concatenate([zero_half, sin, zeros], axis=1)
    two = lambda t: jnp.concatenate([t, t], axis=1)
    return two(cos_head), two(lo_head), two(hi_head)


ATTN_MIX_ROWS = 256
ATTN_TILES_PER_BATCH = 32


def _attn_kernel(*refs, seq):
    n_groups = len(DILATIONS)
    q_refs, k_refs, v_refs = refs[0:n_groups], refs[n_groups:2 * n_groups], refs[2 * n_groups:3 * n_groups]
    att_refs = refs[3 * n_groups:4 * n_groups]
    o_s, m_s, den_s = refs[4 * n_groups:]
    blk = WINDOW_STEPS
    x_idx = lax.broadcasted_iota(jnp.int32, (blk, 2 * blk), 0)
    k_idx = lax.broadcasted_iota(jnp.int32, (blk, 2 * blk), 1)
    band_mask = (k_idx >= x_idx) & (k_idx <= x_idx + blk)
    causal_mask = (lax.broadcasted_iota(jnp.int32, (blk, blk), 1)
                   <= lax.broadcasted_iota(jnp.int32, (blk, blk), 0))
    half_masks = [_head_half_mask(LANES, hh) for hh in range(2)]

    def batch(g, dil, blocks):
        def rows(r, first_block, n):
            start = r + dil * blk * first_block
            return pl.ds(start, n) if dil == 1 else pl.ds(start, n, stride=dil)

        first = blocks[0][1] == 0
        mask = causal_mask if first else band_mask
        scores, values = [], []
        for r, c in blocks:
            k_rows = rows(r, 0, blk) if first else rows(r, c - 1, 2 * blk)
            qp = q_refs[g][0, rows(r, c, blk), :]
            kp = k_refs[g][0, k_rows, :].astype(BF16)
            values.append(v_refs[g][0, k_rows, :].astype(BF16))
            for hh in range(2):
                qh = jnp.where(half_masks[hh], qp, 0.0).astype(BF16)
                s = lax.dot_general(qh, kp, (((1,), (1,)), ((), ())), preferred_element_type=F32)
                scores.append(jnp.where(mask, s, NEG))
        s_all = jnp.concatenate(scores, axis=0)
        m = jnp.max(s_all, axis=-1, keepdims=True)
        p = jnp.exp2(s_all - m)
        den = jnp.sum(p, axis=-1, keepdims=True)
        p = p.astype(BF16)
        for n, (r, c) in enumerate(blocks):
            piece = lambda t, hh: t[(2 * n + hh) * blk:(2 * n + hh + 1) * blk]
            outs = [jnp.dot(piece(p, hh), values[n], preferred_element_type=F32) for hh in range(2)]
            o_s[g, rows(r, c, blk), :] = jnp.where(half_masks[0], outs[0], outs[1])
            m_s[g, rows(r, c, blk), :] = jnp.where(half_masks[0], piece(m, 0), piece(m, 1))
            den_s[g, rows(r, c, blk), :] = jnp.where(half_masks[0], piece(den, 0), piece(den, 1))

    for g, dil in enumerate(DILATIONS):
        n_blocks = seq // dil // blk
        first_blocks = [(r, 0) for r in range(dil)]
        later_blocks = [(r, c) for r in range(dil) for c in range(1, n_blocks)]
        per_first = ATTN_TILES_PER_BATCH // 2
        per_later = ATTN_TILES_PER_BATCH // 4
        for i in range(0, len(first_blocks), per_first):
            batch(g, dil, first_blocks[i:i + per_first])
        for i in range(0, len(later_blocks), per_later):
            batch(g, dil, later_blocks[i:i + per_later])

    for t0 in range(0, seq, ATTN_MIX_ROWS):
        rws = slice(t0, t0 + ATTN_MIX_ROWS)
        tops = [m_s[g, rws, :] for g in range(n_groups)]
        peak = functools.reduce(jnp.maximum, tops)
        scales = [jnp.exp2(t - peak) for t in tops]
        inv = 1.0 / functools.reduce(jnp.add, [den_s[g, rws, :] * scales[g] for g in range(n_groups)])
        for g in range(n_groups):
            att_refs[g][0, rws, :] = (o_s[g, rws, :] * (scales[g] * inv)).astype(att_refs[g].dtype)


def _prompt_attention(q, kvs):
    bk, tk, _ = q.shape
    n_groups = len(DILATIONS)
    pairs = GROUP_WIDTH // LANES
    col = lambda first: pl.BlockSpec((1, tk, LANES), lambda i, j: (i, 0, first + j))
    out = jax.ShapeDtypeStruct((bk, tk, GROUP_WIDTH), BF16)
    return pl.pallas_call(
        functools.partial(_attn_kernel, seq=tk),
        grid=(bk, pairs),
        in_specs=([col(g * pairs) for g in range(n_groups)]
                  + [col(0)] * n_groups
                  + [col(pairs)] * n_groups),
        out_specs=[col(0)] * n_groups,
        out_shape=[out] * n_groups,
        scratch_shapes=[pltpu.VMEM((n_groups, tk, LANES), F32)] * 3,
        compiler_params=_params(2),
        name="attn",
    )(*([q] * n_groups), *kvs, *kvs)


def _sample_attn_kernel(q_ref, kvn0_ref, kvn1_ref, kvn2_ref, c0_ref, c1_ref, c2_ref, att_ref, *, dec_seq):
    kvn_refs = (kvn0_ref, kvn1_ref, kvn2_ref)
    cache_refs = (c0_ref, c1_ref, c2_ref)
    seqs_per_slab = SUBLANES // dec_seq
    n_groups = len(DILATIONS)
    widest = DILATIONS[-1]
    slab = pl.ds(pl.multiple_of(pl.program_id(0) * SUBLANES, SUBLANES), SUBLANES)
    lane = lax.broadcasted_iota(jnp.int32, (1, LANES), 1)
    row8 = lax.broadcasted_iota(jnp.int32, (SUBLANES, LANES), 0)
    sel_rows = 2 * SUBLANES
    sel_row = lax.broadcasted_iota(jnp.int32, (sel_rows, LANES), 0)
    sel_lane = lax.broadcasted_iota(jnp.int32, (sel_rows, LANES), 1)
    query_class = lane % widest

    def one_hot(cond):
        return jnp.where(cond, 1.0, 0.0).astype(BF16)

    def spread(rows8, sel):
        padded = jnp.concatenate([rows8, jnp.zeros_like(rows8)], axis=0).astype(BF16)
        return lax.dot_general(padded, sel, (((0,), (0,)), ((), ())), preferred_element_type=F32)

    def class_fold(x, period, op):
        shift = LANES // 2
        while shift >= period:
            x = op(x, pltpu.roll(x, shift, 1))
            shift //= 2
        return x

    def valid_mask(n_tiles, cache_ok, new_ok):
        rows = SUBLANES * (n_tiles // SUBLANES + 1)
        r = lax.broadcasted_iota(jnp.int32, (rows, LANES), 0)
        p = lax.broadcasted_iota(jnp.int32, (rows, LANES), 1)
        return ((r < n_tiles) & cache_ok(p)) | ((r == n_tiles) & new_ok(p))

    def attend(qm, cache_ref, seq, head, knt, vnt, n_tiles, valid, period):
        groups = []
        for t0 in range(0, n_tiles + 1, SUBLANES):
            acc = jnp.full((SUBLANES, LANES), NEG, F32)
            for t in range(t0, min(t0 + SUBLANES, n_tiles + 1)):
                keys = knt if t == n_tiles else cache_ref[0, seq, 0, head, :, t * LANES:(t + 1) * LANES]
                acc = jnp.where(row8 == t - t0, jnp.sum(qm * keys, axis=0, keepdims=True), acc)
            groups.append(acc)
        scores = jnp.where(valid, groups[0] if len(groups) == 1 else jnp.concatenate(groups, axis=0), NEG)
        top = jnp.broadcast_to(jnp.max(scores, axis=0, keepdims=True), (SUBLANES, LANES))
        if period:
            top = class_fold(top, period, jnp.maximum)
        else:
            top = jnp.broadcast_to(jnp.max(top, axis=1, keepdims=True), (SUBLANES, LANES))
        probs = jnp.exp2(scores - jnp.concatenate([top] * len(groups), axis=0))
        den = jnp.broadcast_to(jnp.sum(probs, axis=0, keepdims=True), (SUBLANES, LANES))
        num = jnp.zeros((HEAD_DIM, LANES), F32)
        for t in range(n_tiles + 1):
            vals = vnt if t == n_tiles else cache_ref[0, seq, 1, head, :, t * LANES:(t + 1) * LANES]
            num = num + vals * probs[t:t + 1, :]
        if period:
            return class_fold(num, period, jnp.add), top, class_fold(den, period, jnp.add)
        return (jnp.broadcast_to(jnp.sum(num, axis=1, keepdims=True), (HEAD_DIM, LANES)), top,
                jnp.broadcast_to(jnp.sum(den, axis=1, keepdims=True), (SUBLANES, LANES)))

    for pair in range(GROUP_WIDTH // LANES):
        mixed = [[None] * seqs_per_slab for _ in range(n_groups)]
        for e in range(seqs_per_slab):
            base = e * dec_seq
            sel_new = one_hot((sel_row == base + sel_lane) & (sel_lane < dec_seq))
            per_group = []
            for g, dil in enumerate(DILATIONS):
                off = pair * LANES
                knt = spread(kvn_refs[g][0, slab, off:off + LANES], sel_new)
                vnt = spread(kvn_refs[g][0, slab, GROUP_WIDTH + off:GROUP_WIDTH + off + LANES], sel_new)
                q8 = q_ref[slab, g * GROUP_WIDTH + off:g * GROUP_WIDTH + off + LANES]
                n_tiles = cache_refs[g].shape[5] // LANES
                if dil == 1:
                    qms = [spread(q8, one_hot(sel_row == base + i)) for i in range(dec_seq)]
                    valids = [valid_mask(n_tiles, lambda p, i=i: p >= i, lambda p, i=i: p <= i) for i in range(dec_seq)]
                else:
                    qms = [spread(q8, one_hot((sel_row == base + sel_lane % dil) & (sel_lane % dil < dec_seq)))]
                    valids = [valid_mask(n_tiles, lambda p, dil=dil: p % dil < dec_seq, lambda p: p < dec_seq)]
                heads = []
                for hh in range(2):
                    hrows = slice(hh * HEAD_DIM, (hh + 1) * HEAD_DIM)
                    num = top = den = None
                    for i, (qm, valid) in enumerate(zip(qms, valids)):
                        n_i, t_i, d_i = attend(qm[hrows], cache_refs[g], e, 2 * pair + hh, knt[hrows], vnt[hrows],
                                               n_tiles, valid, dil if dil > 1 else 0)
                        if num is None:
                            num, top, den = n_i, t_i, d_i
                        else:
                            mine = query_class == i
                            num, top, den = jnp.where(mine, n_i, num), jnp.where(mine, t_i, top), jnp.where(mine, d_i, den)
                    heads.append([num, top, den])
                per_group.append(heads)
            for hh in range(2):
                tops = [per_group[g][hh][1] for g in range(n_groups)]
                peak = functools.reduce(jnp.maximum, tops)
                scales = [jnp.exp2(t - peak) for t in tops]
                total = functools.reduce(jnp.add, [per_group[g][hh][2] * scales[g] for g in range(n_groups)])
                for g in range(n_groups):
                    per_group[g][hh] = per_group[g][hh][0] * (scales[g] / total)[0:1, :]
            for g in range(n_groups):
                mixed[g][e] = jnp.concatenate(per_group[g], axis=0)
        sel_out = one_hot(sel_lane == widest * (sel_row // dec_seq) + sel_row % dec_seq)
        for g in range(n_groups):
            both = mixed[g][0]
            for e in range(1, seqs_per_slab):
                both = jnp.where((lane >= widest * e) & (lane < widest * (e + 1)), mixed[g][e], both)
            hi = both.astype(BF16)
            lo = (both - hi.astype(F32)).astype(BF16)
            dims = (((1,), (1,)), ((), ()))
            rows = (lax.dot_general(sel_out, hi, dims, preferred_element_type=F32)
                    + lax.dot_general(sel_out, lo, dims, preferred_element_type=F32))
            att_ref[slab, g * GROUP_WIDTH + pair * LANES:g * GROUP_WIDTH + (pair + 1) * LANES] = rows[:SUBLANES]


def _sample_attention(q, kvns, caches_t, layer, dec_seq):
    rows, width = q.shape
    seqs_per_slab = SUBLANES // dec_seq
    cache_spec = lambda c: pl.BlockSpec((1, seqs_per_slab) + c.shape[2:], lambda i: (layer, i, 0, 0, 0, 0))
    return pl.pallas_call(
        functools.partial(_sample_attn_kernel, dec_seq=dec_seq),
        grid=(rows // SUBLANES,),
        in_specs=[_const_spec(q.shape)] + [_const_spec(k.shape) for k in kvns] + [cache_spec(c) for c in caches_t],
        out_specs=pl.BlockSpec((rows, width), lambda i: (0, 0)),
        out_shape=jax.ShapeDtypeStruct((rows, width), F32),
        compiler_params=_params(1),
        name="sample_attn",
    )(q, *kvns, *caches_t)


MIX_SUBTILE = 1024


def _mix_kernel(*refs, tm, alpha, n_att, cross_attend, cast_scales):
    n_cast = len(cast_scales)
    u_ref, gv_ref, x_ref, wsp_ref, bsp_ref, wm_ref, g_ref, b_ref = refs[n_att:n_att + 8]
    n_cross = 5 if cross_attend else 0
    xattn_refs = refs[n_att + 8:n_att + 8 + n_cross]
    cast_in_refs = refs[n_att + 8 + n_cross:n_att + 8 + n_cross + n_cast]
    out_ref = refs[n_att + 8 + n_cross + n_cast]
    _side_cast(cast_in_refs, refs[n_att + 9 + n_cross + n_cast:], cast_scales)
    t_idx = lax.broadcasted_iota(jnp.int32, (SGU_CHUNK, SGU_CHUNK), 0)
    s_idx = lax.broadcasted_iota(jnp.int32, (SGU_CHUNK, SGU_CHUNK), 1)
    w_sp = [jnp.where(s_idx <= t_idx, wsp_ref[k], 0.0).astype(BF16) for k in range(wsp_ref.shape[0])]
    sgu_width = gv_ref.shape[2]

    def gated_rows(r0, n_rows):
        starts = range(r0, r0 + n_rows, SGU_CHUNK)
        per_pair = []
        for pair in range(sgu_width // LANES):
            gps = [gv_ref[0, c:c + SGU_CHUNK, pair * LANES:(pair + 1) * LANES] for c in starts]
            mixed = None
            for hh in range(2):
                wide = jnp.concatenate([jnp.where(_head_half_mask(LANES, hh), gp, 0.0).astype(BF16) for gp in gps], axis=1)
                part = jnp.dot(w_sp[2 * pair + hh], wide, preferred_element_type=F32)
                mixed = part if mixed is None else mixed + part
            per_pair.append(mixed)
        mixed = jnp.concatenate(
            [jnp.concatenate([m[:, n * LANES:(n + 1) * LANES] for m in per_pair], axis=1) for n in range(len(starts))],
            axis=0)
        bias = jnp.concatenate([bsp_ref[...]] * len(starts), axis=0)
        return (u_ref[0, r0:r0 + n_rows, :] * (mixed + bias)).astype(BF16)

    sub = min(tm, MIX_SUBTILE)
    for r0 in range(0, tm, sub):
        rws = slice(r0, r0 + sub)
        sgu = gated_rows(r0, sub)
        lhs = jnp.concatenate([a[0, rws, :].astype(BF16) for a in refs[:n_att]] + [sgu], axis=1)
        y = alpha * x_ref[0, rws, :] + jnp.dot(lhs, wm_ref[...], preferred_element_type=F32)
        x1 = _layer_norm(y, g_ref[...], b_ref[...])
        out_ref[0, rws, :] = _xattn_rows(x1, *xattn_refs, alpha) if cross_attend else x1


def _mix(att_inputs, u, gv, x, w_sp, b_sp_tile, w_mix_b, layer, g, b, tm, alpha, cross=None, casts=()):
    bk, tk, d = x.shape
    sgu_width = u.shape[2]
    grid = (bk, tk // tm)
    row = lambda w: pl.BlockSpec((1, tm, w), lambda i, j: (i, j, 0))
    cross_specs = []
    if cross:
        mkv = cross[0]
        cross_specs = ([pl.BlockSpec((None, 1) + mkv.shape[2:], lambda i, j: (layer, i, 0, 0))]
                       + [_layer_spec(p, layer) for p in cross[1:]])
    cast_specs, cast_shapes = _side_cast_specs(casts, grid)
    return pl.pallas_call(
        functools.partial(_mix_kernel, tm=tm, alpha=alpha, n_att=len(att_inputs), cross_attend=bool(cross),
                          cast_scales=tuple(s for _, s in casts)),
        grid=grid,
        in_specs=[row(a.shape[2]) for a in att_inputs] + [
            row(sgu_width), row(sgu_width), row(d),
            _const_spec(w_sp.shape), _const_spec(b_sp_tile.shape), _layer_spec(w_mix_b, layer),
            _layer_spec(g, layer), _layer_spec(b, layer)] + cross_specs + cast_specs,
        out_specs=[row(d)] + cast_specs,
        out_shape=[jax.ShapeDtypeStruct((bk, tk, d), F32)] + cast_shapes,
        compiler_params=_params(2),
        name="mix_xattn" if cross else "mix",
    )(*att_inputs, u, gv, x, w_sp, b_sp_tile, w_mix_b, g, b, *(cross or ()), *(w for w, _ in casts))


def _mem_row_stride(d):
    return 2 * (d // MEM_HEADS // LANES) * MEM_HEADS


def _memkv_kernel(m_ref, w_ref, flat_ref, out_b_ref):
    depth, _, n_mem, width = out_b_ref.shape
    d = width // 2
    hd = d // MEM_HEADS
    n_chunks = hd // LANES
    mb = m_ref[0].astype(BF16)
    for layer in range(depth):
        mkv = jnp.dot(mb, w_ref[layer], preferred_element_type=F32)
        out_b_ref[layer, 0] = mkv.astype(BF16)
        for kv in range(2):
            for c in range(n_chunks):
                for h in range(MEM_HEADS):
                    col = kv * d + h * hd + c * LANES
                    rows = pl.ds((kv * n_chunks + c) * MEM_HEADS + h, n_mem, stride=_mem_row_stride(d))
                    flat_ref[layer, rows, :] = mkv[:, col:col + LANES]


def _memkv(mem, w_xkv_b):
    bk, n_mem, d = mem.shape
    depth, _, width = w_xkv_b.shape
    flat_rows = n_mem * _mem_row_stride(d)
    return pl.pallas_call(
        _memkv_kernel,
        grid=(bk,),
        in_specs=[pl.BlockSpec((1, n_mem, d), lambda i: (i, 0, 0)), _const_spec(w_xkv_b.shape)],
        out_specs=[pl.BlockSpec((depth, None, flat_rows, LANES), lambda i: (0, i, 0, 0)),
                   pl.BlockSpec((depth, 1, n_mem, width), lambda i: (0, i, 0, 0))],
        out_shape=[jax.ShapeDtypeStruct((depth, bk, flat_rows, LANES), F32),
                   jax.ShapeDtypeStruct((depth, bk, n_mem, width), BF16)],
        compiler_params=_params(1),
        name="memkv",
    )(mem, w_xkv_b)


def _softmax_rows(s):
    m = jnp.max(s, axis=-1, keepdims=True)
    p = jnp.exp(s - m)
    return p * (1.0 / jnp.sum(p, axis=-1, keepdims=True))


def _xattn_rows(x, mkv_ref, wq_ref, wo_ref, g_ref, b_ref, alpha):
    d = x.shape[1]
    hd = d // MEM_HEADS
    qx = jnp.dot(x.astype(BF16), wq_ref[...], preferred_element_type=F32)
    heads = []
    for h in range(MEM_HEADS):
        qh = qx[:, h * hd:(h + 1) * hd].astype(BF16)
        kh = mkv_ref[0, :, h * hd:(h + 1) * hd]
        vh = mkv_ref[0, :, d + h * hd:d + (h + 1) * hd]
        s = lax.dot_general(qh, kh, (((1,), (1,)), ((), ())), preferred_element_type=F32)
        m = jnp.max(s, axis=-1, keepdims=True)
        p = jnp.exp(s - m)
        inv = 1.0 / jnp.sum(p, axis=-1, keepdims=True)
        heads.append((jnp.dot(p.astype(BF16), vh, preferred_element_type=F32) * inv).astype(BF16))
    y = alpha * x + jnp.dot(jnp.concatenate(heads, axis=1), wo_ref[...], preferred_element_type=F32)
    return _layer_norm(y, g_ref[...], b_ref[...])


def _sample_xattn_kernel(x_ref, mkv_ref, wq_ref, wo_ref, g_ref, b_ref, out_ref, qx_ref, ox_ref,
                         *, dec_seq, seqs_per_step, alpha):
    step = pl.program_id(0)
    d = x_ref.shape[1]
    hd = d // MEM_HEADS
    seqs_per_slab = SUBLANES // dec_seq

    @pl.when(step == 0)
    def _():
        qx_ref[...] = jnp.dot(x_ref[...].astype(BF16), wq_ref[...], preferred_element_type=F32)

    n_chunks = hd // LANES
    rows_per_mem = 2 * n_chunks * MEM_HEADS
    n_mem = mkv_ref.shape[2] // rows_per_mem

    def head_matrix(b_local, kv, h):
        chunks = [mkv_ref[0, b_local, pl.ds((kv * n_chunks + c) * MEM_HEADS + h, n_mem, stride=rows_per_mem), :]
                  for c in range(n_chunks)]
        return jnp.concatenate(chunks, axis=1).astype(BF16)

    slab_row = lax.broadcasted_iota(jnp.int32, (SUBLANES, 1), 0)
    slabs = [pl.ds(pl.multiple_of((step * (seqs_per_step // seqs_per_slab) + j) * SUBLANES, SUBLANES), SUBLANES)
             for j in range(seqs_per_step // seqs_per_slab)]
    units = [(j, h, e) for j in range(len(slabs)) for h in range(MEM_HEADS) for e in range(seqs_per_slab)]
    scores = []
    for j, h, e in units:
        if e == 0:
            q8 = qx_ref[slabs[j], h * hd:(h + 1) * hd]
            qh = jnp.concatenate([q8, jnp.zeros_like(q8)], axis=0).astype(BF16)
        scores.append(lax.dot_general(head_matrix(j * seqs_per_slab + e, 0, h), qh, (((1,), (1,)), ((), ())),
                                      preferred_element_type=F32))
    s_all = jnp.stack(scores)
    p_all = jnp.exp(s_all - jnp.max(s_all, axis=1, keepdims=True))
    p_all = (p_all * (1.0 / jnp.sum(p_all, axis=1, keepdims=True))).astype(BF16)
    o8 = None
    for n, (j, h, e) in enumerate(units):
        oh = lax.dot_general(p_all[n], head_matrix(j * seqs_per_slab + e, 1, h), (((0,), (0,)), ((), ())),
                             preferred_element_type=F32)
        o8 = oh[:SUBLANES] if e == 0 else jnp.where(slab_row // dec_seq == e, oh[:SUBLANES], o8)
        if e == seqs_per_slab - 1:
            ox_ref[slabs[j], h * hd:(h + 1) * hd] = o8

    @pl.when(step == pl.num_programs(0) - 1)
    def _():
        y = alpha * x_ref[...] + jnp.dot(ox_ref[...].astype(BF16), wo_ref[...], preferred_element_type=F32)
        out_ref[...] = _layer_norm(y, g_ref[...], b_ref[...])


def _sample_xattn(x, mkv_flat, layer, wq_b, wo_b, g, b, dec_seq, seqs_per_step, alpha):
    rows, d = x.shape
    _, n_seq, flat_rows, _ = mkv_flat.shape
    return pl.pallas_call(
        functools.partial(_sample_xattn_kernel, dec_seq=dec_seq, seqs_per_step=seqs_per_step, alpha=alpha),
        grid=(n_seq // seqs_per_step,),
        in_specs=[_const_spec(x.shape),
                  pl.BlockSpec((1, seqs_per_step, flat_rows, LANES), lambda i: (layer, i, 0, 0)),
                  _layer_spec(wq_b, layer), _layer_spec(wo_b, layer), _layer_spec(g, layer), _layer_spec(b, layer)],
        out_specs=pl.BlockSpec((rows, d), lambda i: (0, 0)),
        out_shape=jax.ShapeDtypeStruct((rows, d), F32),
        scratch_shapes=[pltpu.VMEM((rows, d), F32), pltpu.VMEM((rows, d), F32)],
        compiler_params=_params(1),
        name="sample_xattn",
    )(x, mkv_flat, wq_b, wo_b, g, b)


MLP_SUBTILE = 512


def _mlp_kernel(*refs, ff_chunk, alpha, windows, first_layer):
    n_win = len(windows)
    x_ref, wu_ref, wd_ref, g_ref, b_ref = refs[:5]
    kv_refs = refs[5:5 + n_win]
    out_ref = refs[-1 - n_win]
    window_refs = list(refs[len(refs) - n_win:])

    def write_window(grp):
        win_ref = window_refs[grp] if first_layer is None else window_refs[grp].at[first_layer]
        for part in range(2):
            t = kv_refs[grp][0, :, part * GROUP_WIDTH:(part + 1) * GROUP_WIDTH].T
            for h in range(HEADS_PER_GROUP):
                win_ref[part, h] = t[h * HEAD_DIM:(h + 1) * HEAD_DIM, :]
        if first_layer is not None:
            for other in range(window_refs[grp].shape[0]):
                if other != first_layer:
                    window_refs[grp][other] = jnp.zeros(window_refs[grp].shape[1:], F32)

    for grp, every_tile in enumerate(windows):
        if not every_tile:
            pl.when(pl.program_id(1) == pl.num_programs(1) - 1)(functools.partial(write_window, grp))
    for grp, every_tile in enumerate(windows):
        if every_tile:
            write_window(grp)

    tm = x_ref.shape[1]
    sub = min(tm, MLP_SUBTILE)
    for r0 in range(0, tm, sub):
        x = x_ref[0, r0:r0 + sub, :]
        xb = x.astype(BF16)
        y = alpha * x
        for c in range(wu_ref.shape[1] // ff_chunk):
            h = jnp.dot(xb, wu_ref[:, c * ff_chunk:(c + 1) * ff_chunk], preferred_element_type=F32)
            h = jnp.square(jnp.maximum(h, 0.0)).astype(BF16)
            y = y + jnp.dot(h, wd_ref[c * ff_chunk:(c + 1) * ff_chunk, :], preferred_element_type=F32)
        out_ref[0, r0:r0 + sub, :] = _layer_norm(y, g_ref[...], b_ref[...])


def _mlp(x, wu_b, wd_b, layer, g, b, tm, alpha, ff_chunk=1024, kvs=(), window_keeps=(), window_bufs=None):
    bk, tk, d = x.shape
    depth = wu_b.shape[0]
    row = pl.BlockSpec((1, tm, d), lambda i, j: (i, j, 0))
    aliased = list(window_bufs or ())
    owns_all_layers = bool(window_keeps) and not aliased
    lead, at = (depth, 0) if owns_all_layers else (None, layer)
    kv_specs, out_shape, out_specs, windows = [], [jax.ShapeDtypeStruct((bk, tk, d), F32)], [row], []
    for kv, keep in zip(kvs, window_keeps):
        every_tile = keep == tk
        assert every_tile or tk % keep == 0
        windows.append(every_tile)
        n = tm if every_tile else keep
        kv_specs.append(pl.BlockSpec((1, n, kv.shape[2]),
                                     (lambda i, j: (i, j, 0)) if every_tile else (lambda i, j, last=tk // keep - 1: (i, last, 0))))
        out_shape.append(jax.ShapeDtypeStruct((depth, bk, 2, HEADS_PER_GROUP, HEAD_DIM, keep), F32))
        out_specs.append(pl.BlockSpec((lead, None, 2, HEADS_PER_GROUP, HEAD_DIM, n),
                                      (lambda i, j: (at, i, 0, 0, 0, j)) if every_tile else (lambda i, j: (at, i, 0, 0, 0, 0))))
    n_in = 5 + len(kv_specs)
    return pl.pallas_call(
        functools.partial(_mlp_kernel, ff_chunk=ff_chunk, alpha=alpha, windows=tuple(windows),
                          first_layer=layer if owns_all_layers else None),
        grid=(bk, tk // tm),
        in_specs=[row, _layer_spec(wu_b, layer), _layer_spec(wd_b, layer), _layer_spec(g, layer), _layer_spec(b, layer)]
        + kv_specs + [pl.BlockSpec(memory_space=pl.ANY)] * len(aliased),
        out_specs=out_specs,
        out_shape=out_shape,
        input_output_aliases={n_in + k: 1 + k for k in range(len(aliased))},
        compiler_params=_params(2),
        name="mlp",
    )(x, wu_b, wd_b, g, b, *kvs, *aliased)


def kernel(x_prompt, x_sample, cache_kv_w128, cache_kv_w512, cache_kv_w2048, cache_mem_kv, mem_prompt,
           w_in, sgu_ln_g, sgu_ln_b, w_spatial, b_spatial, w_mix_out, ln1_g, ln1_b,
           w_xq, w_xkv, w_xo, ln2_g, ln2_b, w_up, w_down, ln3_g, ln3_b):
    depth = w_in.shape[0]
    bp, tp, d = x_prompt.shape
    bs, ts, _ = x_sample.shape
    past_len = 8192
    alpha = float((2 * depth) ** 0.25)
    att_width = len(DILATIONS) * GROUP_WIDTH
    sgu_width = sgu_ln_g.shape[1]
    sgu_groups = w_spatial.shape[1]
    n_mem = mem_prompt.shape[1]
    rows_s = bs * ts
    assert tp % (DILATIONS[-1] * WINDOW_STEPS) == 0 and SUBLANES % ts == 0 and rows_s % SGU_CHUNK == 0
    assert d // MEM_HEADS == GROUP_WIDTH and sgu_width == GROUP_WIDTH

    col_scale = jnp.concatenate([jnp.full((att_width,), HEAD_DIM ** -0.5 * math.log2(math.e), F32),
                                 jnp.ones((w_in.shape[2] - att_width,), F32)])
    w_in_b = (w_in * col_scale).astype(BF16)
    w_xkv_b = w_xkv.astype(BF16)
    w_mix_b = w_xq_b = w_xo_b = w_up_b = w_down_b = None
    vec = lambda p: p[:, None, :]
    sgu_g, sgu_b, g1, b1, g2, b2, g3, b3 = map(vec, (sgu_ln_g, sgu_ln_b, ln1_g, ln1_b, ln2_g, ln2_b, ln3_g, ln3_b))

    tables_p = _rope_tables(jnp.arange(tp, dtype=jnp.int32))
    tables_s = _rope_tables(past_len + (jnp.arange(rows_s, dtype=jnp.int32) % ts))

    row_s = jnp.arange(rows_s, dtype=jnp.int32)
    pick = (row_s[:, None] % ts == jnp.arange(ts)[None, :]).astype(F32)
    same_seq = (row_s[:, None] // ts == row_s[None, :] // ts).astype(F32)
    w_sp_sample = jnp.einsum("ra,lgab,cb->lgrc", pick, w_spatial[:, :, :ts, :ts], pick,
                             precision=lax.Precision.HIGHEST) * same_seq

    hp = x_prompt
    hs = x_sample.reshape(1, rows_s, d)
    out_rows_s = [[] for _ in DILATIONS]
    out_gv = []
    window_keeps = tuple(min(dil * WINDOW_STEPS, tp) for dil in DILATIONS)
    windows = None
    mem_out, mkv = _memkv(mem_prompt, w_xkv_b)
    caches_t = [jnp.transpose(c, (0, 1, 3, 4, 5, 2)) for c in (cache_kv_w128, cache_kv_w512, cache_kv_w2048)]
    mem_chunks = d // MEM_HEADS // LANES
    mem_flat = (cache_mem_kv.reshape(depth, bs, n_mem, 2, MEM_HEADS, mem_chunks, LANES)
                .transpose(0, 1, 2, 3, 5, 4, 6).reshape(depth, bs, n_mem * 2 * mem_chunks * MEM_HEADS, LANES))
    for l in range(depth):
        first = l == 0
        q, kv0, kv1, kv2, u, gv, *rounded = _inproj(
            hp, w_in_b, l, tables_p, sgu_g, sgu_b, tm=1024, gate_dtype=BF16,
            casts=((w_mix_out, 1.0), (w_xq, (d // MEM_HEADS) ** -0.5), (w_xo, 1.0)) if first else ())
        if first:
            w_mix_b, w_xq_b, w_xo_b = rounded
        atts = _prompt_attention(q, [kv0, kv1, kv2])
        b_tile = jnp.repeat(b_spatial[l][:, :SGU_CHUNK].T, sgu_width // sgu_groups, axis=1)
        hp, *rounded = _mix(atts, u, gv, hp, w_spatial[l], b_tile, w_mix_b, l, g1, b1, tm=1024, alpha=alpha,
                            cross=(mkv, w_xq_b, w_xo_b, g2, b2),
                            casts=((w_up, 1.0), (w_down, 1.0)) if first else ())
        if first:
            w_up_b, w_down_b = rounded
        hp, *windows = _mlp(hp, w_up_b, w_down_b, l, g3, b3, tm=512, alpha=alpha,
                            kvs=(kv0, kv1, kv2), window_keeps=window_keeps, window_bufs=windows)

        q, kv0, kv1, kv2, u, gv = _inproj(hs, w_in_b, l, tables_s, sgu_g, sgu_b, tm=rows_s, gate_dtype=F32)
        out_gv.append(gv.reshape(bs, ts, sgu_width))
        for g, kv in enumerate((kv0, kv1, kv2)):
            out_rows_s[g].append(kv.reshape(bs, ts, 2, HEADS_PER_GROUP, HEAD_DIM))
        att = _sample_attention(q[0], (kv0, kv1, kv2), caches_t, l, dec_seq=ts)
        b_tile_s = jnp.repeat(jnp.tile(b_spatial[l][:, :ts].T, (rows_s // ts, 1)), sgu_width // sgu_groups, axis=1)
        hs, = _mix([att[None]], u, gv, hs, w_sp_sample[l], b_tile_s, w_mix_b, l, g1, b1, tm=rows_s, alpha=alpha)
        hs2 = _sample_xattn(hs[0], mem_flat, l, w_xq_b, w_xo_b, g2, b2, dec_seq=ts, seqs_per_step=4, alpha=alpha)
        hs, = _mlp(hs2[None], w_up_b, w_down_b, l, g3, b3, tm=rows_s, alpha=alpha)

    rows_p = [jnp.transpose(w, (0, 1, 5, 2, 3, 4)) for w in windows]
    mem_p = (mem_out.reshape(depth, bp, n_mem, 2, mem_chunks, MEM_HEADS, LANES)
             .transpose(0, 1, 2, 3, 5, 4, 6).reshape(depth, bp, n_mem, 2, MEM_HEADS, d // MEM_HEADS))
    stack = lambda xs: jnp.stack(xs)
    return (hp, hs.reshape(bs, ts, d), rows_p[0], rows_p[1], rows_p[2], mem_p,
            stack(out_rows_s[0]), stack(out_rows_s[1]), stack(out_rows_s[2]), stack(out_gv))
```

```python
import functools
import math

import jax
import jax.numpy as jnp
from jax import lax
from jax.experimental import pallas as pl
from jax.experimental.pallas import tpu as pltpu

F32 = jnp.float32
BF16 = jnp.bfloat16

HEAD_DIM = 64
HEADS_PER_GROUP = 4
GROUP_WIDTH = HEAD_DIM * HEADS_PER_GROUP
DILATIONS = (1, 4, 16)
WINDOW_STEPS = 128
ROT_DIM = 16
ROPE_THETA = 500000.0
SGU_CHUNK = 128
MEM_HEADS = 4
LN_EPS = 1e-5
NEG = -1e30
LANES = 128
SUBLANES = 8
VMEM_LIMIT = 56 * 1024 * 1024


def _params(n_grid_dims):
    return pltpu.CompilerParams(
        dimension_semantics=("arbitrary",) * n_grid_dims,
        vmem_limit_bytes=VMEM_LIMIT)


def _const_spec(shape):
    nd = len(shape)
    return pl.BlockSpec(shape, lambda *_: (0,) * nd, pipeline_mode=pl.Buffered(1))


def _layer_spec(stacked, layer):
    rest = stacked.shape[1:]
    return pl.BlockSpec((None,) + rest, lambda *_: (layer,) + (0,) * len(rest), pipeline_mode=pl.Buffered(1))


def _side_cast_specs(casts, grid):
    n_steps = grid[0] * grid[1]
    specs, shapes = [], []
    for w, _ in casts:
        depth, k, n = w.shape
        specs.append(pl.BlockSpec((depth, k // n_steps, n), lambda i, j: (0, i * grid[1] + j, 0)))
        shapes.append(jax.ShapeDtypeStruct(w.shape, BF16))
    return specs, shapes


def _side_cast(in_refs, out_refs, scales):
    for r_in, r_out, scale in zip(in_refs, out_refs, scales):
        w = r_in[...]
        r_out[...] = (w if scale == 1.0 else w * scale).astype(r_out.dtype)


def _layer_norm(y, g, b):
    mu = jnp.mean(y, axis=-1, keepdims=True)
    yc = y - mu
    var = jnp.mean(yc * yc, axis=-1, keepdims=True)
    return yc * lax.rsqrt(var + LN_EPS) * g + b


def _gelu_tanh(x):
    return 0.5 * x * (1.0 + jnp.tanh(0.7978845608028654 * (x + 0.044715 * (x * x * x))))


def _head_half_mask(width, half):
    lane = lax.broadcasted_iota(jnp.int32, (1, width), 1)
    return (lane // HEAD_DIM) % 2 == half


def _inproj_kernel(*refs, att_width, sgu_width, cast_scales):
    n_cast = len(cast_scales)
    x_ref, w_ref, cos_ref, sin_lo_ref, sin_hi_ref, g_ref, b_ref = refs[:7]
    q_ref, kv0_ref, kv1_ref, kv2_ref, u_ref, gv_ref = refs[7 + n_cast:13 + n_cast]
    _side_cast(refs[7:7 + n_cast], refs[13 + n_cast:], cast_scales)
    xb = x_ref[0].astype(BF16)
    cos = cos_ref[...]
    sin_lo = sin_lo_ref[...]
    sin_hi = sin_hi_ref[...]

    def proj(c0, width):
        return jnp.dot(xb, w_ref[:, c0:c0 + width], preferred_element_type=F32)

    def rope(t):
        return t * cos + pltpu.roll(t, LANES - ROT_DIM // 2, 1) * sin_lo + pltpu.roll(t, ROT_DIM // 2, 1) * sin_hi

    def rope_group(t):
        return jnp.concatenate([rope(t[:, s:s + LANES]) for s in range(0, GROUP_WIDTH, LANES)], axis=1)

    u_ref[0] = _gelu_tanh(proj(3 * att_width, sgu_width)).astype(u_ref.dtype)
    gate = _gelu_tanh(proj(3 * att_width + sgu_width, sgu_width))
    gv_ref[0] = _layer_norm(gate, g_ref[...], b_ref[...]).astype(gv_ref.dtype)
    kv_refs = (kv0_ref, kv1_ref, kv2_ref)
    for grp, kv_ref in enumerate(kv_refs):
        c = grp * GROUP_WIDTH
        q_ref[0, :, c:c + GROUP_WIDTH] = rope_group(proj(c, GROUP_WIDTH))
        kv_ref[0, :, 0:GROUP_WIDTH] = rope_group(proj(att_width + c, GROUP_WIDTH))
    for grp, kv_ref in enumerate(kv_refs):
        kv_ref[0, :, GROUP_WIDTH:2 * GROUP_WIDTH] = proj(2 * att_width + grp * GROUP_WIDTH, GROUP_WIDTH)


def _inproj(x, w_in_b, layer, tables, g, b, tm, gate_dtype, casts=()):
    bk, tk, d = x.shape
    att_width = len(DILATIONS) * GROUP_WIDTH
    sgu_width = (w_in_b.shape[2] - 3 * att_width) // 2
    cos, sin_lo, sin_hi = tables
    grid = (bk, tk // tm)
    row = lambda w: pl.BlockSpec((1, tm, w), lambda i, j: (i, j, 0))
    tab = pl.BlockSpec((tm, LANES), lambda i, j: (j, 0))
    out_shape = ([jax.ShapeDtypeStruct((bk, tk, w), F32) for w in (att_width,) + (2 * GROUP_WIDTH,) * len(DILATIONS)]
                 + [jax.ShapeDtypeStruct((bk, tk, sgu_width), gate_dtype)] * 2)
    cast_specs, cast_shapes = _side_cast_specs(casts, grid)
    return pl.pallas_call(
        functools.partial(_inproj_kernel, att_width=att_width, sgu_width=sgu_width,
                          cast_scales=tuple(s for _, s in casts)),
        grid=grid,
        in_specs=[row(d), _layer_spec(w_in_b, layer), tab, tab, tab, _layer_spec(g, layer), _layer_spec(b, layer)]
        + cast_specs,
        out_specs=[row(s.shape[2]) for s in out_shape] + cast_specs,
        out_shape=out_shape + cast_shapes,
        compiler_params=_params(2),
        name="inproj",
    )(x, w_in_b, cos, sin_lo, sin_hi, g, b, *(w for w, _ in casts))


def _rope_tables(pos):
    half = ROT_DIM // 2
    inv = ROPE_THETA ** (-jnp.arange(half, dtype=F32) / half)
    ang = pos.astype(F32)[:, None] * inv[None, :]
    cos, sin = jnp.cos(ang), jnp.sin(ang)
    zeros = jnp.zeros((pos.shape[0], HEAD_DIM - ROT_DIM), F32)
    zero_half = jnp.zeros_like(sin)
    cos_head = jnp.concatenate([cos, cos, zeros + 1.0], axis=1)
    lo_head = jnp.concatenate([-sin, zero_half, zeros], axis=1)
    hi_head = jnp.concatenate([zero_half, sin, zeros], axis=1)
    two = lambda t: jnp.concatenate([t, t], axis=1)
    return two(cos_head), two(lo_head), two(hi_head)


ATTN_MIX_ROWS = 256
ATTN_TILES_PER_BATCH = 32


def _attn_kernel(*refs, seq, with_memkv):
    n_groups = len(DILATIONS)
    q_refs, k_refs, v_refs = refs[0:n_groups], refs[n_groups:2 * n_groups], refs[2 * n_groups:3 * n_groups]
    n_job = 2 if with_memkv else 0
    att_refs = refs[3 * n_groups + n_job:4 * n_groups + n_job]
    o_s, m_s, den_s = refs[4 * n_groups + 2 * n_job:]
    blk = WINDOW_STEPS
    x_idx = lax.broadcasted_iota(jnp.int32, (blk, 2 * blk), 0)
    k_idx = lax.broadcasted_iota(jnp.int32, (blk, 2 * blk), 1)
    band_mask = (k_idx >= x_idx) & (k_idx <= x_idx + blk)
    causal_mask = (lax.broadcasted_iota(jnp.int32, (blk, blk), 1)
                   <= lax.broadcasted_iota(jnp.int32, (blk, blk), 0))
    half_masks = [_head_half_mask(LANES, hh) for hh in range(2)]

    def batch(g, dil, blocks):
        def rows(r, first_block, n):
            start = r + dil * blk * first_block
            return pl.ds(start, n) if dil == 1 else pl.ds(start, n, stride=dil)

        first = blocks[0][1] == 0
        mask = causal_mask if first else band_mask
        scores, values = [], []
        for r, c in blocks:
            k_rows = rows(r, 0, blk) if first else rows(r, c - 1, 2 * blk)
            qp = q_refs[g][0, rows(r, c, blk), :]
            kp = k_refs[g][0, k_rows, :].astype(BF16)
            values.append(v_refs[g][0, k_rows, :].astype(BF16))
            for hh in range(2):
                qh = jnp.where(half_masks[hh], qp, 0.0).astype(BF16)
                s = lax.dot_general(qh, kp, (((1,), (1,)), ((), ())), preferred_element_type=F32)
                scores.append(jnp.where(mask, s, NEG))
        s_all = jnp.concatenate(scores, axis=0)
        m = jnp.max(s_all, axis=-1, keepdims=True)
        p = jnp.exp2(s_all - m)
        den = jnp.sum(p, axis=-1, keepdims=True)
        p = p.astype(BF16)
        for n, (r, c) in enumerate(blocks):
            piece = lambda t, hh: t[(2 * n + hh) * blk:(2 * n + hh + 1) * blk]
            outs = [jnp.dot(piece(p, hh), values[n], preferred_element_type=F32) for hh in range(2)]
            o_s[g, rows(r, c, blk), :] = jnp.where(half_masks[0], outs[0], outs[1])
            m_s[g, rows(r, c, blk), :] = jnp.where(half_masks[0], piece(m, 0), piece(m, 1))
            den_s[g, rows(r, c, blk), :] = jnp.where(half_masks[0], piece(den, 0), piece(den, 1))

    for g, dil in enumerate(DILATIONS):
        n_blocks = seq // dil // blk
        first_blocks = [(r, 0) for r in range(dil)]
        later_blocks = [(r, c) for r in range(dil) for c in range(1, n_blocks)]
        per_first = ATTN_TILES_PER_BATCH // 2
        per_later = ATTN_TILES_PER_BATCH // 4
        for i in range(0, len(first_blocks), per_first):
            batch(g, dil, first_blocks[i:i + per_first])
        for i in range(0, len(later_blocks), per_later):
            batch(g, dil, later_blocks[i:i + per_later])

    if with_memkv:
        _memkv_rows(*refs[3 * n_groups:3 * n_groups + n_job], *refs[4 * n_groups + n_job:4 * n_groups + 2 * n_job])

    for t0 in range(0, seq, ATTN_MIX_ROWS):
        rws = slice(t0, t0 + ATTN_MIX_ROWS)
        tops = [m_s[g, rws, :] for g in range(n_groups)]
        peak = functools.reduce(jnp.maximum, tops)
        scales = [jnp.exp2(t - peak) for t in tops]
        inv = 1.0 / functools.reduce(jnp.add, [den_s[g, rws, :] * scales[g] for g in range(n_groups)])
        for g in range(n_groups):
            att_refs[g][0, rws, :] = (o_s[g, rws, :] * (scales[g] * inv)).astype(att_refs[g].dtype)


def _prompt_attention(q, kvs, memkv_job=None):
    bk, tk, _ = q.shape
    n_groups = len(DILATIONS)
    pairs = GROUP_WIDTH // LANES
    col = lambda first: pl.BlockSpec((1, tk, LANES), lambda i, j: (i, 0, first + j))
    out = jax.ShapeDtypeStruct((bk, tk, GROUP_WIDTH), BF16)
    job_in, job_out, job_shapes = _memkv_specs(*memkv_job, n_parts=pairs) if memkv_job else ([], [], [])
    return pl.pallas_call(
        functools.partial(_attn_kernel, seq=tk, with_memkv=bool(memkv_job)),
        grid=(bk, pairs),
        in_specs=([col(g * pairs) for g in range(n_groups)]
                  + [col(0)] * n_groups
                  + [col(pairs)] * n_groups
                  + job_in),
        out_specs=[col(0)] * n_groups + job_out,
        out_shape=[out] * n_groups + job_shapes,
        scratch_shapes=[pltpu.VMEM((n_groups, tk, LANES), F32)] * 3,
        compiler_params=_params(2),
        name="attn",
    )(*([q] * n_groups), *kvs, *kvs, *(memkv_job or ()))


def _sample_attn_kernel(q_ref, kvn0_ref, kvn1_ref, kvn2_ref, c0_ref, c1_ref, c2_ref, att_ref, *, dec_seq):
    kvn_refs = (kvn0_ref, kvn1_ref, kvn2_ref)
    cache_refs = (c0_ref, c1_ref, c2_ref)
    seqs_per_slab = SUBLANES // dec_seq
    n_groups = len(DILATIONS)
    widest = DILATIONS[-1]
    slab = pl.ds(pl.multiple_of(pl.program_id(0) * SUBLANES, SUBLANES), SUBLANES)
    lane = lax.broadcasted_iota(jnp.int32, (1, LANES), 1)
    row8 = lax.broadcasted_iota(jnp.int32, (SUBLANES, LANES), 0)
    sel_rows = 2 * SUBLANES
    sel_row = lax.broadcasted_iota(jnp.int32, (sel_rows, LANES), 0)
    sel_lane = lax.broadcasted_iota(jnp.int32, (sel_rows, LANES), 1)
    query_class = lane % widest

    def one_hot(cond):
        return jnp.where(cond, 1.0, 0.0).astype(BF16)

    def spread(rows8, sel):
        padded = jnp.concatenate([rows8, jnp.zeros_like(rows8)], axis=0).astype(BF16)
        return lax.dot_general(padded, sel, (((0,), (0,)), ((), ())), preferred_element_type=F32)

    def class_fold(x, period, op):
        shift = LANES // 2
        while shift >= period:
            x = op(x, pltpu.roll(x, shift, 1))
            shift //= 2
        return x

    def valid_mask(n_tiles, cache_ok, new_ok):
        rows = SUBLANES * (n_tiles // SUBLANES + 1)
        r = lax.broadcasted_iota(jnp.int32, (rows, LANES), 0)
        p = lax.broadcasted_iota(jnp.int32, (rows, LANES), 1)
        return ((r < n_tiles) & cache_ok(p)) | ((r == n_tiles) & new_ok(p))

    def attend(qm, cache_ref, seq, head, knt, vnt, n_tiles, valid, period):
        groups = []
        for t0 in range(0, n_tiles + 1, SUBLANES):
            acc = jnp.full((SUBLANES, LANES), NEG, F32)
            for t in range(t0, min(t0 + SUBLANES, n_tiles + 1)):
                keys = knt if t == n_tiles else cache_ref[0, seq, 0, head, :, t * LANES:(t + 1) * LANES]
                acc = jnp.where(row8 == t - t0, jnp.sum(qm * keys, axis=0, keepdims=True), acc)
            groups.append(acc)
        scores = jnp.where(valid, groups[0] if len(groups) == 1 else jnp.concatenate(groups, axis=0), NEG)
        top = jnp.broadcast_to(jnp.max(scores, axis=0, keepdims=True), (SUBLANES, LANES))
        if period:
            top = class_fold(top, period, jnp.maximum)
        else:
            top = jnp.broadcast_to(jnp.max(top, axis=1, keepdims=True), (SUBLANES, LANES))
        probs = jnp.exp2(scores - jnp.concatenate([top] * len(groups), axis=0))
        den = jnp.broadcast_to(jnp.sum(probs, axis=0, keepdims=True), (SUBLANES, LANES))
        num = jnp.zeros((HEAD_DIM, LANES), F32)
        for t in range(n_tiles + 1):
            vals = vnt if t == n_tiles else cache_ref[0, seq, 1, head, :, t * LANES:(t + 1) * LANES]
            num = num + vals * probs[t:t + 1, :]
        if period:
            return class_fold(num, period, jnp.add), top, class_fold(den, period, jnp.add)
        return (jnp.broadcast_to(jnp.sum(num, axis=1, keepdims=True), (HEAD_DIM, LANES)), top,
                jnp.broadcast_to(jnp.sum(den, axis=1, keepdims=True), (SUBLANES, LANES)))

    for pair in range(GROUP_WIDTH // LANES):
        mixed = [[None] * seqs_per_slab for _ in range(n_groups)]
        for e in range(seqs_per_slab):
            base = e * dec_seq
            sel_new = one_hot((sel_row == base + sel_lane) & (sel_lane < dec_seq))
            per_group = []
            for g, dil in enumerate(DILATIONS):
                off = pair * LANES
                knt = spread(kvn_refs[g][0, slab, off:off + LANES], sel_new)
                vnt = spread(kvn_refs[g][0, slab, GROUP_WIDTH + off:GROUP_WIDTH + off + LANES], sel_new)
                q8 = q_ref[slab, g * GROUP_WIDTH + off:g * GROUP_WIDTH + off + LANES]
                n_tiles = cache_refs[g].shape[5] // LANES
                if dil == 1:
                    qms = [spread(q8, one_hot(sel_row == base + i)) for i in range(dec_seq)]
                    valids = [valid_mask(n_tiles, lambda p, i=i: p >= i, lambda p, i=i: p <= i) for i in range(dec_seq)]
                else:
                    qms = [spread(q8, one_hot((sel_row == base + sel_lane % dil) & (sel_lane % dil < dec_seq)))]
                    valids = [valid_mask(n_tiles, lambda p, dil=dil: p % dil < dec_seq, lambda p: p < dec_seq)]
                heads = []
                for hh in range(2):
                    hrows = slice(hh * HEAD_DIM, (hh + 1) * HEAD_DIM)
                    num = top = den = None
                    for i, (qm, valid) in enumerate(zip(qms, valids)):
                        n_i, t_i, d_i = attend(qm[hrows], cache_refs[g], e, 2 * pair + hh, knt[hrows], vnt[hrows],
                                               n_tiles, valid, dil if dil > 1 else 0)
                        if num is None:
                            num, top, den = n_i, t_i, d_i
                        else:
                            mine = query_class == i
                            num, top, den = jnp.where(mine, n_i, num), jnp.where(mine, t_i, top), jnp.where(mine, d_i, den)
                    heads.append([num, top, den])
                per_group.append(heads)
            for hh in range(2):
                tops = [per_group[g][hh][1] for g in range(n_groups)]
                peak = functools.reduce(jnp.maximum, tops)
                scales = [jnp.exp2(t - peak) for t in tops]
                total = functools.reduce(jnp.add, [per_group[g][hh][2] * scales[g] for g in range(n_groups)])
                for g in range(n_groups):
                    per_group[g][hh] = per_group[g][hh][0] * (scales[g] / total)[0:1, :]
            for g in range(n_groups):
                mixed[g][e] = jnp.concatenate(per_group[g], axis=0)
        sel_out = one_hot(sel_lane == widest * (sel_row // dec_seq) + sel_row % dec_seq)
        for g in range(n_groups):
            both = mixed[g][0]
            for e in range(1, seqs_per_slab):
                both = jnp.where((lane >= widest * e) & (lane < widest * (e + 1)), mixed[g][e], both)
            hi = both.astype(BF16)
            lo = (both - hi.astype(F32)).astype(BF16)
            dims = (((1,), (1,)), ((), ()))
            rows = (lax.dot_general(sel_out, hi, dims, preferred_element_type=F32)
                    + lax.dot_general(sel_out, lo, dims, preferred_element_type=F32))
            att_ref[slab, g * GROUP_WIDTH + pair * LANES:g * GROUP_WIDTH + (pair + 1) * LANES] = rows[:SUBLANES]


def _sample_attention(q, kvns, caches_t, layer, dec_seq):
    rows, width = q.shape
    seqs_per_slab = SUBLANES // dec_seq
    cache_spec = lambda c: pl.BlockSpec((1, seqs_per_slab) + c.shape[2:], lambda i: (layer, i, 0, 0, 0, 0))
    return pl.pallas_call(
        functools.partial(_sample_attn_kernel, dec_seq=dec_seq),
        grid=(rows // SUBLANES,),
        in_specs=[_const_spec(q.shape)] + [_const_spec(k.shape) for k in kvns] + [cache_spec(c) for c in caches_t],
        out_specs=pl.BlockSpec((rows, width), lambda i: (0, 0)),
        out_shape=jax.ShapeDtypeStruct((rows, width), F32),
        compiler_params=_params(1),
        name="sample_attn",
    )(q, *kvns, *caches_t)


MIX_SUBTILE = 1024


def _mix_kernel(*refs, tm, alpha, n_att, cross_attend, cast_scales):
    n_cast = len(cast_scales)
    u_ref, gv_ref, x_ref, wsp_ref, bsp_ref, wm_ref, g_ref, b_ref = refs[n_att:n_att + 8]
    n_cross = 5 if cross_attend else 0
    xattn_refs = refs[n_att + 8:n_att + 8 + n_cross]
    cast_in_refs = refs[n_att + 8 + n_cross:n_att + 8 + n_cross + n_cast]
    out_ref = refs[n_att + 8 + n_cross + n_cast]
    _side_cast(cast_in_refs, refs[n_att + 9 + n_cross + n_cast:], cast_scales)
    t_idx = lax.broadcasted_iota(jnp.int32, (SGU_CHUNK, SGU_CHUNK), 0)
    s_idx = lax.broadcasted_iota(jnp.int32, (SGU_CHUNK, SGU_CHUNK), 1)
    w_sp = [jnp.where(s_idx <= t_idx, wsp_ref[k], 0.0).astype(BF16) for k in range(wsp_ref.shape[0])]
    sgu_width = gv_ref.shape[2]

    def gated_rows(r0, n_rows):
        starts = range(r0, r0 + n_rows, SGU_CHUNK)
        per_pair = []
        for pair in range(sgu_width // LANES):
            gps = [gv_ref[0, c:c + SGU_CHUNK, pair * LANES:(pair + 1) * LANES] for c in starts]
            mixed = None
            for hh in range(2):
                wide = jnp.concatenate([jnp.where(_head_half_mask(LANES, hh), gp, 0.0).astype(BF16) for gp in gps], axis=1)
                part = jnp.dot(w_sp[2 * pair + hh], wide, preferred_element_type=F32)
                mixed = part if mixed is None else mixed + part
            per_pair.append(mixed)
        mixed = jnp.concatenate(
            [jnp.concatenate([m[:, n * LANES:(n + 1) * LANES] for m in per_pair], axis=1) for n in range(len(starts))],
            axis=0)
        bias = jnp.concatenate([bsp_ref[...]] * len(starts), axis=0)
        return (u_ref[0, r0:r0 + n_rows, :] * (mixed + bias)).astype(BF16)

    sub = min(tm, MIX_SUBTILE)
    for r0 in range(0, tm, sub):
        rws = slice(r0, r0 + sub)
        sgu = gated_rows(r0, sub)
        lhs = jnp.concatenate([a[0, rws, :].astype(BF16) for a in refs[:n_att]] + [sgu], axis=1)
        y = alpha * x_ref[0, rws, :] + jnp.dot(lhs, wm_ref[...], preferred_element_type=F32)
        x1 = _layer_norm(y, g_ref[...], b_ref[...])
        out_ref[0, rws, :] = _xattn_rows(x1, *xattn_refs, alpha) if cross_attend else x1


def _mix(att_inputs, u, gv, x, w_sp, b_sp_tile, w_mix_b, layer, g, b, tm, alpha, cross=None, casts=()):
    bk, tk, d = x.shape
    sgu_width = u.shape[2]
    grid = (bk, tk // tm)
    row = lambda w: pl.BlockSpec((1, tm, w), lambda i, j: (i, j, 0))
    cross_specs = []
    if cross:
        mkv = cross[0]
        cross_specs = ([pl.BlockSpec((None, 1) + mkv.shape[2:], lambda i, j: (layer, i, 0, 0))]
                       + [_layer_spec(p, layer) for p in cross[1:]])
    cast_specs, cast_shapes = _side_cast_specs(casts, grid)
    return pl.pallas_call(
        functools.partial(_mix_kernel, tm=tm, alpha=alpha, n_att=len(att_inputs), cross_attend=bool(cross),
                          cast_scales=tuple(s for _, s in casts)),
        grid=grid,
        in_specs=[row(a.shape[2]) for a in att_inputs] + [
            row(sgu_width), row(sgu_width), row(d),
            _const_spec(w_sp.shape), _const_spec(b_sp_tile.shape), _layer_spec(w_mix_b, layer),
            _layer_spec(g, layer), _layer_spec(b, layer)] + cross_specs + cast_specs,
        out_specs=[row(d)] + cast_specs,
        out_shape=[jax.ShapeDtypeStruct((bk, tk, d), F32)] + cast_shapes,
        compiler_params=_params(2),
        name="mix_xattn" if cross else "mix",
    )(*att_inputs, u, gv, x, w_sp, b_sp_tile, w_mix_b, g, b, *(cross or ()), *(w for w, _ in casts))


def _mem_row_stride(d):
    return 2 * (d // MEM_HEADS // LANES) * MEM_HEADS


def _memkv_rows(m_ref, w_ref, flat_ref, out_b_ref):
    depth, _, n_mem, width = out_b_ref.shape
    d = width // 2
    hd = d // MEM_HEADS
    n_chunks = hd // LANES
    mb = m_ref[0].astype(BF16)
    for layer in range(depth):
        mkv = jnp.dot(mb, w_ref[layer], preferred_element_type=F32)
        out_b_ref[layer, 0] = mkv.astype(BF16)
        for kv in range(2):
            for c in range(n_chunks):
                for h in range(MEM_HEADS):
                    col = kv * d + h * hd + c * LANES
                    rows = pl.ds((kv * n_chunks + c) * MEM_HEADS + h, n_mem, stride=_mem_row_stride(d))
                    flat_ref[layer, rows, :] = mkv[:, col:col + LANES]


def _memkv_specs(mem, w_xkv_b, n_parts):
    bk, n_mem, d = mem.shape
    depth, _, width = w_xkv_b.shape
    part = n_mem // n_parts
    return ([pl.BlockSpec((1, part, d), lambda i, j: (i, j, 0)), _const_spec(w_xkv_b.shape)],
            [pl.BlockSpec((depth, None, part * _mem_row_stride(d), LANES), lambda i, j: (0, i, j, 0)),
             pl.BlockSpec((depth, 1, part, width), lambda i, j: (0, i, j, 0))],
            [jax.ShapeDtypeStruct((depth, bk, n_mem * _mem_row_stride(d), LANES), F32),
             jax.ShapeDtypeStruct((depth, bk, n_mem, width), BF16)])


def _softmax_rows(s):
    m = jnp.max(s, axis=-1, keepdims=True)
    p = jnp.exp(s - m)
    return p * (1.0 / jnp.sum(p, axis=-1, keepdims=True))


def _xattn_rows(x, mkv_ref, wq_ref, wo_ref, g_ref, b_ref, alpha):
    d = x.shape[1]
    hd = d // MEM_HEADS
    qx = jnp.dot(x.astype(BF16), wq_ref[...], preferred_element_type=F32)
    heads = []
    for h in range(MEM_HEADS):
        qh = qx[:, h * hd:(h + 1) * hd].astype(BF16)
        kh = mkv_ref[0, :, h * hd:(h + 1) * hd]
        vh = mkv_ref[0, :, d + h * hd:d + (h + 1) * hd]
        s = lax.dot_general(qh, kh, (((1,), (1,)), ((), ())), preferred_element_type=F32)
        m = jnp.max(s, axis=-1, keepdims=True)
        p = jnp.exp(s - m)
        inv = 1.0 / jnp.sum(p, axis=-1, keepdims=True)
        heads.append((jnp.dot(p.astype(BF16), vh, preferred_element_type=F32) * inv).astype(BF16))
    y = alpha * x + jnp.dot(jnp.concatenate(heads, axis=1), wo_ref[...], preferred_element_type=F32)
    return _layer_norm(y, g_ref[...], b_ref[...])


def _sample_xattn_kernel(x_ref, mkv_ref, wq_ref, wo_ref, g_ref, b_ref, out_ref, qx_ref, ox_ref,
                         *, dec_seq, seqs_per_step, alpha):
    step = pl.program_id(0)
    d = x_ref.shape[1]
    hd = d // MEM_HEADS
    seqs_per_slab = SUBLANES // dec_seq

    @pl.when(step == 0)
    def _():
        qx_ref[...] = jnp.dot(x_ref[...].astype(BF16), wq_ref[...], preferred_element_type=F32)

    n_chunks = hd // LANES
    rows_per_mem = 2 * n_chunks * MEM_HEADS
    n_mem = mkv_ref.shape[2] // rows_per_mem

    def head_matrix(b_local, kv, h):
        chunks = [mkv_ref[0, b_local, pl.ds((kv * n_chunks + c) * MEM_HEADS + h, n_mem, stride=rows_per_mem), :]
                  for c in range(n_chunks)]
        return jnp.concatenate(chunks, axis=1).astype(BF16)

    slab_row = lax.broadcasted_iota(jnp.int32, (SUBLANES, 1), 0)
    slabs = [pl.ds(pl.multiple_of((step * (seqs_per_step // seqs_per_slab) + j) * SUBLANES, SUBLANES), SUBLANES)
             for j in range(seqs_per_step // seqs_per_slab)]
    units = [(j, h, e) for j in range(len(slabs)) for h in range(MEM_HEADS) for e in range(seqs_per_slab)]
    scores = []
    for j, h, e in units:
        if e == 0:
            q8 = qx_ref[slabs[j], h * hd:(h + 1) * hd]
            qh = jnp.concatenate([q8, jnp.zeros_like(q8)], axis=0).astype(BF16)
        scores.append(lax.dot_general(head_matrix(j * seqs_per_slab + e, 0, h), qh, (((1,), (1,)), ((), ())),
                                      preferred_element_type=F32))
    s_all = jnp.stack(scores)
    p_all = jnp.exp(s_all - jnp.max(s_all, axis=1, keepdims=True))
    p_all = (p_all * (1.0 / jnp.sum(p_all, axis=1, keepdims=True))).astype(BF16)
    o8 = None
    for n, (j, h, e) in enumerate(units):
        oh = lax.dot_general(p_all[n], head_matrix(j * seqs_per_slab + e, 1, h), (((0,), (0,)), ((), ())),
                             preferred_element_type=F32)
        o8 = oh[:SUBLANES] if e == 0 else jnp.where(slab_row // dec_seq == e, oh[:SUBLANES], o8)
        if e == seqs_per_slab - 1:
            ox_ref[slabs[j], h * hd:(h + 1) * hd] = o8

    @pl.when(step == pl.num_programs(0) - 1)
    def _():
        y = alpha * x_ref[...] + jnp.dot(ox_ref[...].astype(BF16), wo_ref[...], preferred_element_type=F32)
        out_ref[...] = _layer_norm(y, g_ref[...], b_ref[...])


def _sample_xattn(x, mkv_flat, layer, wq_b, wo_b, g, b, dec_seq, seqs_per_step, alpha):
    rows, d = x.shape
    _, n_seq, flat_rows, _ = mkv_flat.shape
    return pl.pallas_call(
        functools.partial(_sample_xattn_kernel, dec_seq=dec_seq, seqs_per_step=seqs_per_step, alpha=alpha),
        grid=(n_seq // seqs_per_step,),
        in_specs=[_const_spec(x.shape),
                  pl.BlockSpec((1, seqs_per_step, flat_rows, LANES), lambda i: (layer, i, 0, 0)),
                  _layer_spec(wq_b, layer), _layer_spec(wo_b, layer), _layer_spec(g, layer), _layer_spec(b, layer)],
        out_specs=pl.BlockSpec((rows, d), lambda i: (0, 0)),
        out_shape=jax.ShapeDtypeStruct((rows, d), F32),
        scratch_shapes=[pltpu.VMEM((rows, d), F32), pltpu.VMEM((rows, d), F32)],
        compiler_params=_params(1),
        name="sample_xattn",
    )(x, mkv_flat, wq_b, wo_b, g, b)


MLP_SUBTILE = 512


def _mlp_kernel(*refs, ff_chunk, alpha, windows, first_layer):
    n_win = len(windows)
    x_ref, wu_ref, wd_ref, g_ref, b_ref = refs[:5]
    kv_refs = refs[5:5 + n_win]
    out_ref = refs[-1 - n_win]
    window_refs = list(refs[len(refs) - n_win:])

    def write_window(grp):
        win_ref = window_refs[grp] if first_layer is None else window_refs[grp].at[first_layer]
        for part in range(2):
            t = kv_refs[grp][0, :, part * GROUP_WIDTH:(part + 1) * GROUP_WIDTH].T
            for h in range(HEADS_PER_GROUP):
                win_ref[part, h] = t[h * HEAD_DIM:(h + 1) * HEAD_DIM, :]
        if first_layer is not None:
            for other in range(window_refs[grp].shape[0]):
                if other != first_layer:
                    window_refs[grp][other] = jnp.zeros(window_refs[grp].shape[1:], F32)

    for grp, every_tile in enumerate(windows):
        if not every_tile:
            pl.when(pl.program_id(1) == pl.num_programs(1) - 1)(functools.partial(write_window, grp))
    for grp, every_tile in enumerate(windows):
        if every_tile:
            write_window(grp)

    tm = x_ref.shape[1]
    sub = min(tm, MLP_SUBTILE)
    for r0 in range(0, tm, sub):
        x = x_ref[0, r0:r0 + sub, :]
        xb = x.astype(BF16)
        y = alpha * x
        for c in range(wu_ref.shape[1] // ff_chunk):
            h = jnp.dot(xb, wu_ref[:, c * ff_chunk:(c + 1) * ff_chunk], preferred_element_type=F32)
            h = jnp.square(jnp.maximum(h, 0.0)).astype(BF16)
            y = y + jnp.dot(h, wd_ref[c * ff_chunk:(c + 1) * ff_chunk, :], preferred_element_type=F32)
        out_ref[0, r0:r0 + sub, :] = _layer_norm(y, g_ref[...], b_ref[...])


def _mlp(x, wu_b, wd_b, layer, g, b, tm, alpha, ff_chunk=1024, kvs=(), window_keeps=(), window_bufs=None):
    bk, tk, d = x.shape
    depth = wu_b.shape[0]
    row = pl.BlockSpec((1, tm, d), lambda i, j: (i, j, 0))
    aliased = list(window_bufs or ())
    owns_all_layers = bool(window_keeps) and not aliased
    lead, at = (depth, 0) if owns_all_layers else (None, layer)
    kv_specs, out_shape, out_specs, windows = [], [jax.ShapeDtypeStruct((bk, tk, d), F32)], [row], []
    for kv, keep in zip(kvs, window_keeps):
        every_tile = keep == tk
        assert every_tile or tk % keep == 0
        windows.append(every_tile)
        n = tm if every_tile else keep
        kv_specs.append(pl.BlockSpec((1, n, kv.shape[2]),
                                     (lambda i, j: (i, j, 0)) if every_tile else (lambda i, j, last=tk // keep - 1: (i, last, 0))))
        out_shape.append(jax.ShapeDtypeStruct((depth, bk, 2, HEADS_PER_GROUP, HEAD_DIM, keep), F32))
        out_specs.append(pl.BlockSpec((lead, None, 2, HEADS_PER_GROUP, HEAD_DIM, n),
                                      (lambda i, j: (at, i, 0, 0, 0, j)) if every_tile else (lambda i, j: (at, i, 0, 0, 0, 0))))
    n_in = 5 + len(kv_specs)
    return pl.pallas_call(
        functools.partial(_mlp_kernel, ff_chunk=ff_chunk, alpha=alpha, windows=tuple(windows),
                          first_layer=layer if owns_all_layers else None),
        grid=(bk, tk // tm),
        in_specs=[row, _layer_spec(wu_b, layer), _layer_spec(wd_b, layer), _layer_spec(g, layer), _layer_spec(b, layer)]
        + kv_specs + [pl.BlockSpec(memory_space=pl.ANY)] * len(aliased),
        out_specs=out_specs,
        out_shape=out_shape,
        input_output_aliases={n_in + k: 1 + k for k in range(len(aliased))},
        compiler_params=_params(2),
        name="mlp",
    )(x, wu_b, wd_b, g, b, *kvs, *aliased)


def kernel(x_prompt, x_sample, cache_kv_w128, cache_kv_w512, cache_kv_w2048, cache_mem_kv, mem_prompt,
           w_in, sgu_ln_g, sgu_ln_b, w_spatial, b_spatial, w_mix_out, ln1_g, ln1_b,
           w_xq, w_xkv, w_xo, ln2_g, ln2_b, w_up, w_down, ln3_g, ln3_b):
    depth = w_in.shape[0]
    bp, tp, d = x_prompt.shape
    bs, ts, _ = x_sample.shape
    past_len = 8192
    alpha = float((2 * depth) ** 0.25)
    att_width = len(DILATIONS) * GROUP_WIDTH
    sgu_width = sgu_ln_g.shape[1]
    sgu_groups = w_spatial.shape[1]
    n_mem = mem_prompt.shape[1]
    rows_s = bs * ts
    assert tp % (DILATIONS[-1] * WINDOW_STEPS) == 0 and SUBLANES % ts == 0 and rows_s % SGU_CHUNK == 0
    assert d // MEM_HEADS == GROUP_WIDTH and sgu_width == GROUP_WIDTH

    col_scale = jnp.concatenate([jnp.full((att_width,), HEAD_DIM ** -0.5 * math.log2(math.e), F32),
                                 jnp.ones((w_in.shape[2] - att_width,), F32)])
    w_in_b = (w_in * col_scale).astype(BF16)
    w_xkv_b = w_mix_b = w_xq_b = w_xo_b = w_up_b = w_down_b = None
    vec = lambda p: p[:, None, :]
    sgu_g, sgu_b, g1, b1, g2, b2, g3, b3 = map(vec, (sgu_ln_g, sgu_ln_b, ln1_g, ln1_b, ln2_g, ln2_b, ln3_g, ln3_b))

    tables_p = _rope_tables(jnp.arange(tp, dtype=jnp.int32))
    tables_s = _rope_tables(past_len + (jnp.arange(rows_s, dtype=jnp.int32) % ts))

    row_s = jnp.arange(rows_s, dtype=jnp.int32)
    pick = (row_s[:, None] % ts == jnp.arange(ts)[None, :]).astype(F32)
    same_seq = (row_s[:, None] // ts == row_s[None, :] // ts).astype(F32)
    w_sp_sample = jnp.einsum("ra,lgab,cb->lgrc", pick, w_spatial[:, :, :ts, :ts], pick,
                             precision=lax.Precision.HIGHEST) * same_seq

    hp = x_prompt
    hs = x_sample.reshape(1, rows_s, d)
    out_rows_s = [[] for _ in DILATIONS]
    out_gv = []
    window_keeps = tuple(min(dil * WINDOW_STEPS, tp) for dil in DILATIONS)
    windows = mem_out = mkv = None
    caches_t = [jnp.transpose(c, (0, 1, 3, 4, 5, 2)) for c in (cache_kv_w128, cache_kv_w512, cache_kv_w2048)]
    mem_chunks = d // MEM_HEADS // LANES
    mem_flat = (cache_mem_kv.reshape(depth, bs, n_mem, 2, MEM_HEADS, mem_chunks, LANES)
                .transpose(0, 1, 2, 3, 5, 4, 6).reshape(depth, bs, n_mem * 2 * mem_chunks * MEM_HEADS, LANES))
    for l in range(depth):
        first = l == 0
        q, kv0, kv1, kv2, u, gv, *rounded = _inproj(
            hp, w_in_b, l, tables_p, sgu_g, sgu_b, tm=1024, gate_dtype=BF16,
            casts=((w_mix_out, 1.0), (w_xq, (d // MEM_HEADS) ** -0.5), (w_xo, 1.0), (w_xkv, 1.0)) if first else ())
        if first:
            w_mix_b, w_xq_b, w_xo_b, w_xkv_b = rounded
            *atts, mem_out, mkv = _prompt_attention(q, [kv0, kv1, kv2], memkv_job=(mem_prompt, w_xkv_b))
        else:
            atts = _prompt_attention(q, [kv0, kv1, kv2])
        b_tile = jnp.repeat(b_spatial[l][:, :SGU_CHUNK].T, sgu_width // sgu_groups, axis=1)
        hp, *rounded = _mix(atts, u, gv, hp, w_spatial[l], b_tile, w_mix_b, l, g1, b1, tm=1024, alpha=alpha,
                            cross=(mkv, w_xq_b, w_xo_b, g2, b2),
                            casts=((w_up, 1.0), (w_down, 1.0)) if first else ())
        if first:
            w_up_b, w_down_b = rounded
        hp, *windows = _mlp(hp, w_up_b, w_down_b, l, g3, b3, tm=512, alpha=alpha,
                            kvs=(kv0, kv1, kv2), window_keeps=window_keeps, window_bufs=windows)

        q, kv0, kv1, kv2, u, gv = _inproj(hs, w_in_b, l, tables_s, sgu_g, sgu_b, tm=rows_s, gate_dtype=F32)
        out_gv.append(gv.reshape(bs, ts, sgu_width))
        for g, kv in enumerate((kv0, kv1, kv2)):
            out_rows_s[g].append(kv.reshape(bs, ts, 2, HEADS_PER_GROUP, HEAD_DIM))
        att = _sample_attention(q[0], (kv0, kv1, kv2), caches_t, l, dec_seq=ts)
        b_tile_s = jnp.repeat(jnp.tile(b_spatial[l][:, :ts].T, (rows_s // ts, 1)), sgu_width // sgu_groups, axis=1)
        hs, = _mix([att[None]], u, gv, hs, w_sp_sample[l], b_tile_s, w_mix_b, l, g1, b1, tm=rows_s, alpha=alpha)
        hs2 = _sample_xattn(hs[0], mem_flat, l, w_xq_b, w_xo_b, g2, b2, dec_seq=ts, seqs_per_step=4, alpha=alpha)
        hs, = _mlp(hs2[None], w_up_b, w_down_b, l, g3, b3, tm=rows_s, alpha=alpha)

    rows_p = [jnp.transpose(w, (0, 1, 5, 2, 3, 4)) for w in windows]
    mem_p = (mem_out.reshape(depth, bp, n_mem, 2, mem_chunks, MEM_HEADS, LANES)
             .transpose(0, 1, 2, 3, 5, 4, 6).reshape(depth, bp, n_mem, 2, MEM_HEADS, d // MEM_HEADS))
    stack = lambda xs: jnp.stack(xs)
    return (hp, hs.reshape(bs, ts, d), rows_p[0], rows_p[1], rows_p[2], mem_p,
            stack(out_rows_s[0]), stack(out_rows_s[1]), stack(out_rows_s[2]), stack(out_gv))
```

```python
import functools
import math

import jax
import jax.numpy as jnp
from jax import lax
from jax.experimental import pallas as pl
from jax.experimental.pallas import tpu as pltpu

F32 = jnp.float32
BF16 = jnp.bfloat16

HEAD_DIM = 64
HEADS_PER_GROUP = 4
GROUP_WIDTH = HEAD_DIM * HEADS_PER_GROUP
DILATIONS = (1, 4, 16)
WINDOW_STEPS = 128
ROT_DIM = 16
ROPE_THETA = 500000.0
SGU_CHUNK = 128
MEM_HEADS = 4
LN_EPS = 1e-5
NEG = -1e30
LANES = 128
SUBLANES = 8
VMEM_LIMIT = 56 * 1024 * 1024

INPROJ_ROWS = 1024
MIX_ROWS = 1024
MIX_SUBTILE = 256
MLP_ROWS = 512
MLP_SUBTILE = 512
MLP_FF_CHUNK = 1024
ATTN_MIX_ROWS = 256
ATTN_TILES_PER_BATCH = 32
SAMPLE_XATTN_SEQS = 4


def _params(n_grid_dims):
    return pltpu.CompilerParams(
        dimension_semantics=("arbitrary",) * n_grid_dims,
        vmem_limit_bytes=VMEM_LIMIT)


def _const_spec(shape):
    nd = len(shape)
    return pl.BlockSpec(shape, lambda *_: (0,) * nd, pipeline_mode=pl.Buffered(1))


def _layer_spec(stacked, layer):
    rest = stacked.shape[1:]
    return pl.BlockSpec((None,) + rest, lambda *_: (layer,) + (0,) * len(rest), pipeline_mode=pl.Buffered(1))


def _side_cast_specs(casts, grid):
    n_steps = grid[0] * grid[1]
    specs, shapes = [], []
    for w, _ in casts:
        depth, k, n = w.shape
        specs.append(pl.BlockSpec((depth, k // n_steps, n), lambda i, j: (0, i * grid[1] + j, 0)))
        shapes.append(jax.ShapeDtypeStruct(w.shape, BF16))
    return specs, shapes


def _side_cast(in_refs, out_refs, scales):
    for r_in, r_out, scale in zip(in_refs, out_refs, scales):
        w = r_in[...]
        r_out[...] = (w if scale == 1.0 else w * scale).astype(r_out.dtype)


def _layer_norm(y, g, b):
    mu = jnp.mean(y, axis=-1, keepdims=True)
    yc = y - mu
    var = jnp.mean(yc * yc, axis=-1, keepdims=True)
    return yc * lax.rsqrt(var + LN_EPS) * g + b


def _gelu_tanh(x):
    return 0.5 * x * (1.0 + jnp.tanh(0.7978845608028654 * (x + 0.044715 * (x * x * x))))


def _head_half_mask(width, half):
    lane = lax.broadcasted_iota(jnp.int32, (1, width), 1)
    return (lane // HEAD_DIM) % 2 == half


def _inproj_kernel(*refs, att_width, sgu_width, cast_scales):
    n_cast = len(cast_scales)
    x_ref, w_ref, cos_ref, sin_lo_ref, sin_hi_ref, g_ref, b_ref = refs[:7]
    q_ref, kv0_ref, kv1_ref, kv2_ref, u_ref, gv_ref = refs[7 + n_cast:13 + n_cast]
    _side_cast(refs[7:7 + n_cast], refs[13 + n_cast:], cast_scales)
    xb = x_ref[0].astype(BF16)
    cos = cos_ref[...]
    sin_lo = sin_lo_ref[...]
    sin_hi = sin_hi_ref[...]

    def proj(c0, width):
        return jnp.dot(xb, w_ref[:, c0:c0 + width], preferred_element_type=F32)

    def rope(t):
        return t * cos + pltpu.roll(t, LANES - ROT_DIM // 2, 1) * sin_lo + pltpu.roll(t, ROT_DIM // 2, 1) * sin_hi

    def rope_group(t):
        return jnp.concatenate([rope(t[:, s:s + LANES]) for s in range(0, GROUP_WIDTH, LANES)], axis=1)

    u_ref[0] = _gelu_tanh(proj(3 * att_width, sgu_width)).astype(u_ref.dtype)
    gate = _gelu_tanh(proj(3 * att_width + sgu_width, sgu_width))
    gv_ref[0] = _layer_norm(gate, g_ref[...], b_ref[...]).astype(gv_ref.dtype)
    kv_refs = (kv0_ref, kv1_ref, kv2_ref)
    for grp, kv_ref in enumerate(kv_refs):
        c = grp * GROUP_WIDTH
        q_ref[0, :, c:c + GROUP_WIDTH] = rope_group(proj(c, GROUP_WIDTH))
        kv_ref[0, :, 0:GROUP_WIDTH] = rope_group(proj(att_width + c, GROUP_WIDTH))
    for grp, kv_ref in enumerate(kv_refs):
        kv_ref[0, :, GROUP_WIDTH:2 * GROUP_WIDTH] = proj(2 * att_width + grp * GROUP_WIDTH, GROUP_WIDTH)


def _inproj(x, w_in_b, layer, tables, g, b, tm, gate_dtype, casts=()):
    bk, tk, d = x.shape
    att_width = len(DILATIONS) * GROUP_WIDTH
    sgu_width = (w_in_b.shape[2] - 3 * att_width) // 2
    cos, sin_lo, sin_hi = tables
    grid = (bk, tk // tm)
    row = lambda w: pl.BlockSpec((1, tm, w), lambda i, j: (i, j, 0))
    tab = pl.BlockSpec((tm, LANES), lambda i, j: (j, 0))
    out_shape = ([jax.ShapeDtypeStruct((bk, tk, w), F32) for w in (att_width,) + (2 * GROUP_WIDTH,) * len(DILATIONS)]
                 + [jax.ShapeDtypeStruct((bk, tk, sgu_width), gate_dtype)] * 2)
    cast_specs, cast_shapes = _side_cast_specs(casts, grid)
    return pl.pallas_call(
        functools.partial(_inproj_kernel, att_width=att_width, sgu_width=sgu_width,
                          cast_scales=tuple(s for _, s in casts)),
        grid=grid,
        in_specs=[row(d), _layer_spec(w_in_b, layer), tab, tab, tab, _layer_spec(g, layer), _layer_spec(b, layer)]
        + cast_specs,
        out_specs=[row(s.shape[2]) for s in out_shape] + cast_specs,
        out_shape=out_shape + cast_shapes,
        compiler_params=_params(2),
        name="inproj",
    )(x, w_in_b, cos, sin_lo, sin_hi, g, b, *(w for w, _ in casts))


def _rope_tables(pos):
    half = ROT_DIM // 2
    inv = ROPE_THETA ** (-jnp.arange(half, dtype=F32) / half)
    ang = pos.astype(F32)[:, None] * inv[None, :]
    cos, sin = jnp.cos(ang), jnp.sin(ang)
    zeros = jnp.zeros((pos.shape[0], HEAD_DIM - ROT_DIM), F32)
    zero_half = jnp.zeros_like(sin)
    cos_head = jnp.concatenate([cos, cos, zeros + 1.0], axis=1)
    lo_head = jnp.concatenate([-sin, zero_half, zeros], axis=1)
    hi_head = jnp.concatenate([zero_half, sin, zeros], axis=1)
    two = lambda t: jnp.concatenate([t, t], axis=1)
    return two(cos_head), two(lo_head), two(hi_head)


def _attn_kernel(*refs, seq, with_memkv):
    n_groups = len(DILATIONS)
    q_refs, k_refs, v_refs = refs[0:n_groups], refs[n_groups:2 * n_groups], refs[2 * n_groups:3 * n_groups]
    n_job = 2 if with_memkv else 0
    att_refs = refs[3 * n_groups + n_job:4 * n_groups + n_job]
    o_s, m_s, den_s = refs[4 * n_groups + 2 * n_job:]
    blk = WINDOW_STEPS
    x_idx = lax.broadcasted_iota(jnp.int32, (blk, 2 * blk), 0)
    k_idx = lax.broadcasted_iota(jnp.int32, (blk, 2 * blk), 1)
    band_mask = (k_idx >= x_idx) & (k_idx <= x_idx + blk)
    causal_mask = (lax.broadcasted_iota(jnp.int32, (blk, blk), 1)
                   <= lax.broadcasted_iota(jnp.int32, (blk, blk), 0))
    half_masks = [_head_half_mask(LANES, hh) for hh in range(2)]

    def batch(g, dil, blocks):
        def rows(r, first_block, n):
            start = r + dil * blk * first_block
            return pl.ds(start, n) if dil == 1 else pl.ds(start, n, stride=dil)

        first = blocks[0][1] == 0
        mask = causal_mask if first else band_mask
        scores, values = [], []
        for r, c in blocks:
            k_rows = rows(r, 0, blk) if first else rows(r, c - 1, 2 * blk)
            qp = q_refs[g][0, rows(r, c, blk), :]
            kp = k_refs[g][0, k_rows, :].astype(BF16)
            values.append(v_refs[g][0, k_rows, :].astype(BF16))
            for hh in range(2):
                qh = jnp.where(half_masks[hh], qp, 0.0).astype(BF16)
                s = lax.dot_general(qh, kp, (((1,), (1,)), ((), ())), preferred_element_type=F32)
                scores.append(jnp.where(mask, s, NEG))
        s_all = jnp.concatenate(scores, axis=0)
        m = jnp.max(s_all, axis=-1, keepdims=True)
        p = jnp.exp2(s_all - m)
        den = jnp.sum(p, axis=-1, keepdims=True)
        p = p.astype(BF16)
        for n, (r, c) in enumerate(blocks):
            piece = lambda t, hh: t[(2 * n + hh) * blk:(2 * n + hh + 1) * blk]
            outs = [jnp.dot(piece(p, hh), values[n], preferred_element_type=F32) for hh in range(2)]
            o_s[g, rows(r, c, blk), :] = jnp.where(half_masks[0], outs[0], outs[1])
            m_s[g, rows(r, c, blk), :] = jnp.where(half_masks[0], piece(m, 0), piece(m, 1))
            den_s[g, rows(r, c, blk), :] = jnp.where(half_masks[0], piece(den, 0), piece(den, 1))

    for g, dil in enumerate(DILATIONS):
        n_blocks = seq // dil // blk
        first_blocks = [(r, 0) for r in range(dil)]
        later_blocks = [(r, c) for r in range(dil) for c in range(1, n_blocks)]
        per_first = ATTN_TILES_PER_BATCH // 2
        per_later = ATTN_TILES_PER_BATCH // 4
        for i in range(0, len(first_blocks), per_first):
            batch(g, dil, first_blocks[i:i + per_first])
        for i in range(0, len(later_blocks), per_later):
            batch(g, dil, later_blocks[i:i + per_later])

    if with_memkv:
        _memkv_rows(*refs[3 * n_groups:3 * n_groups + n_job], *refs[4 * n_groups + n_job:4 * n_groups + 2 * n_job])

    for t0 in range(0, seq, ATTN_MIX_ROWS):
        rws = slice(t0, t0 + ATTN_MIX_ROWS)
        tops = [m_s[g, rws, :] for g in range(n_groups)]
        peak = functools.reduce(jnp.maximum, tops)
        scales = [jnp.exp2(t - peak) for t in tops]
        inv = 1.0 / functools.reduce(jnp.add, [den_s[g, rws, :] * scales[g] for g in range(n_groups)])
        for g in range(n_groups):
            att_refs[g][0, rws, :] = (o_s[g, rws, :] * (scales[g] * inv)).astype(att_refs[g].dtype)


def _prompt_attention(q, kvs, memkv_job=None):
    bk, tk, _ = q.shape
    n_groups = len(DILATIONS)
    pairs = GROUP_WIDTH // LANES
    col = lambda first: pl.BlockSpec((1, tk, LANES), lambda i, j: (i, 0, first + j))
    out = jax.ShapeDtypeStruct((bk, tk, GROUP_WIDTH), BF16)
    job_in, job_out, job_shapes = _memkv_specs(*memkv_job, n_parts=pairs) if memkv_job else ([], [], [])
    return pl.pallas_call(
        functools.partial(_attn_kernel, seq=tk, with_memkv=bool(memkv_job)),
        grid=(bk, pairs),
        in_specs=([col(g * pairs) for g in range(n_groups)]
                  + [col(0)] * n_groups
                  + [col(pairs)] * n_groups
                  + job_in),
        out_specs=[col(0)] * n_groups + job_out,
        out_shape=[out] * n_groups + job_shapes,
        scratch_shapes=[pltpu.VMEM((n_groups, tk, LANES), F32)] * 3,
        compiler_params=_params(2),
        name="attn",
    )(*([q] * n_groups), *kvs, *kvs, *(memkv_job or ()))


def _sample_attn_kernel(q_ref, kvn0_ref, kvn1_ref, kvn2_ref, c0_ref, c1_ref, c2_ref, att_ref, *, dec_seq):
    kvn_refs = (kvn0_ref, kvn1_ref, kvn2_ref)
    cache_refs = (c0_ref, c1_ref, c2_ref)
    seqs_per_slab = SUBLANES // dec_seq
    n_groups = len(DILATIONS)
    widest = DILATIONS[-1]
    slab = pl.ds(pl.multiple_of(pl.program_id(0) * SUBLANES, SUBLANES), SUBLANES)
    lane = lax.broadcasted_iota(jnp.int32, (1, LANES), 1)
    row8 = lax.broadcasted_iota(jnp.int32, (SUBLANES, LANES), 0)
    sel_rows = 2 * SUBLANES
    sel_row = lax.broadcasted_iota(jnp.int32, (sel_rows, LANES), 0)
    sel_lane = lax.broadcasted_iota(jnp.int32, (sel_rows, LANES), 1)
    query_class = lane % widest

    def one_hot(cond):
        return jnp.where(cond, 1.0, 0.0).astype(BF16)

    def spread(rows8, sel):
        padded = jnp.concatenate([rows8, jnp.zeros_like(rows8)], axis=0).astype(BF16)
        return lax.dot_general(padded, sel, (((0,), (0,)), ((), ())), preferred_element_type=F32)

    def class_fold(x, period, op):
        shift = LANES // 2
        while shift >= period:
            x = op(x, pltpu.roll(x, shift, 1))
            shift //= 2
        return x

    def valid_mask(n_tiles, cache_ok, new_ok):
        rows = SUBLANES * (n_tiles // SUBLANES + 1)
        r = lax.broadcasted_iota(jnp.int32, (rows, LANES), 0)
        p = lax.broadcasted_iota(jnp.int32, (rows, LANES), 1)
        return ((r < n_tiles) & cache_ok(p)) | ((r == n_tiles) & new_ok(p))

    def attend(qm, cache_ref, seq, head, knt, vnt, n_tiles, valid, period):
        groups = []
        for t0 in range(0, n_tiles + 1, SUBLANES):
            acc = jnp.full((SUBLANES, LANES), NEG, F32)
            for t in range(t0, min(t0 + SUBLANES, n_tiles + 1)):
                keys = knt if t == n_tiles else cache_ref[0, seq, 0, head, :, t * LANES:(t + 1) * LANES]
                acc = jnp.where(row8 == t - t0, jnp.sum(qm * keys, axis=0, keepdims=True), acc)
            groups.append(acc)
        scores = jnp.where(valid, groups[0] if len(groups) == 1 else jnp.concatenate(groups, axis=0), NEG)
        top = jnp.broadcast_to(jnp.max(scores, axis=0, keepdims=True), (SUBLANES, LANES))
        if period:
            top = class_fold(top, period, jnp.maximum)
        else:
            top = jnp.broadcast_to(jnp.max(top, axis=1, keepdims=True), (SUBLANES, LANES))
        probs = jnp.exp2(scores - jnp.concatenate([top] * len(groups), axis=0))
        den = jnp.broadcast_to(jnp.sum(probs, axis=0, keepdims=True), (SUBLANES, LANES))
        num = jnp.zeros((HEAD_DIM, LANES), F32)
        for t in range(n_tiles + 1):
            vals = vnt if t == n_tiles else cache_ref[0, seq, 1, head, :, t * LANES:(t + 1) * LANES]
            num = num + vals * probs[t:t + 1, :]
        if period:
            return class_fold(num, period, jnp.add), top, class_fold(den, period, jnp.add)
        return (jnp.broadcast_to(jnp.sum(num, axis=1, keepdims=True), (HEAD_DIM, LANES)), top,
                jnp.broadcast_to(jnp.sum(den, axis=1, keepdims=True), (SUBLANES, LANES)))

    def tiles(wide):
        return [wide[:, n * LANES:(n + 1) * LANES] for n in range(wide.shape[1] // LANES)]

    sel_new = jnp.concatenate([one_hot((sel_row == e * dec_seq + sel_lane) & (sel_lane < dec_seq))
                               for e in range(seqs_per_slab)], axis=1)
    spreads = {}
    for pair in range(GROUP_WIDTH // LANES):
        off = pair * LANES
        for g, dil in enumerate(DILATIONS):
            if dil == 1:
                sels = [one_hot(sel_row == e * dec_seq + i) for e in range(seqs_per_slab) for i in range(dec_seq)]
            else:
                sels = [one_hot((sel_row == e * dec_seq + sel_lane % dil) & (sel_lane % dil < dec_seq))
                        for e in range(seqs_per_slab)]
            per_seq = len(sels) // seqs_per_slab
            q_t = tiles(spread(q_ref[slab, g * GROUP_WIDTH + off:g * GROUP_WIDTH + off + LANES],
                               jnp.concatenate(sels, axis=1)))
            k_t = tiles(spread(kvn_refs[g][0, slab, off:off + LANES], sel_new))
            v_t = tiles(spread(kvn_refs[g][0, slab, GROUP_WIDTH + off:GROUP_WIDTH + off + LANES], sel_new))
            for e in range(seqs_per_slab):
                spreads[pair, g, e] = (q_t[e * per_seq:(e + 1) * per_seq], k_t[e], v_t[e])

    for pair in range(GROUP_WIDTH // LANES):
        mixed = [[None] * seqs_per_slab for _ in range(n_groups)]
        for e in range(seqs_per_slab):
            per_group = []
            for g, dil in enumerate(DILATIONS):
                qms, knt, vnt = spreads[pair, g, e]
                n_tiles = cache_refs[g].shape[5] // LANES
                if dil == 1:
                    valids = [valid_mask(n_tiles, lambda p, i=i: p >= i, lambda p, i=i: p <= i) for i in range(dec_seq)]
                else:
                    valids = [valid_mask(n_tiles, lambda p, dil=dil: p % dil < dec_seq, lambda p: p < dec_seq)]
                heads = []
                for hh in range(2):
                    hrows = slice(hh * HEAD_DIM, (hh + 1) * HEAD_DIM)
                    num = top = den = None
                    for i, (qm, valid) in enumerate(zip(qms, valids)):
                        n_i, t_i, d_i = attend(qm[hrows], cache_refs[g], e, 2 * pair + hh, knt[hrows], vnt[hrows],
                                               n_tiles, valid, dil if dil > 1 else 0)
                        if num is None:
                            num, top, den = n_i, t_i, d_i
                        else:
                            mine = query_class == i
                            num, top, den = jnp.where(mine, n_i, num), jnp.where(mine, t_i, top), jnp.where(mine, d_i, den)
                    heads.append([num, top, den])
                per_group.append(heads)
            for hh in range(2):
                tops = [per_group[g][hh][1] for g in range(n_groups)]
                peak = functools.reduce(jnp.maximum, tops)
                scales = [jnp.exp2(t - peak) for t in tops]
                total = functools.reduce(jnp.add, [per_group[g][hh][2] * scales[g] for g in range(n_groups)])
                for g in range(n_groups):
                    per_group[g][hh] = per_group[g][hh][0] * (scales[g] / total)[0:1, :]
            for g in range(n_groups):
                mixed[g][e] = jnp.concatenate(per_group[g], axis=0)
        sel_out = one_hot(sel_lane == widest * (sel_row // dec_seq) + sel_row % dec_seq)
        halves = []
        for g in range(n_groups):
            both = mixed[g][0]
            for e in range(1, seqs_per_slab):
                both = jnp.where((lane >= widest * e) & (lane < widest * (e + 1)), mixed[g][e], both)
            hi = both.astype(BF16)
            halves += [hi, (both - hi.astype(F32)).astype(BF16)]
        rows = tiles(lax.dot_general(sel_out, jnp.concatenate(halves, axis=0), (((1,), (1,)), ((), ())),
                                     preferred_element_type=F32))
        for g in range(n_groups):
            att_ref[slab, g * GROUP_WIDTH + pair * LANES:g * GROUP_WIDTH + (pair + 1) * LANES] = (
                rows[2 * g] + rows[2 * g + 1])[:SUBLANES]


def _sample_attention(q, kvns, caches_t, layer, dec_seq):
    rows, width = q.shape
    seqs_per_slab = SUBLANES // dec_seq
    cache_spec = lambda c: pl.BlockSpec((1, seqs_per_slab) + c.shape[2:], lambda i: (layer, i, 0, 0, 0, 0))
    return pl.pallas_call(
        functools.partial(_sample_attn_kernel, dec_seq=dec_seq),
        grid=(rows // SUBLANES,),
        in_specs=[_const_spec(q.shape)] + [_const_spec(k.shape) for k in kvns] + [cache_spec(c) for c in caches_t],
        out_specs=pl.BlockSpec((rows, width), lambda i: (0, 0)),
        out_shape=jax.ShapeDtypeStruct((rows, width), F32),
        compiler_params=_params(1),
        name="sample_attn",
    )(q, *kvns, *caches_t)


def _mix_kernel(*refs, tm, alpha, n_att, cross_attend, cast_scales):
    n_cast = len(cast_scales)
    u_ref, gv_ref, x_ref, wsp_ref, bsp_ref, wm_ref, g_ref, b_ref = refs[n_att:n_att + 8]
    n_cross = 5 if cross_attend else 0
    xattn_refs = refs[n_att + 8:n_att + 8 + n_cross]
    cast_in_refs = refs[n_att + 8 + n_cross:n_att + 8 + n_cross + n_cast]
    out_ref = refs[n_att + 8 + n_cross + n_cast]
    _side_cast(cast_in_refs, refs[n_att + 9 + n_cross + n_cast:], cast_scales)
    t_idx = lax.broadcasted_iota(jnp.int32, (SGU_CHUNK, SGU_CHUNK), 0)
    s_idx = lax.broadcasted_iota(jnp.int32, (SGU_CHUNK, SGU_CHUNK), 1)
    w_sp = [jnp.where(s_idx <= t_idx, wsp_ref[k], 0.0).astype(BF16) for k in range(wsp_ref.shape[0])]
    sgu_width = gv_ref.shape[2]

    def gated_rows(r0, n_rows):
        starts = range(r0, r0 + n_rows, SGU_CHUNK)
        per_pair = []
        for pair in range(sgu_width // LANES):
            gps = [gv_ref[0, c:c + SGU_CHUNK, pair * LANES:(pair + 1) * LANES] for c in starts]
            mixed = None
            for hh in range(2):
                wide = jnp.concatenate([jnp.where(_head_half_mask(LANES, hh), gp, 0.0).astype(BF16) for gp in gps], axis=1)
                part = jnp.dot(w_sp[2 * pair + hh], wide, preferred_element_type=F32)
                mixed = part if mixed is None else mixed + part
            per_pair.append(mixed)
        mixed = jnp.concatenate(
            [jnp.concatenate([m[:, n * LANES:(n + 1) * LANES] for m in per_pair], axis=1) for n in range(len(starts))],
            axis=0)
        bias = jnp.concatenate([bsp_ref[...]] * len(starts), axis=0)
        return (u_ref[0, r0:r0 + n_rows, :] * (mixed + bias)).astype(BF16)

    def mixed_rows(r0, n_rows):
        rws = slice(r0, r0 + n_rows)
        lhs = jnp.concatenate([a[0, rws, :].astype(BF16) for a in refs[:n_att]] + [gated_rows(r0, n_rows)], axis=1)
        return alpha * x_ref[0, rws, :] + jnp.dot(lhs, wm_ref[...], preferred_element_type=F32)

    sub = min(tm, MIX_SUBTILE)
    starts = list(range(0, tm, sub))
    ln1 = lambda y: _layer_norm(y, g_ref[...], b_ref[...])
    if not cross_attend:
        for r0 in starts:
            out_ref[0, r0:r0 + sub, :] = ln1(mixed_rows(r0, sub))
        return
    mkv_ref, wq_ref, wo_ref, g2_ref, b2_ref = xattn_refs
    ys = [mixed_rows(r0, sub) for r0 in starts]
    x1s, qxs = [], []
    for y in ys:
        x1s.append(ln1(y))
        qxs.append(jnp.dot(x1s[-1].astype(BF16), wq_ref[...], preferred_element_type=F32))
    pending = None
    for r0, x1, qx in zip(starts, x1s, qxs):
        heads = _memory_attention(qx, mkv_ref)
        if pending is not None:
            out_ref[0, pending[0]:pending[0] + sub, :] = _layer_norm(pending[1], g2_ref[...], b2_ref[...])
        pending = (r0, alpha * x1 + jnp.dot(heads, wo_ref[...], preferred_element_type=F32))
    out_ref[0, pending[0]:pending[0] + sub, :] = _layer_norm(pending[1], g2_ref[...], b2_ref[...])


def _mix(att_inputs, u, gv, x, w_sp, b_sp_tile, w_mix_b, layer, g, b, tm, alpha, cross=None, casts=()):
    bk, tk, d = x.shape
    sgu_width = u.shape[2]
    grid = (bk, tk // tm)
    row = lambda w: pl.BlockSpec((1, tm, w), lambda i, j: (i, j, 0))
    cross_specs = []
    if cross:
        mkv = cross[0]
        cross_specs = ([pl.BlockSpec((None, 1) + mkv.shape[2:], lambda i, j: (layer, i, 0, 0))]
                       + [_layer_spec(p, layer) for p in cross[1:]])
    cast_specs, cast_shapes = _side_cast_specs(casts, grid)
    return pl.pallas_call(
        functools.partial(_mix_kernel, tm=tm, alpha=alpha, n_att=len(att_inputs), cross_attend=bool(cross),
                          cast_scales=tuple(s for _, s in casts)),
        grid=grid,
        in_specs=[row(a.shape[2]) for a in att_inputs] + [
            row(sgu_width), row(sgu_width), row(d),
            _const_spec(w_sp.shape), _const_spec(b_sp_tile.shape), _layer_spec(w_mix_b, layer),
            _layer_spec(g, layer), _layer_spec(b, layer)] + cross_specs + cast_specs,
        out_specs=[row(d)] + cast_specs,
        out_shape=[jax.ShapeDtypeStruct((bk, tk, d), F32)] + cast_shapes,
        compiler_params=_params(2),
        name="mix_xattn" if cross else "mix",
    )(*att_inputs, u, gv, x, w_sp, b_sp_tile, w_mix_b, g, b, *(cross or ()), *(w for w, _ in casts))


def _mem_row_stride(d):
    return 2 * (d // MEM_HEADS // LANES) * MEM_HEADS


def _memkv_rows(m_ref, w_ref, flat_ref, out_b_ref):
    depth, _, n_mem, width = out_b_ref.shape
    d = width // 2
    hd = d // MEM_HEADS
    n_chunks = hd // LANES
    mb = m_ref[0].astype(BF16)
    for layer in range(depth):
        mkv = jnp.dot(mb, w_ref[layer], preferred_element_type=F32)
        out_b_ref[layer, 0] = mkv.astype(BF16)
        for kv in range(2):
            for c in range(n_chunks):
                for h in range(MEM_HEADS):
                    col = kv * d + h * hd + c * LANES
                    rows = pl.ds((kv * n_chunks + c) * MEM_HEADS + h, n_mem, stride=_mem_row_stride(d))
                    flat_ref[layer, rows, :] = mkv[:, col:col + LANES]


def _memkv_specs(mem, w_xkv_b, n_parts):
    bk, n_mem, d = mem.shape
    depth, _, width = w_xkv_b.shape
    part = n_mem // n_parts
    return ([pl.BlockSpec((1, part, d), lambda i, j: (i, j, 0)), _const_spec(w_xkv_b.shape)],
            [pl.BlockSpec((depth, None, part * _mem_row_stride(d), LANES), lambda i, j: (0, i, j, 0)),
             pl.BlockSpec((depth, 1, part, width), lambda i, j: (0, i, j, 0))],
            [jax.ShapeDtypeStruct((depth, bk, n_mem * _mem_row_stride(d), LANES), F32),
             jax.ShapeDtypeStruct((depth, bk, n_mem, width), BF16)])


def _memory_attention(qx, mkv_ref):
    d = qx.shape[1]
    hd = d // MEM_HEADS
    heads = []
    for h in range(MEM_HEADS):
        qh = qx[:, h * hd:(h + 1) * hd].astype(BF16)
        kh = mkv_ref[0, :, h * hd:(h + 1) * hd]
        vh = mkv_ref[0, :, d + h * hd:d + (h + 1) * hd]
        s = lax.dot_general(qh, kh, (((1,), (1,)), ((), ())), preferred_element_type=F32)
        m = jnp.max(s, axis=-1, keepdims=True)
        p = jnp.exp(s - m)
        inv = 1.0 / jnp.sum(p, axis=-1, keepdims=True)
        heads.append((jnp.dot(p.astype(BF16), vh, preferred_element_type=F32) * inv).astype(BF16))
    return jnp.concatenate(heads, axis=1)


def _sample_xattn_kernel(x_ref, mkv_ref, wq_ref, wo_ref, g_ref, b_ref, out_ref, qx_ref, ox_ref,
                         *, dec_seq, seqs_per_step, alpha):
    step = pl.program_id(0)
    d = x_ref.shape[1]
    hd = d // MEM_HEADS
    seqs_per_slab = SUBLANES // dec_seq

    @pl.when(step == 0)
    def _():
        qx_ref[...] = jnp.dot(x_ref[...].astype(BF16), wq_ref[...], preferred_element_type=F32)

    n_chunks = hd // LANES
    rows_per_mem = 2 * n_chunks * MEM_HEADS
    n_mem = mkv_ref.shape[2] // rows_per_mem

    def head_matrix(b_local, kv, h):
        chunks = [mkv_ref[0, b_local, pl.ds((kv * n_chunks + c) * MEM_HEADS + h, n_mem, stride=rows_per_mem), :]
                  for c in range(n_chunks)]
        return jnp.concatenate(chunks, axis=1).astype(BF16)

    slab_row = lax.broadcasted_iota(jnp.int32, (SUBLANES, 1), 0)
    slabs = [pl.ds(pl.multiple_of((step * (seqs_per_step // seqs_per_slab) + j) * SUBLANES, SUBLANES), SUBLANES)
             for j in range(seqs_per_step // seqs_per_slab)]
    units = [(j, h, e) for j in range(len(slabs)) for h in range(MEM_HEADS) for e in range(seqs_per_slab)]
    scores = []
    for j, h, e in units:
        if e == 0:
            q8 = qx_ref[slabs[j], h * hd:(h + 1) * hd]
            qh = jnp.concatenate([q8, jnp.zeros_like(q8)], axis=0).astype(BF16)
        scores.append(lax.dot_general(head_matrix(j * seqs_per_slab + e, 0, h), qh, (((1,), (1,)), ((), ())),
                                      preferred_element_type=F32))
    s_all = jnp.stack(scores)
    p_all = jnp.exp(s_all - jnp.max(s_all, axis=1, keepdims=True))
    p_all = (p_all * (1.0 / jnp.sum(p_all, axis=1, keepdims=True))).astype(BF16)
    o8 = None
    for n, (j, h, e) in enumerate(units):
        oh = lax.dot_general(p_all[n], head_matrix(j * seqs_per_slab + e, 1, h), (((0,), (0,)), ((), ())),
                             preferred_element_type=F32)
        o8 = oh[:SUBLANES] if e == 0 else jnp.where(slab_row // dec_seq == e, oh[:SUBLANES], o8)
        if e == seqs_per_slab - 1:
            ox_ref[slabs[j], h * hd:(h + 1) * hd] = o8

    @pl.when(step == pl.num_programs(0) - 1)
    def _():
        y = alpha * x_ref[...] + jnp.dot(ox_ref[...].astype(BF16), wo_ref[...], preferred_element_type=F32)
        out_ref[...] = _layer_norm(y, g_ref[...], b_ref[...])


def _sample_xattn(x, mkv_flat, layer, wq_b, wo_b, g, b, dec_seq, seqs_per_step, alpha):
    rows, d = x.shape
    _, n_seq, flat_rows, _ = mkv_flat.shape
    return pl.pallas_call(
        functools.partial(_sample_xattn_kernel, dec_seq=dec_seq, seqs_per_step=seqs_per_step, alpha=alpha),
        grid=(n_seq // seqs_per_step,),
        in_specs=[_const_spec(x.shape),
                  pl.BlockSpec((1, seqs_per_step, flat_rows, LANES), lambda i: (layer, i, 0, 0)),
                  _layer_spec(wq_b, layer), _layer_spec(wo_b, layer), _layer_spec(g, layer), _layer_spec(b, layer)],
        out_specs=pl.BlockSpec((rows, d), lambda i: (0, 0)),
        out_shape=jax.ShapeDtypeStruct((rows, d), F32),
        scratch_shapes=[pltpu.VMEM((rows, d), F32), pltpu.VMEM((rows, d), F32)],
        compiler_params=_params(1),
        name="sample_xattn",
    )(x, mkv_flat, wq_b, wo_b, g, b)


def _mlp_kernel(*refs, ff_chunk, alpha, windows, first_layer):
    n_win = len(windows)
    x_ref, wu_ref, wd_ref, g_ref, b_ref = refs[:5]
    kv_refs = refs[5:5 + n_win]
    out_ref = refs[-1 - n_win]
    window_refs = list(refs[len(refs) - n_win:])

    def write_window(grp):
        win_ref = window_refs[grp] if first_layer is None else window_refs[grp].at[first_layer]
        for part in range(2):
            t = kv_refs[grp][0, :, part * GROUP_WIDTH:(part + 1) * GROUP_WIDTH].T
            for h in range(HEADS_PER_GROUP):
                win_ref[part, h] = t[h * HEAD_DIM:(h + 1) * HEAD_DIM, :]
        if first_layer is not None:
            for other in range(window_refs[grp].shape[0]):
                if other != first_layer:
                    window_refs[grp][other] = jnp.zeros(window_refs[grp].shape[1:], F32)

    for grp, every_tile in enumerate(windows):
        if not every_tile:
            pl.when(pl.program_id(1) == pl.num_programs(1) - 1)(functools.partial(write_window, grp))
    for grp, every_tile in enumerate(windows):
        if every_tile:
            write_window(grp)

    tm = x_ref.shape[1]
    sub = min(tm, MLP_SUBTILE)
    for r0 in range(0, tm, sub):
        x = x_ref[0, r0:r0 + sub, :]
        xb = x.astype(BF16)
        y = alpha * x
        for c in range(wu_ref.shape[1] // ff_chunk):
            h = jnp.dot(xb, wu_ref[:, c * ff_chunk:(c + 1) * ff_chunk], preferred_element_type=F32)
            h = jnp.square(jnp.maximum(h, 0.0)).astype(BF16)
            y = y + jnp.dot(h, wd_ref[c * ff_chunk:(c + 1) * ff_chunk, :], preferred_element_type=F32)
        out_ref[0, r0:r0 + sub, :] = _layer_norm(y, g_ref[...], b_ref[...])


def _mlp(x, wu_b, wd_b, layer, g, b, tm, alpha, kvs=(), window_keeps=(), window_bufs=None):
    bk, tk, d = x.shape
    depth = wu_b.shape[0]
    row = pl.BlockSpec((1, tm, d), lambda i, j: (i, j, 0))
    aliased = list(window_bufs or ())
    owns_all_layers = bool(window_keeps) and not aliased
    lead, at = (depth, 0) if owns_all_layers else (None, layer)
    kv_specs, out_shape, out_specs, windows = [], [jax.ShapeDtypeStruct((bk, tk, d), F32)], [row], []
    for kv, keep in zip(kvs, window_keeps):
        every_tile = keep == tk
        assert every_tile or tk % keep == 0
        windows.append(every_tile)
        n = tm if every_tile else keep
        kv_specs.append(pl.BlockSpec((1, n, kv.shape[2]),
                                     (lambda i, j: (i, j, 0)) if every_tile else (lambda i, j, last=tk // keep - 1: (i, last, 0))))
        out_shape.append(jax.ShapeDtypeStruct((depth, bk, 2, HEADS_PER_GROUP, HEAD_DIM, keep), F32))
        out_specs.append(pl.BlockSpec((lead, None, 2, HEADS_PER_GROUP, HEAD_DIM, n),
                                      (lambda i, j: (at, i, 0, 0, 0, j)) if every_tile else (lambda i, j: (at, i, 0, 0, 0, 0))))
    n_in = 5 + len(kv_specs)
    return pl.pallas_call(
        functools.partial(_mlp_kernel, ff_chunk=MLP_FF_CHUNK, alpha=alpha, windows=tuple(windows),
                          first_layer=layer if owns_all_layers else None),
        grid=(bk, tk // tm),
        in_specs=[row, _layer_spec(wu_b, layer), _layer_spec(wd_b, layer), _layer_spec(g, layer), _layer_spec(b, layer)]
        + kv_specs + [pl.BlockSpec(memory_space=pl.ANY)] * len(aliased),
        out_specs=out_specs,
        out_shape=out_shape,
        input_output_aliases={n_in + k: 1 + k for k in range(len(aliased))},
        compiler_params=_params(2),
        name="mlp",
    )(x, wu_b, wd_b, g, b, *kvs, *aliased)


def kernel(x_prompt, x_sample, cache_kv_w128, cache_kv_w512, cache_kv_w2048, cache_mem_kv, mem_prompt,
           w_in, sgu_ln_g, sgu_ln_b, w_spatial, b_spatial, w_mix_out, ln1_g, ln1_b,
           w_xq, w_xkv, w_xo, ln2_g, ln2_b, w_up, w_down, ln3_g, ln3_b):
    depth = w_in.shape[0]
    bp, tp, d = x_prompt.shape
    bs, ts, _ = x_sample.shape
    past_len = 8192
    alpha = float((2 * depth) ** 0.25)
    att_width = len(DILATIONS) * GROUP_WIDTH
    sgu_width = sgu_ln_g.shape[1]
    sgu_groups = w_spatial.shape[1]
    n_mem = mem_prompt.shape[1]
    rows_s = bs * ts
    assert tp % (DILATIONS[-1] * WINDOW_STEPS) == 0 and SUBLANES % ts == 0 and rows_s % SGU_CHUNK == 0
    assert d // MEM_HEADS == GROUP_WIDTH and sgu_width == GROUP_WIDTH

    col_scale = jnp.concatenate([jnp.full((att_width,), HEAD_DIM ** -0.5 * math.log2(math.e), F32),
                                 jnp.ones((w_in.shape[2] - att_width,), F32)])
    w_in_b = (w_in * col_scale).astype(BF16)
    w_xkv_b = w_mix_b = w_xq_b = w_xo_b = w_up_b = w_down_b = None
    vec = lambda p: p[:, None, :]
    sgu_g, sgu_b, g1, b1, g2, b2, g3, b3 = map(vec, (sgu_ln_g, sgu_ln_b, ln1_g, ln1_b, ln2_g, ln2_b, ln3_g, ln3_b))

    tables_p = _rope_tables(jnp.arange(tp, dtype=jnp.int32))
    tables_s = _rope_tables(past_len + (jnp.arange(rows_s, dtype=jnp.int32) % ts))

    row_s = jnp.arange(rows_s, dtype=jnp.int32)
    pick = (row_s[:, None] % ts == jnp.arange(ts)[None, :]).astype(F32)
    same_seq = (row_s[:, None] // ts == row_s[None, :] // ts).astype(F32)
    w_sp_sample = jnp.einsum("ra,lgab,cb->lgrc", pick, w_spatial[:, :, :ts, :ts], pick,
                             precision=lax.Precision.HIGHEST) * same_seq

    hp = x_prompt
    hs = x_sample.reshape(1, rows_s, d)
    out_rows_s = [[] for _ in DILATIONS]
    out_gv = []
    window_keeps = tuple(min(dil * WINDOW_STEPS, tp) for dil in DILATIONS)
    windows = mem_out = mkv = None
    caches_t = [jnp.transpose(c, (0, 1, 3, 4, 5, 2)) for c in (cache_kv_w128, cache_kv_w512, cache_kv_w2048)]
    mem_chunks = d // MEM_HEADS // LANES
    mem_flat = (cache_mem_kv.reshape(depth, bs, n_mem, 2, MEM_HEADS, mem_chunks, LANES)
                .transpose(0, 1, 2, 3, 5, 4, 6).reshape(depth, bs, n_mem * 2 * mem_chunks * MEM_HEADS, LANES))
    for l in range(depth):
        first = l == 0
        q, kv0, kv1, kv2, u, gv, *rounded = _inproj(
            hp, w_in_b, l, tables_p, sgu_g, sgu_b, tm=INPROJ_ROWS, gate_dtype=BF16,
            casts=((w_mix_out, 1.0), (w_xq, (d // MEM_HEADS) ** -0.5), (w_xo, 1.0), (w_xkv, 1.0)) if first else ())
        if first:
            w_mix_b, w_xq_b, w_xo_b, w_xkv_b = rounded
            *atts, mem_out, mkv = _prompt_attention(q, [kv0, kv1, kv2], memkv_job=(mem_prompt, w_xkv_b))
        else:
            atts = _prompt_attention(q, [kv0, kv1, kv2])
        b_tile = jnp.repeat(b_spatial[l][:, :SGU_CHUNK].T, sgu_width // sgu_groups, axis=1)
        hp, *rounded = _mix(atts, u, gv, hp, w_spatial[l], b_tile, w_mix_b, l, g1, b1, tm=MIX_ROWS, alpha=alpha,
                            cross=(mkv, w_xq_b, w_xo_b, g2, b2),
                            casts=((w_up, 1.0), (w_down, 1.0)) if first else ())
        if first:
            w_up_b, w_down_b = rounded
        hp, *windows = _mlp(hp, w_up_b, w_down_b, l, g3, b3, tm=MLP_ROWS, alpha=alpha,
                            kvs=(kv0, kv1, kv2), window_keeps=window_keeps, window_bufs=windows)

        q, kv0, kv1, kv2, u, gv = _inproj(hs, w_in_b, l, tables_s, sgu_g, sgu_b, tm=rows_s, gate_dtype=F32)
        out_gv.append(gv.reshape(bs, ts, sgu_width))
        for g, kv in enumerate((kv0, kv1, kv2)):
            out_rows_s[g].append(kv.reshape(bs, ts, 2, HEADS_PER_GROUP, HEAD_DIM))
        att = _sample_attention(q[0], (kv0, kv1, kv2), caches_t, l, dec_seq=ts)
        b_tile_s = jnp.repeat(jnp.tile(b_spatial[l][:, :ts].T, (rows_s // ts, 1)), sgu_width // sgu_groups, axis=1)
        hs, = _mix([att[None]], u, gv, hs, w_sp_sample[l], b_tile_s, w_mix_b, l, g1, b1, tm=rows_s, alpha=alpha)
        hs2 = _sample_xattn(hs[0], mem_flat, l, w_xq_b, w_xo_b, g2, b2, dec_seq=ts,
                            seqs_per_step=SAMPLE_XATTN_SEQS, alpha=alpha)
        hs, = _mlp(hs2[None], w_up_b, w_down_b, l, g3, b3, tm=rows_s, alpha=alpha)

    rows_p = [jnp.transpose(w, (0, 1, 5, 2, 3, 4)) for w in windows]
    mem_p = (mem_out.reshape(depth, bp, n_mem, 2, mem_chunks, MEM_HEADS, LANES)
             .transpose(0, 1, 2, 3, 5, 4, 6).reshape(depth, bp, n_mem, 2, MEM_HEADS, d // MEM_HEADS))
    stack = lambda xs: jnp.stack(xs)
    return (hp, hs.reshape(bs, ts, d), rows_p[0], rows_p[1], rows_p[2], mem_p,
            stack(out_rows_s[0]), stack(out_rows_s[1]), stack(out_rows_s[2]), stack(out_gv))
```
